```python
import math
import jax
import jax.numpy as jnp
from jax import lax
import numpy as np

D_MODEL = 4096
BATCH = 2
SEQ = 8192
DEPTH = 2

GRID_W = 64
CTX_LEN = 256
HEAD_DIM = 128
ROPE_THETA = 10000.0
Q_BLOCK = 128
MIX_W = D_MODEL // 4
A_HEADS = MIX_W // HEAD_DIM
A_KV_HEADS = A_HEADS // 4
B_HEADS = MIX_W // HEAD_DIM
B_KV_HEADS = B_HEADS // 4
WINDOW = 128
LRU_WIDTH = MIX_W
LRU_BLOCKS = 8
LRU_BW = LRU_WIDTH // LRU_BLOCKS
CONV_W = 4
LRU_C = 8.0
S5_WIDTH = MIX_W
S5_GROUP = 16
S5_GROUPS = S5_WIDTH // S5_GROUP
S5_STATE = 64
N_BRANCH = 4
BRANCH_W = MIX_W
PEER_HEADS = 8
PEER_KEYS = 128
PEER_EXPERTS = PEER_KEYS * PEER_KEYS
PEER_QDIM = 256
PEER_TOPK = 16
PEER_TOKEN_BLOCK = 64
DN_ALPHA = (2.0 * DEPTH) ** 0.25
DN_BETA = (8.0 * DEPTH) ** -0.25
LN_EPS = 1e-6
DIRECTIONS = (False, True)

IN_WIDTHS = (A_HEADS * HEAD_DIM, A_KV_HEADS * HEAD_DIM, A_KV_HEADS * HEAD_DIM,
             B_HEADS * HEAD_DIM, B_KV_HEADS * HEAD_DIM, B_KV_HEADS * HEAD_DIM,
             LRU_WIDTH, LRU_WIDTH, S5_WIDTH)
IN_WIDTH = sum(IN_WIDTHS)
IN_OFFSETS = tuple(sum(IN_WIDTHS[:i + 1]) for i in range(len(IN_WIDTHS) - 1))

kernel_name = 'hybrid_gated_attn_lru_s5_peer_dit'

f32 = jnp.float32


def rms_norm(x, gain):
    xf = x.astype(f32)
    y = xf * lax.rsqrt(jnp.mean(xf * xf, axis=-1, keepdims=True) + LN_EPS)
    return (y * gain.astype(f32)).astype(x.dtype)


def layer_norm(x, gain, bias):
    xf = x.astype(f32)
    mu = jnp.mean(xf, axis=-1, keepdims=True)
    xc = xf - mu
    var = jnp.mean(xc * xc, axis=-1, keepdims=True)
    return (xc * lax.rsqrt(var + LN_EPS) * gain.astype(f32) + bias.astype(f32)).astype(x.dtype)


def axial_rope(rows):
    t = jnp.arange(rows * GRID_W)
    row = (t // GRID_W).astype(f32)
    col = (t % GRID_W).astype(f32)
    n_freq = HEAD_DIM // 4
    inv = ROPE_THETA ** (-jnp.arange(n_freq, dtype=f32) / n_freq)
    ang = jnp.concatenate([row[:, None] * inv, col[:, None] * inv], axis=-1)
    return jnp.cos(ang), jnp.sin(ang)


def apply_rope(x, cos, sin):
    xf = x.astype(f32).reshape(x.shape[:-1] + (HEAD_DIM // 2, 2))
    x0, x1 = xf[..., 0], xf[..., 1]
    cs, sn = cos[None, :, None, :], sin[None, :, None, :]
    out = jnp.stack([x0 * cs - x1 * sn, x0 * sn + x1 * cs], axis=-1)
    return out.reshape(x.shape).astype(x.dtype)


def split_heads(t, n_heads):
    return t.reshape(t.shape[0], t.shape[1], n_heads, HEAD_DIM)


def gqa_scores(q, k):
    return jnp.einsum('bqhgd,bkhd->bhgqk', q, k, preferred_element_type=f32) * (HEAD_DIM ** -0.5)


def gqa_values(p, v):
    return jnp.einsum('bhgqk,bkhd->bqhgd', p.astype(v.dtype), v)


def mixer_a(q, k, v, qc, kc, vc, q_gain, k_gain, cos, sin, need_ctx):
    bsz, seq = q.shape[:2]
    grp = A_HEADS // A_KV_HEADS
    q = apply_rope(rms_norm(split_heads(q, A_HEADS), q_gain), cos, sin)
    k = apply_rope(rms_norm(split_heads(k, A_KV_HEADS), k_gain), cos, sin)
    kc = rms_norm(split_heads(kc, A_KV_HEADS), k_gain)
    vc = split_heads(vc, A_KV_HEADS)
    keys = jnp.concatenate([kc, k], axis=1)
    vals = jnp.concatenate([vc, split_heads(v, A_KV_HEADS)], axis=1)
    n_blk = seq // Q_BLOCK
    q_blocks = q.reshape(bsz, n_blk, Q_BLOCK, A_KV_HEADS, grp, HEAD_DIM).swapaxes(0, 1)

    def attend(qb):
        return gqa_values(jax.nn.softmax(gqa_scores(qb, keys), axis=-1), vals)

    y = lax.map(attend, q_blocks).swapaxes(0, 1).reshape(bsz, seq, A_HEADS * HEAD_DIM)
    if not need_ctx:
        return y, None
    qc = rms_norm(split_heads(qc, A_HEADS), q_gain).reshape(bsz, -1, A_KV_HEADS, grp, HEAD_DIM)
    yc = gqa_values(jax.nn.softmax(gqa_scores(qc, kc), axis=-1), vc)
    return y, yc.reshape(bsz, -1, A_HEADS * HEAD_DIM)


def mixer_b(q, k, v, qc, kc, vc, sink, cos, sin, need_ctx):
    bsz, seq = q.shape[:2]
    grp = B_HEADS // B_KV_HEADS
    span = 3 * Q_BLOCK
    n_blk = seq // Q_BLOCK
    q = apply_rope(split_heads(q, B_HEADS), cos, sin)
    k = apply_rope(split_heads(k, B_KV_HEADS), cos, sin)
    v = split_heads(v, B_KV_HEADS)
    kc = split_heads(kc, B_KV_HEADS)
    vc = split_heads(vc, B_KV_HEADS)
    sink = sink.astype(f32).reshape(B_KV_HEADS, grp)[None, :, :, None, None]

    def softmax_with_sink(s):
        s = jnp.concatenate([s, jnp.broadcast_to(sink, s.shape[:-1] + (1,))], axis=-1)
        return jax.nn.softmax(s, axis=-1)[..., :-1]

    pad = ((0, 0), (Q_BLOCK, Q_BLOCK), (0, 0), (0, 0))
    win = jnp.arange(n_blk)[:, None] * Q_BLOCK + jnp.arange(span)[None, :]
    k_win = jnp.pad(k, pad)[:, win].swapaxes(0, 1)
    v_win = jnp.pad(v, pad)[:, win].swapaxes(0, 1)
    q_pos = jnp.arange(n_blk)[:, None] * Q_BLOCK + jnp.arange(Q_BLOCK)[None, :]
    k_pos = (win - Q_BLOCK)[:, None, :]
    valid = (jnp.abs(k_pos - q_pos[:, :, None]) <= WINDOW) & (k_pos >= 0) & (k_pos < seq)
    q_blocks = q.reshape(bsz, n_blk, Q_BLOCK, B_KV_HEADS, grp, HEAD_DIM).swapaxes(0, 1)

    def attend(args):
        qb, kb, vb, mask = args
        s_loc = jnp.where(mask, gqa_scores(qb, kb), -jnp.inf)
        p = softmax_with_sink(jnp.concatenate([s_loc, gqa_scores(qb, kc)], axis=-1))
        return gqa_values(p[..., :span], vb) + gqa_values(p[..., span:], vc)

    y = lax.map(attend, (q_blocks, k_win, v_win, valid)).swapaxes(0, 1).reshape(bsz, seq, B_HEADS * HEAD_DIM)
    if not need_ctx:
        return y, None
    qc = split_heads(qc, B_HEADS).reshape(bsz, -1, B_KV_HEADS, grp, HEAD_DIM)
    yc = gqa_values(softmax_with_sink(gqa_scores(qc, kc)), vc)
    return y, yc.reshape(bsz, -1, B_HEADS * HEAD_DIM)


def centred_conv(x, w, b):
    left = CONV_W // 2
    seq = x.shape[1]
    xp = jnp.pad(x, ((0, 0), (left, CONV_W - 1 - left), (0, 0)))
    y = b
    for tap in range(CONV_W):
        y = y + xp[:, tap:tap + seq] * w[tap]
    return y


def real_combine(e1, e2):
    a1, b1 = e1
    a2, b2 = e2
    return a1 * a2, a2 * b1 + b2


def real_scan(a, b, h0, reverse):
    if reverse:
        a, b = jnp.flip(a, 1), jnp.flip(b, 1)
    b = b.at[:, 0].add(a[:, 0] * h0)
    _, h = lax.associative_scan(real_combine, (a, b), axis=1)
    return jnp.flip(h, 1) if reverse else h


def end_state(h, reverse):
    return h[:, 0] if reverse else h[:, -1]


def rglru_coeffs(x, w_r, b_r, w_i, b_i, lam):
    xb = x.reshape(x.shape[0], x.shape[1], LRU_BLOCKS, LRU_BW)
    r = jax.nn.sigmoid(jnp.einsum('blni,nij->blnj', xb, w_r).reshape(x.shape) + b_r)
    gi = jax.nn.sigmoid(jnp.einsum('blni,nij->blnj', xb, w_i).reshape(x.shape) + b_i)
    log_a = -LRU_C * r * jax.nn.softplus(-lam)
    return jnp.exp(log_a), jnp.sqrt(-jnp.expm1(2.0 * log_a)) * gi * x


def mixer_c(xr, gr, xrc, grc, conv_w, conv_b, w_r, b_r, w_i, b_i, lam, need_ctx):
    dt = xr.dtype
    cw, cb = conv_w.astype(f32), conv_b.astype(f32)
    xl = centred_conv(xr.astype(f32), cw, cb)
    xc = centred_conv(xrc.astype(f32), cw, cb)
    h_lat = jnp.zeros_like(xl)
    ctx_states = []
    for d, rev in enumerate(DIRECTIONS):
        prm = (w_r[d].astype(f32), b_r[d].astype(f32), w_i[d].astype(f32), b_i[d].astype(f32), lam[d].astype(f32))
        a_c, b_c = rglru_coeffs(xc, *prm)
        hc = real_scan(a_c, b_c, jnp.zeros_like(xc[:, 0]), rev)
        a_l, b_l = rglru_coeffs(xl, *prm)
        h_lat = h_lat + real_scan(a_l, b_l, end_state(hc, rev), rev)
        ctx_states.append(hc)
    y = (h_lat * jax.nn.gelu(gr.astype(f32))).astype(dt)
    if not need_ctx:
        return y, None
    yc = (ctx_states[0] + ctx_states[1]) * jax.nn.gelu(grc.astype(f32))
    return y, yc.astype(dt)


def s5_discretise(a_re, a_im, log_step, b_re, b_im):
    step = jnp.exp(log_step)[:, None]
    mag = jnp.exp(a_re * step)
    lb_re, lb_im = mag * jnp.cos(a_im * step), mag * jnp.sin(a_im * step)
    den = a_re * a_re + a_im * a_im
    num_re = lb_re - 1.0
    coef_re = (num_re * a_re + lb_im * a_im) / den
    coef_im = (lb_im * a_re - num_re * a_im) / den
    bb_re = coef_re[..., None] * b_re - coef_im[..., None] * b_im
    bb_im = coef_re[..., None] * b_im + coef_im[..., None] * b_re
    return lb_re, lb_im, bb_re, bb_im


def complex_combine(e1, e2):
    a1r, a1i, b1r, b1i = e1
    a2r, a2i, b2r, b2i = e2
    return (a1r * a2r - a1i * a2i, a1r * a2i + a1i * a2r,
            a2r * b1r - a2i * b1i + b2r, a2r * b1i + a2i * b1r + b2i)


def complex_scan(u, lb_re, lb_im, bb_re, bb_im, h0_re, h0_im, reverse):
    if reverse:
        u = jnp.flip(u, 1)
    bu_re = jnp.einsum('blgc,gpc->blgp', u, bb_re)
    bu_im = jnp.einsum('blgc,gpc->blgp', u, bb_im)
    bu_re = bu_re.at[:, 0].add(lb_re * h0_re - lb_im * h0_im)
    bu_im = bu_im.at[:, 0].add(lb_re * h0_im + lb_im * h0_re)
    shape = (1, u.shape[1]) + lb_re.shape
    elems = (jnp.broadcast_to(lb_re, shape), jnp.broadcast_to(lb_im, shape), bu_re, bu_im)
    _, _, h_re, h_im = lax.associative_scan(complex_combine, elems, axis=1)
    if reverse:
        h_re, h_im = jnp.flip(h_re, 1), jnp.flip(h_im, 1)
    return h_re, h_im


def s5_readout(h_re, h_im, c_re, c_im):
    return jnp.einsum('blgp,gcp->blgc', h_re, c_re) - jnp.einsum('blgp,gcp->blgc', h_im, c_im)


def glu(y, w, b):
    val, gate = jnp.split(y @ w + b, 2, axis=-1)
    return val * jax.nn.sigmoid(gate)


def mixer_d(xs, xsc, a_re, a_im, log_step, b_re, b_im, c_re, c_im, d_skip, w_glu, b_glu, need_ctx):
    dt = xs.dtype
    bsz, seq, width = xs.shape
    u = xs.astype(f32).reshape(bsz, seq, S5_GROUPS, S5_GROUP)
    uc = xsc.astype(f32).reshape(bsz, -1, S5_GROUPS, S5_GROUP)
    dg = d_skip.astype(f32).reshape(S5_GROUPS, S5_GROUP)
    y, yc = u * dg, uc * dg
    zeros = jnp.zeros((bsz, S5_GROUPS, S5_STATE), f32)
    for d, rev in enumerate(DIRECTIONS):
        disc = s5_discretise(a_re[d].astype(f32), a_im[d].astype(f32), log_step[d].astype(f32),
                             b_re[d].astype(f32), b_im[d].astype(f32))
        cr, ci = c_re[d].astype(f32), c_im[d].astype(f32)
        hc_re, hc_im = complex_scan(uc, *disc, zeros, zeros, rev)
        h_re, h_im = complex_scan(u, *disc, end_state(hc_re, rev), end_state(hc_im, rev), rev)
        y = y + s5_readout(h_re, h_im, cr, ci)
        if need_ctx:
            yc = yc + s5_readout(hc_re, hc_im, cr, ci)
    out = glu(y.reshape(bsz, seq, width).astype(dt), w_glu, b_glu)
    if not need_ctx:
        return out, None
    return out, glu(yc.reshape(bsz, -1, width).astype(dt), w_glu, b_glu)


def merge_branches(u, branches, w_gate, b_gate, w_branch, w_out):
    merged = None
    for i, y in enumerate(branches):
        term = jax.nn.sigmoid(u @ w_gate[i] + b_gate[i]) * (y @ w_branch[i])
        merged = term if merged is None else merged + term
    return merged @ w_out


def token_mixing(u, uc, p, cos, sin, need_ctx):
    z_lat = jnp.split(u @ p['w_in'], IN_OFFSETS, axis=-1)
    z_ctx = jnp.split(uc @ p['w_in'], IN_OFFSETS, axis=-1)
    ya, ya_c = mixer_a(*z_lat[0:3], *z_ctx[0:3], p['a_q_norm'], p['a_k_norm'], cos, sin, need_ctx)
    yb, yb_c = mixer_b(*z_lat[3:6], *z_ctx[3:6], p['b_sink'], cos, sin, need_ctx)
    yr, yr_c = mixer_c(z_lat[6], z_lat[7], z_ctx[6], z_ctx[7], p['lru_conv_w'], p['lru_conv_b'],
                       p['lru_w_r'], p['lru_b_r'], p['lru_w_i'], p['lru_b_i'], p['lru_lambda'], need_ctx)
    ys, ys_c = mixer_d(z_lat[8], z_ctx[8], p['s5_a_re'], p['s5_a_im'], p['s5_log_step'], p['s5_b_re'],
                       p['s5_b_im'], p['s5_c_re'], p['s5_c_im'], p['s5_d'], p['s5_w_glu'], p['s5_b_glu'], need_ctx)
    out = merge_branches(u, (ya, yb, yr, ys), p['w_gate'], p['b_gate'], p['w_branch'], p['w_out'])
    if not need_ctx:
        return out, None
    out_c = merge_branches(uc, (ya_c, yb_c, yr_c, ys_c), p['w_gate'], p['b_gate'], p['w_branch'], p['w_out'])
    return out, out_c


def peer_ffn(x, w_q, sub_keys, u_tab, v_tab):
    n_tok, width = x.shape

    def block(xb):
        tb = xb.shape[0]
        q = (xb @ w_q).reshape(tb, PEER_HEADS, 2, PEER_QDIM // 2)
        s = jnp.einsum('thpd,hpkd->thpk', q, sub_keys, preferred_element_type=f32)
        top_s, top_i = lax.top_k(s, PEER_TOPK)
        cand_s = (top_s[:, :, 0, :, None] + top_s[:, :, 1, None, :]).reshape(tb, PEER_HEADS, -1)
        cand_i = (top_i[:, :, 0, :, None] * PEER_KEYS + top_i[:, :, 1, None, :]).reshape(tb, PEER_HEADS, -1)
        best_s, best_j = lax.top_k(cand_s, PEER_TOPK)
        expert = jnp.take_along_axis(cand_i, best_j, axis=-1)
        g = jax.nn.softmax(best_s, axis=-1)
        ue = u_tab[expert]
        ve = v_tab[expert]
        act = jax.nn.gelu(jnp.einsum('thkd,td->thk', ue, xb, preferred_element_type=f32)) * g
        return jnp.einsum('thk,thkd->td', act.astype(ve.dtype), ve)

    out = lax.map(block, x.reshape(-1, PEER_TOKEN_BLOCK, width))
    return out.reshape(n_tok, width)


def setup_inputs(seed: int = 0) -> dict:
    key = jax.random.key(seed)
    keys = iter(jax.random.split(key, 48))

    def nrm(shape, scale):
        return jax.random.normal(next(keys), shape, f32) * scale

    d = D_MODEL
    x = nrm((BATCH, SEQ, d), 1.0)
    c = nrm((BATCH, d), 1.0)
    ctx = nrm((BATCH, CTX_LEN, d), 1.0)
    c_ctx = nrm((d,), 1.0)
    w_ada = nrm((DEPTH, d, 6 * d), d ** -0.5)
    b_ada = nrm((DEPTH, 6 * d), 0.02)
    w_in = nrm((DEPTH, d, IN_WIDTH), d ** -0.5)
    a_q_norm = 1.0 + nrm((DEPTH, HEAD_DIM), 0.02)
    a_k_norm = 1.0 + nrm((DEPTH, HEAD_DIM), 0.02)
    b_sink = nrm((DEPTH, B_HEADS), 0.5)
    lru_conv_w = nrm((DEPTH, CONV_W, LRU_WIDTH), CONV_W ** -0.5)
    lru_conv_b = nrm((DEPTH, LRU_WIDTH), 0.02)
    lru_w_r = nrm((DEPTH, 2, LRU_BLOCKS, LRU_BW, LRU_BW), LRU_BW ** -0.5)
    lru_b_r = nrm((DEPTH, 2, LRU_WIDTH), 0.02)
    lru_w_i = nrm((DEPTH, 2, LRU_BLOCKS, LRU_BW, LRU_BW), LRU_BW ** -0.5)
    lru_b_i = nrm((DEPTH, 2, LRU_WIDTH), 0.02)
    a_pow = jax.random.uniform(next(keys), (DEPTH, 2, LRU_WIDTH), f32, 0.9, 0.999)
    s_lam = a_pow ** (1.0 / LRU_C)
    lru_lambda = jnp.log(s_lam) - jnp.log1p(-s_lam)
    s5_a_re = -0.5 + nrm((DEPTH, 2, S5_GROUPS, S5_STATE), 0.01)
    s5_a_im = jnp.pi * jnp.arange(S5_STATE, dtype=f32) + nrm((DEPTH, 2, S5_GROUPS, S5_STATE), 0.01)
    s5_log_step = jax.random.uniform(next(keys), (DEPTH, 2, S5_GROUPS), f32, math.log(1e-3), math.log(1e-1))
    s5_b_re = nrm((DEPTH, 2, S5_GROUPS, S5_STATE, S5_GROUP), (2 * S5_GROUP) ** -0.5)
    s5_b_im = nrm((DEPTH, 2, S5_GROUPS, S5_STATE, S5_GROUP), (2 * S5_GROUP) ** -0.5)
    s5_c_re = nrm((DEPTH, 2, S5_GROUPS, S5_GROUP, S5_STATE), S5_STATE ** -0.5)
    s5_c_im = nrm((DEPTH, 2, S5_GROUPS, S5_GROUP, S5_STATE), S5_STATE ** -0.5)
    s5_d = nrm((DEPTH, S5_WIDTH), 1.0)
    s5_w_glu = nrm((DEPTH, S5_WIDTH, 2 * S5_WIDTH), S5_WIDTH ** -0.5)
    s5_b_glu = nrm((DEPTH, 2 * S5_WIDTH), 0.02)
    w_gate = nrm((DEPTH, N_BRANCH, d, d), d ** -0.5)
    b_gate = nrm((DEPTH, N_BRANCH, d), 0.02)
    w_branch = nrm((DEPTH, N_BRANCH, BRANCH_W, d), BRANCH_W ** -0.5 * DN_BETA)
    w_out = nrm((DEPTH, d, d), d ** -0.5 * DN_BETA)
    ln_g = 1.0 + nrm((DEPTH, 2, d), 0.02)
    ln_b = nrm((DEPTH, 2, d), 0.02)
    peer_w_q = nrm((DEPTH, d, PEER_HEADS * PEER_QDIM), d ** -0.5)
    peer_sub_keys = nrm((DEPTH, PEER_HEADS, 2, PEER_KEYS, PEER_QDIM // 2), (PEER_QDIM // 2) ** -0.5)
    peer_u = nrm((DEPTH, PEER_EXPERTS, d), d ** -0.5)
    peer_v = nrm((DEPTH, PEER_EXPERTS, d), DN_BETA)
    return {'x': x, 'c': c, 'ctx': ctx, 'c_ctx': c_ctx, 'w_ada': w_ada, 'b_ada': b_ada, 'w_in': w_in,
            'a_q_norm': a_q_norm, 'a_k_norm': a_k_norm, 'b_sink': b_sink,
            'lru_conv_w': lru_conv_w, 'lru_conv_b': lru_conv_b, 'lru_w_r': lru_w_r, 'lru_b_r': lru_b_r,
            'lru_w_i': lru_w_i, 'lru_b_i': lru_b_i, 'lru_lambda': lru_lambda,
            's5_a_re': s5_a_re, 's5_a_im': s5_a_im, 's5_log_step': s5_log_step, 's5_b_re': s5_b_re,
            's5_b_im': s5_b_im, 's5_c_re': s5_c_re, 's5_c_im': s5_c_im, 's5_d': s5_d,
            's5_w_glu': s5_w_glu, 's5_b_glu': s5_b_glu, 'w_gate': w_gate, 'b_gate': b_gate,
            'w_branch': w_branch, 'w_out': w_out, 'ln_g': ln_g, 'ln_b': ln_b,
            'peer_w_q': peer_w_q, 'peer_sub_keys': peer_sub_keys, 'peer_u': peer_u, 'peer_v': peer_v}


def reference(x, c, ctx, c_ctx, w_ada, b_ada, w_in, a_q_norm, a_k_norm, b_sink,
              lru_conv_w, lru_conv_b, lru_w_r, lru_b_r, lru_w_i, lru_b_i, lru_lambda,
              s5_a_re, s5_a_im, s5_log_step, s5_b_re, s5_b_im, s5_c_re, s5_c_im, s5_d,
              s5_w_glu, s5_b_glu, w_gate, b_gate, w_branch, w_out, ln_g, ln_b,
              peer_w_q, peer_sub_keys, peer_u, peer_v):
    d = x.shape[-1]
    rows = x.shape[1] // GRID_W
    cos, sin = axial_rope(rows)
    xc = ctx
    for l in range(DEPTH):
        last = l == DEPTH - 1
        mod = jax.nn.silu(c) @ w_ada[l] + b_ada[l]
        sh1, sc1, g1, sh2, sc2, g2 = jnp.split(mod[:, None, :], 6, axis=-1)
        mod_c = jax.nn.silu(c_ctx) @ w_ada[l] + b_ada[l]
        sh1c, sc1c, g1c, sh2c, sc2c, g2c = jnp.split(mod_c, 6)
        p = {'w_in': w_in[l], 'a_q_norm': a_q_norm[l], 'a_k_norm': a_k_norm[l], 'b_sink': b_sink[l],
             'lru_conv_w': lru_conv_w[l], 'lru_conv_b': lru_conv_b[l], 'lru_w_r': lru_w_r[l],
             'lru_b_r': lru_b_r[l], 'lru_w_i': lru_w_i[l], 'lru_b_i': lru_b_i[l], 'lru_lambda': lru_lambda[l],
             's5_a_re': s5_a_re[l], 's5_a_im': s5_a_im[l], 's5_log_step': s5_log_step[l],
             's5_b_re': s5_b_re[l], 's5_b_im': s5_b_im[l], 's5_c_re': s5_c_re[l], 's5_c_im': s5_c_im[l],
             's5_d': s5_d[l], 's5_w_glu': s5_w_glu[l], 's5_b_glu': s5_b_glu[l],
             'w_gate': w_gate[l], 'b_gate': b_gate[l], 'w_branch': w_branch[l], 'w_out': w_out[l]}
        y, yc = token_mixing(x * (1.0 + sc1) + sh1, xc * (1.0 + sc1c) + sh1c, p, cos, sin, not last)
        x = layer_norm(DN_ALPHA * x + g1 * y, ln_g[l, 0], ln_b[l, 0])
        f = peer_ffn((x * (1.0 + sc2) + sh2).reshape(-1, d), peer_w_q[l], peer_sub_keys[l],
                     peer_u[l], peer_v[l]).reshape(x.shape)
        x = layer_norm(DN_ALPHA * x + g2 * f, ln_g[l, 1], ln_b[l, 1])
        if not last:
            xc = layer_norm(DN_ALPHA * xc + g1c * yc, ln_g[l, 0], ln_b[l, 0])
            fc = peer_ffn((xc * (1.0 + sc2c) + sh2c).reshape(-1, d), peer_w_q[l], peer_sub_keys[l],
                          peer_u[l], peer_v[l]).reshape(xc.shape)
            xc = layer_norm(DN_ALPHA * xc + g2c * fc, ln_g[l, 1], ln_b[l, 1])
    return x
```

```python
import functools
import math

import jax
import jax.numpy as jnp
from jax import lax
from jax.experimental import pallas as pl
from jax.experimental.pallas import tpu as pltpu

f32 = jnp.float32
bf16 = jnp.bfloat16

HEAD_DIM = 128
GRID_W = 64
ROPE_THETA = 10000.0
WINDOW = 128
Q_TILE = 128
KV_GROUP = 4
LRU_C = 8.0
LRU_BLOCKS = 8
S5_GROUP = 16
S5_STATE = 64
S5_CHUNK = 8
S5_OCT = 8
PEER_TOPK = 16
PEER_KEYS = 128
LN_EPS = 1e-6
NEG = -1e30
ROW_TILE = 256
MM_TILE_M = 512
V7X_VMEM_LIMIT = 56 * 1024 * 1024


def _params(*sem, vmem=V7X_VMEM_LIMIT):
    return pltpu.CompilerParams(dimension_semantics=sem, vmem_limit_bytes=vmem)


def _dot(a, b):
    return jnp.dot(a, b, preferred_element_type=f32)


def _dot_nt(a, b):
    return lax.dot_general(a, b, (((1,), (1,)), ((), ())), preferred_element_type=f32)


def _gelu(x):
    return x * (0.5 * (1.0 + jnp.tanh(math.sqrt(2.0 / math.pi) * (x + 0.044715 * (x * x * x)))))


def _sigmoid(x):
    return 1.0 / (1.0 + jnp.exp(-x))


def _ada_kernel(c_ref, w_ref, b_ref, o_ref):
    c = c_ref[...]
    a = (c * _sigmoid(c)).astype(bf16)
    o_ref[0] = _dot(a, w_ref[0].astype(bf16)) + b_ref[0]


def _ada_call(cin, w_ada, b_ada, tn=512):
    depth, d, n = w_ada.shape
    return pl.pallas_call(
        _ada_kernel,
        grid=(depth, n // tn),
        in_specs=[pl.BlockSpec((8, d), lambda l, j: (0, 0)),
                  pl.BlockSpec((1, d, tn), lambda l, j: (l, 0, j)),
                  pl.BlockSpec((1, 1, tn), lambda l, j: (l, 0, j))],
        out_specs=pl.BlockSpec((1, 8, tn), lambda l, j: (l, 0, j)),
        out_shape=jax.ShapeDtypeStruct((depth, 8, n), f32),
        compiler_params=_params("parallel", "parallel"),
        name="adaln",
    )(cin, w_ada, b_ada.reshape(depth, 1, n))


def _row_group(i, tiles_per_batch, ctx_tiles, n_batch):
    return jnp.where(i % tiles_per_batch < ctx_tiles, n_batch, i // tiles_per_batch)


def _mod_kernel(x_ref, sc_ref, sh_ref, o_ref):
    o_ref[...] = (x_ref[...] * (1.0 + sc_ref[0]) + sh_ref[0]).astype(o_ref.dtype)


def _mod_call(x, sc, sh, geom):
    n, d = x.shape
    tpb, ct, nb = geom
    sel = lambda i: (_row_group(i, tpb, ct, nb), 0, 0)
    return pl.pallas_call(
        _mod_kernel,
        grid=(n // ROW_TILE,),
        in_specs=[pl.BlockSpec((ROW_TILE, d), lambda i: (i, 0)),
                  pl.BlockSpec((1, 1, d), sel), pl.BlockSpec((1, 1, d), sel)],
        out_specs=pl.BlockSpec((ROW_TILE, d), lambda i: (i, 0)),
        out_shape=jax.ShapeDtypeStruct((n, d), bf16),
        compiler_params=_params("parallel"),
        name="modulate",
    )(x, sc, sh)


def _ln_kernel(x_ref, y_ref, g_ref, lg_ref, lb_ref, sc_ref, sh_ref, xo_ref, xm_ref, *, alpha):
    v = alpha * x_ref[...] + g_ref[0] * y_ref[...]
    mu = jnp.mean(v, axis=-1, keepdims=True)
    vc = v - mu
    var = jnp.mean(vc * vc, axis=-1, keepdims=True)
    o = vc * lax.rsqrt(var + LN_EPS) * lg_ref[...] + lb_ref[...]
    xo_ref[...] = o
    xm_ref[...] = (o * (1.0 + sc_ref[0]) + sh_ref[0]).astype(xm_ref.dtype)


def _ln_call(x, y, gate, ln_g, ln_b, sc, sh, geom, alpha):
    n, d = x.shape
    tpb, ct, nb = geom
    sel = lambda i: (_row_group(i, tpb, ct, nb), 0, 0)
    row = pl.BlockSpec((ROW_TILE, d), lambda i: (i, 0))
    vec = pl.BlockSpec((1, d), lambda i: (0, 0))
    return pl.pallas_call(
        functools.partial(_ln_kernel, alpha=alpha),
        grid=(n // ROW_TILE,),
        in_specs=[row, row, pl.BlockSpec((1, 1, d), sel), vec, vec,
                  pl.BlockSpec((1, 1, d), sel), pl.BlockSpec((1, 1, d), sel)],
        out_specs=[row, row],
        out_shape=[jax.ShapeDtypeStruct((n, d), f32), jax.ShapeDtypeStruct((n, d), bf16)],
        compiler_params=_params("parallel"),
        name="residual_ln",
    )(x, y, gate, ln_g.reshape(1, d), ln_b.reshape(1, d), sc, sh)


def _mm_kernel(a_ref, b_ref, o_ref):
    o_ref[...] = _dot(a_ref[...], b_ref[...]).astype(o_ref.dtype)


def _mm_call(a, b, out_dtype, tn=512, tm=MM_TILE_M, name="matmul"):
    m, k = a.shape
    n = b.shape[1]
    return pl.pallas_call(
        _mm_kernel,
        grid=(m // tm, n // tn),
        in_specs=[pl.BlockSpec((tm, k), lambda i, j: (i, 0)),
                  pl.BlockSpec((k, tn), lambda i, j: (0, j))],
        out_specs=pl.BlockSpec((tm, tn), lambda i, j: (i, j)),
        out_shape=jax.ShapeDtypeStruct((m, n), out_dtype),
        compiler_params=_params("parallel", "arbitrary"),
        name=name,
    )(a, b)


def _glu_kernel(a_ref, wv_ref, wg_ref, bv_ref, bg_ref, o_ref):
    a = a_ref[...].astype(bf16)
    val = _dot(a, wv_ref[...]) + bv_ref[...]
    gate = _dot(a, wg_ref[...]) + bg_ref[...]
    o_ref[...] = (val * _sigmoid(gate)).astype(o_ref.dtype)


def _glu_call(a, w, b, tn=512, tm=MM_TILE_M):
    m, k = a.shape
    n = w.shape[1] // 2
    nj = n // tn
    b2 = b.reshape(1, 2 * n)
    return pl.pallas_call(
        _glu_kernel,
        grid=(m // tm, nj),
        in_specs=[pl.BlockSpec((tm, k), lambda i, j: (i, 0)),
                  pl.BlockSpec((k, tn), lambda i, j: (0, j)),
                  pl.BlockSpec((k, tn), lambda i, j: (0, j + nj)),
                  pl.BlockSpec((1, tn), lambda i, j: (0, j)),
                  pl.BlockSpec((1, tn), lambda i, j: (0, j + nj))],
        out_specs=pl.BlockSpec((tm, tn), lambda i, j: (i, j)),
        out_shape=jax.ShapeDtypeStruct((m, n), bf16),
        compiler_params=_params("parallel", "arbitrary"),
        name="s5_glu",
    )(a, w, w, b2, b2)


def _merge_kernel(u_ref, ya_ref, yb_ref, yr_ref, ys_ref, wg_ref, bg_ref, wb_ref, o_ref):
    u = u_ref[...]
    acc = None
    for i, y_ref in enumerate((ya_ref, yb_ref, yr_ref, ys_ref)):
        gate = _sigmoid(_dot(u, wg_ref[i]) + bg_ref[i])
        term = gate * _dot(y_ref[...], wb_ref[i])
        acc = term if acc is None else acc + term
    o_ref[...] = acc.astype(o_ref.dtype)


def _merge_call(u, branches, w_gate, b_gate, w_branch, tn=256, tm=MM_TILE_M):
    m, d = u.shape
    nbr, bw, n = w_branch.shape
    ybr = pl.BlockSpec((tm, bw), lambda i, j: (i, 0))
    return pl.pallas_call(
        _merge_kernel,
        grid=(m // tm, n // tn),
        in_specs=[pl.BlockSpec((tm, d), lambda i, j: (i, 0)), ybr, ybr, ybr, ybr,
                  pl.BlockSpec((nbr, d, tn), lambda i, j: (0, 0, j)),
                  pl.BlockSpec((nbr, 1, tn), lambda i, j: (0, 0, j)),
                  pl.BlockSpec((nbr, bw, tn), lambda i, j: (0, 0, j))],
        out_specs=pl.BlockSpec((tm, tn), lambda i, j: (i, j)),
        out_shape=jax.ShapeDtypeStruct((m, n), bf16),
        compiler_params=_params("parallel", "arbitrary"),
        name="branch_merge",
    )(u, *branches, w_gate, b_gate.reshape(nbr, 1, n), w_branch)


def _rope_tables(n_lat, n_ctx):
    t = jnp.arange(n_lat)
    row = (t // GRID_W).astype(f32)
    col = (t % GRID_W).astype(f32)
    n_freq = HEAD_DIM // 4
    inv = ROPE_THETA ** (-jnp.arange(n_freq, dtype=f32) / n_freq)
    ang = jnp.concatenate([row[:, None] * inv, col[:, None] * inv], axis=-1)
    cos = jnp.repeat(jnp.cos(ang), 2, axis=-1)
    sin = jnp.repeat(jnp.sin(ang), 2, axis=-1) * jnp.tile(jnp.array([-1.0, 1.0], f32), HEAD_DIM // 2)
    cos = jnp.concatenate([jnp.ones((n_ctx, HEAD_DIM), f32), cos], axis=0)
    sin = jnp.concatenate([jnp.zeros((n_ctx, HEAD_DIM), f32), sin], axis=0)
    return cos, sin


N_QKV_SLICES = 24
A_Q, A_K, A_V, B_Q, B_K, B_V = 0, 8, 10, 12, 20, 22


def _prep_kernel(z_ref, cos_ref, sin_ref, gain_ref, o_ref):
    s = pl.program_id(1)
    do_norm = s < A_V
    do_rope = jnp.logical_or(s < A_V, jnp.logical_and(s >= B_Q, s < B_V))
    is_q = jnp.logical_or(s < A_K, jnp.logical_and(s >= B_Q, s < B_K))
    x = z_ref[...]
    inv = lax.rsqrt(jnp.mean(x * x, axis=-1, keepdims=True) + LN_EPS)
    y = x * jnp.where(do_norm, inv, 1.0) * gain_ref[0]
    cs = jnp.where(do_rope, cos_ref[...], 1.0)
    sn = jnp.where(do_rope, sin_ref[...], 0.0)
    lane = lax.broadcasted_iota(jnp.int32, y.shape, 1)
    partner = jnp.where(lane % 2 == 0, pltpu.roll(y, HEAD_DIM - 1, axis=1), pltpu.roll(y, 1, axis=1))
    out = (y * cs + partner * sn) * jnp.where(is_q, HEAD_DIM ** -0.5, 1.0)
    o_ref[0, 0] = out.astype(o_ref.dtype)


def _prep_call(z, cos, sin, gains, n_batch, seq_tot):
    n = z.shape[0]
    tr = ROW_TILE
    tpb = seq_tot // tr
    return pl.pallas_call(
        _prep_kernel,
        grid=(n // tr, N_QKV_SLICES),
        in_specs=[pl.BlockSpec((tr, HEAD_DIM), lambda i, s: (i, s)),
                  pl.BlockSpec((tr, HEAD_DIM), lambda i, s: (i % tpb, 0)),
                  pl.BlockSpec((tr, HEAD_DIM), lambda i, s: (i % tpb, 0)),
                  pl.BlockSpec((1, 1, HEAD_DIM), lambda i, s: (s, 0, 0))],
        out_specs=pl.BlockSpec((1, 1, tr, HEAD_DIM), lambda i, s: (i // tpb, s, i % tpb, 0)),
        out_shape=jax.ShapeDtypeStruct((n_batch, N_QKV_SLICES, seq_tot, HEAD_DIM), bf16),
        compiler_params=_params("parallel", "arbitrary"),
        name="qkv_prep",
    )(z, cos, sin, gains)


def _attn_a_kernel(q_ref, k_ref, v_ref, o_ref, m_s, l_s, acc_s, *, tk, ctx_tiles, ctx_chunks, all_chunks):
    i = pl.program_id(2)
    q = q_ref[0].reshape(KV_GROUP * Q_TILE, HEAD_DIM)
    m_s[...] = jnp.full(m_s.shape, NEG, f32)
    l_s[...] = jnp.zeros(l_s.shape, f32)
    acc_s[...] = jnp.zeros(acc_s.shape, f32)
    n_chunks = jnp.where(i < ctx_tiles, ctx_chunks, all_chunks)

    def body(c, carry):
        start = pl.multiple_of(c * tk, tk)
        k = k_ref[0, 0, pl.ds(start, tk), :]
        v = v_ref[0, 0, pl.ds(start, tk), :]
        s = _dot_nt(q, k)
        m_prev = m_s[...]
        m_new = jnp.maximum(m_prev, jnp.max(s, axis=-1, keepdims=True))
        alpha = jnp.exp(m_prev - m_new)
        p = jnp.exp(s - m_new)
        l_s[...] = alpha * l_s[...] + jnp.sum(p, axis=-1, keepdims=True)
        acc_s[...] = alpha * acc_s[...] + _dot(p.astype(bf16), v)
        m_s[...] = m_new
        return carry

    lax.fori_loop(0, n_chunks, body, 0)
    out = acc_s[...] / l_s[...]
    for g in range(KV_GROUP):
        o_ref[0, :, g * HEAD_DIM:(g + 1) * HEAD_DIM] = out[g * Q_TILE:(g + 1) * Q_TILE].astype(o_ref.dtype)


def _attn_a_call(qkv, n_ctx, tk=256):
    nb, _, seq_tot, _ = qkv.shape
    n_kv = (A_V - A_K)
    rows = KV_GROUP * Q_TILE
    kern = functools.partial(_attn_a_kernel, tk=tk, ctx_tiles=n_ctx // Q_TILE,
                             ctx_chunks=n_ctx // tk, all_chunks=seq_tot // tk)
    return pl.pallas_call(
        kern,
        grid=(nb, n_kv, seq_tot // Q_TILE),
        in_specs=[pl.BlockSpec((1, KV_GROUP, Q_TILE, HEAD_DIM), lambda b, h, i: (b, A_Q // KV_GROUP + h, i, 0)),
                  pl.BlockSpec((1, 1, seq_tot, HEAD_DIM), lambda b, h, i: (b, A_K + h, 0, 0)),
                  pl.BlockSpec((1, 1, seq_tot, HEAD_DIM), lambda b, h, i: (b, A_V + h, 0, 0))],
        out_specs=pl.BlockSpec((1, Q_TILE, KV_GROUP * HEAD_DIM), lambda b, h, i: (b, i, h)),
        out_shape=jax.ShapeDtypeStruct((nb, seq_tot, n_kv * KV_GROUP * HEAD_DIM), bf16),
        scratch_shapes=[pltpu.VMEM((rows, 1), f32), pltpu.VMEM((rows, 1), f32),
                        pltpu.VMEM((rows, HEAD_DIM), f32)],
        compiler_params=_params("parallel", "parallel", "arbitrary"),
        name="dense_attention",
    )(qkv, qkv, qkv)


def _attn_b_kernel(q_ref, k_ref, v_ref, sink_ref, o_ref, *, n_ctx, n_blk):
    i = pl.program_id(2)
    blk = i - n_ctx // Q_TILE
    rows = KV_GROUP * Q_TILE
    q = q_ref[0].reshape(rows, HEAD_DIM)
    sink = sink_ref[0]
    kc = k_ref[0, 0, 0:n_ctx, :]
    vc = v_ref[0, 0, 0:n_ctx, :]
    s_ctx = _dot_nt(q, kc)
    m = jnp.maximum(jnp.max(s_ctx, axis=-1, keepdims=True), sink)
    q_pos = lax.broadcasted_iota(jnp.int32, (rows, Q_TILE), 0) % Q_TILE
    k_off = lax.broadcasted_iota(jnp.int32, (rows, Q_TILE), 1)
    s_loc, v_loc = [], []
    for d in (-1, 0, 1):
        j = blk + d
        valid = jnp.logical_and(blk >= 0, jnp.logical_and(j >= 0, j < n_blk))
        start = pl.multiple_of(n_ctx + jnp.clip(j, 0, n_blk - 1) * Q_TILE, Q_TILE)
        k = k_ref[0, 0, pl.ds(start, Q_TILE), :]
        v_loc.append(v_ref[0, 0, pl.ds(start, Q_TILE), :])
        keep = jnp.logical_and(jnp.abs(k_off + d * Q_TILE - q_pos) <= WINDOW, valid)
        s = jnp.where(keep, _dot_nt(q, k), NEG)
        m = jnp.maximum(m, jnp.max(s, axis=-1, keepdims=True))
        s_loc.append(s)
    p = jnp.exp(s_ctx - m)
    den = jnp.exp(sink - m) + jnp.sum(p, axis=-1, keepdims=True)
    acc = _dot(p.astype(bf16), vc)
    for s, v in zip(s_loc, v_loc):
        p = jnp.exp(s - m)
        den = den + jnp.sum(p, axis=-1, keepdims=True)
        acc = acc + _dot(p.astype(bf16), v)
    out = acc / den
    for g in range(KV_GROUP):
        o_ref[0, :, g * HEAD_DIM:(g + 1) * HEAD_DIM] = out[g * Q_TILE:(g + 1) * Q_TILE].astype(o_ref.dtype)


def _attn_b_call(qkv, sink, n_ctx):
    nb, _, seq_tot, _ = qkv.shape
    n_kv = (B_V - B_K)
    rows = KV_GROUP * Q_TILE
    sink_rows = jnp.repeat(sink.astype(f32).reshape(n_kv, KV_GROUP), Q_TILE, axis=1).reshape(n_kv, rows, 1)
    kern = functools.partial(_attn_b_kernel, n_ctx=n_ctx, n_blk=(seq_tot - n_ctx) // Q_TILE)
    return pl.pallas_call(
        kern,
        grid=(nb, n_kv, seq_tot // Q_TILE),
        in_specs=[pl.BlockSpec((1, KV_GROUP, Q_TILE, HEAD_DIM), lambda b, h, i: (b, B_Q // KV_GROUP + h, i, 0)),
                  pl.BlockSpec((1, 1, seq_tot, HEAD_DIM), lambda b, h, i: (b, B_K + h, 0, 0)),
                  pl.BlockSpec((1, 1, seq_tot, HEAD_DIM), lambda b, h, i: (b, B_V + h, 0, 0)),
                  pl.BlockSpec((1, rows, 1), lambda b, h, i: (h, 0, 0))],
        out_specs=pl.BlockSpec((1, Q_TILE, KV_GROUP * HEAD_DIM), lambda b, h, i: (b, i, h)),
        out_shape=jax.ShapeDtypeStruct((nb, seq_tot, n_kv * KV_GROUP * HEAD_DIM), bf16),
        compiler_params=_params("parallel", "parallel", "arbitrary"),
        name="window_attention",
    )(qkv, qkv, qkv, sink_rows)


HALO = 8


def _lru_gate_kernel(x_ref, xp_ref, xn_ref, cw_ref, cb_ref, wr_ref, br_ref, wi_ref, bi_ref, sp_ref,
                     a_ref, b_ref, pad_s, *, tiles_per_batch):
    tr = x_ref.shape[0]
    t = pl.program_id(0) % tiles_per_batch
    has_prev = t >= 2
    has_next = jnp.logical_and(t >= 1, t < tiles_per_batch - 1)
    x = x_ref[...]
    pad_s[0:HALO, :] = jnp.where(has_prev, xp_ref[...], 0.0)
    pad_s[HALO:HALO + tr, :] = x
    pad_s[HALO + tr:2 * HALO + tr, :] = jnp.where(has_next, xn_ref[...], 0.0)
    cw = cw_ref[...]
    xl = cb_ref[...] + pad_s[HALO - 2:HALO - 2 + tr, :] * cw[0:1]
    xl = xl + pad_s[HALO - 1:HALO - 1 + tr, :] * cw[1:2]
    xl = xl + x * cw[2:3]
    xl = xl + pad_s[HALO + 1:HALO + 1 + tr, :] * cw[3:4]
    bw = xl.shape[1] // LRU_BLOCKS
    for n in range(LRU_BLOCKS):
        cols = slice(n * bw, (n + 1) * bw)
        xb = xl[:, cols]
        xb16 = xb.astype(bf16)
        for d in range(2):
            r = _sigmoid(_dot(xb16, wr_ref[d, n]) + br_ref[d:d + 1, cols])
            gi = _sigmoid(_dot(xb16, wi_ref[d, n]) + bi_ref[d:d + 1, cols])
            log_a = (-LRU_C) * r * sp_ref[d:d + 1, cols]
            a = jnp.exp(log_a)
            a_ref[d, :, cols] = a
            b_ref[d, :, cols] = jnp.sqrt(1.0 - a * a) * gi * xb


def _lru_gate_call(z, col_block, conv_w, conv_b, w_r, b_r, w_i, b_i, softplus_lam, tiles_per_batch):
    n = z.shape[0]
    w = conv_w.shape[1]
    tr = ROW_TILE
    hb = tr // HALO
    n_halo = n // HALO
    full = lambda shape: pl.BlockSpec(shape, lambda i: (0,) * len(shape))
    kern = functools.partial(_lru_gate_kernel, tiles_per_batch=tiles_per_batch)
    out = jax.ShapeDtypeStruct((2, n, w), f32)
    return pl.pallas_call(
        kern,
        grid=(n // tr,),
        in_specs=[pl.BlockSpec((tr, w), lambda i: (i, col_block)),
                  pl.BlockSpec((HALO, w), lambda i: (jnp.maximum(i * hb - 1, 0), col_block)),
                  pl.BlockSpec((HALO, w), lambda i: (jnp.minimum((i + 1) * hb, n_halo - 1), col_block)),
                  full(conv_w.shape), full((1, w)), full(w_r.shape), full(b_r.shape),
                  full(w_i.shape), full(b_i.shape), full(softplus_lam.shape)],
        out_specs=[pl.BlockSpec((2, tr, w), lambda i: (0, i, 0)), pl.BlockSpec((2, tr, w), lambda i: (0, i, 0))],
        out_shape=[out, out],
        scratch_shapes=[pltpu.VMEM((tr + 2 * HALO, w), f32)],
        compiler_params=_params("parallel"),
        name="lru_gates",
    )(z, z, z, conv_w, conv_b.reshape(1, w), w_r, b_r, w_i, b_i, softplus_lam)


def _lru_scan_kernel(af_ref, bf_ref, ar_ref, br_ref, hf_ref, hr_ref, sf, sr):
    tr = hf_ref.shape[0]

    @pl.when(pl.program_id(1) == 0)
    def _():
        sf[...] = jnp.zeros(sf.shape, f32)
        sr[...] = jnp.zeros(sr.shape, f32)

    def body(t, carry):
        hf, hr = carry
        hf = af_ref[0, pl.ds(t, 1), :] * hf + bf_ref[0, pl.ds(t, 1), :]
        hf_ref[pl.ds(t, 1), :] = hf
        u = tr - 1 - t
        hr = ar_ref[0, pl.ds(u, 1), :] * hr + br_ref[0, pl.ds(u, 1), :]
        hr_ref[pl.ds(u, 1), :] = hr
        return hf, hr

    hf, hr = lax.fori_loop(0, tr, body, (sf[...], sr[...]), unroll=8)
    sf[...] = hf
    sr[...] = hr


def _scan_tile_maps(tiles_per_batch):
    fwd = lambda b, i: b * tiles_per_batch + i
    rev = lambda b, i: b * tiles_per_batch + jnp.where(i == 0, 0, tiles_per_batch - i)
    return fwd, rev


def _lru_scan_call(a, b, n_batch, tiles_per_batch):
    _, n, w = a.shape
    tr = ROW_TILE
    fwd, rev = _scan_tile_maps(tiles_per_batch)
    out = jax.ShapeDtypeStruct((n, w), f32)
    return pl.pallas_call(
        _lru_scan_kernel,
        grid=(n_batch, tiles_per_batch),
        in_specs=[pl.BlockSpec((1, tr, w), lambda bb, i: (0, fwd(bb, i), 0)),
                  pl.BlockSpec((1, tr, w), lambda bb, i: (0, fwd(bb, i), 0)),
                  pl.BlockSpec((1, tr, w), lambda bb, i: (1, rev(bb, i), 0)),
                  pl.BlockSpec((1, tr, w), lambda bb, i: (1, rev(bb, i), 0))],
        out_specs=[pl.BlockSpec((tr, w), lambda bb, i: (fwd(bb, i), 0)),
                   pl.BlockSpec((tr, w), lambda bb, i: (rev(bb, i), 0))],
        out_shape=[out, out],
        scratch_shapes=[pltpu.VMEM((1, w), f32), pltpu.VMEM((1, w), f32)],
        compiler_params=_params("parallel", "arbitrary"),
        name="lru_scan",
    )(a, b, a, b)


def _lru_out_kernel(hf_ref, hr_ref, g_ref, o_ref):
    o_ref[...] = ((hf_ref[...] + hr_ref[...]) * _gelu(g_ref[...])).astype(o_ref.dtype)


def _lru_out_call(hf, hr, z, gate_col_block):
    n, w = hf.shape
    tr = ROW_TILE
    row = pl.BlockSpec((tr, w), lambda i: (i, 0))
    return pl.pallas_call(
        _lru_out_kernel,
        grid=(n // tr,),
        in_specs=[row, row, pl.BlockSpec((tr, w), lambda i: (i, gate_col_block))],
        out_specs=row,
        out_shape=jax.ShapeDtypeStruct((n, w), bf16),
        compiler_params=_params("parallel"),
        name="lru_out",
    )(hf, hr, z)


def _s5_weights(a_re, a_im, log_step, b_re, b_im, c_re, c_im):
    r = S5_CHUNK
    n_g = a_re.shape[1]
    n_oct = n_g // S5_OCT
    step = jnp.exp(log_step)[..., None]
    mag = jnp.exp(a_re * step)
    lb_re, lb_im = mag * jnp.cos(a_im * step), mag * jnp.sin(a_im * step)
    den = a_re * a_re + a_im * a_im
    num_re = lb_re - 1.0
    coef_re = (num_re * a_re + lb_im * a_im) / den
    coef_im = (lb_im * a_re - num_re * a_im) / den
    bb_re = coef_re[..., None] * b_re - coef_im[..., None] * b_im
    bb_im = coef_re[..., None] * b_im + coef_im[..., None] * b_re
    tau = jnp.arange(r + 1, dtype=f32)[:, None, None, None]
    pmag = jnp.exp(tau * (a_re * step))
    pw_re, pw_im = pmag * jnp.cos(tau * (a_im * step)), pmag * jnp.sin(tau * (a_im * step))

    lb_b_re = pw_re[..., None] * bb_re - pw_im[..., None] * bb_im
    lb_b_im = pw_re[..., None] * bb_im + pw_im[..., None] * bb_re
    lag = (jnp.einsum('tdgpc,dgop->tdgco', lb_b_re, c_re) - jnp.einsum('tdgpc,dgop->tdgco', lb_b_im, c_im))
    idx = jnp.arange(r)
    sh_f = (idx[None, None, :] - idx[None, :, None] == jnp.arange(r + 1)[:, None, None]).astype(f32)
    sh_r = (idx[None, :, None] - idx[None, None, :] == jnp.arange(r + 1)[:, None, None]).astype(f32)
    k_loc = jnp.einsum('tio,tgcd->gicod', sh_f, lag[:, 0]) + jnp.einsum('tio,tgcd->gicod', sh_r, lag[:, 1])
    eye = jnp.eye(S5_OCT, dtype=f32)
    k_loc = k_loc.reshape(n_oct, S5_OCT, r, S5_GROUP, r, S5_GROUP)
    k_oct = jnp.einsum('kgicod,gh->kigcohd', k_loc, eye).reshape(n_oct, r * 128, r * 128)

    def inject(d, powers):
        w_re = pw_re[powers, d][..., None] * bb_re[d] - pw_im[powers, d][..., None] * bb_im[d]
        w_im = pw_re[powers, d][..., None] * bb_im[d] + pw_im[powers, d][..., None] * bb_re[d]
        out = []
        for w in (w_re, w_im):
            w = w.reshape(r, n_oct, S5_OCT, S5_STATE, S5_GROUP)
            out.append(jnp.einsum('ikgpc,gh->kigchp', w, eye).reshape(n_oct, r * 128, S5_OCT * S5_STATE))
        return out
    w_end = jnp.concatenate(inject(0, idx[::-1]) + inject(1, idx), axis=-1)

    def readout(d, powers):
        cl_re = c_re[d][None] * pw_re[powers, d][:, :, None, :] - c_im[d][None] * pw_im[powers, d][:, :, None, :]
        cl_im = c_re[d][None] * pw_im[powers, d][:, :, None, :] + c_im[d][None] * pw_re[powers, d][:, :, None, :]
        out = []
        for w in (cl_re, -cl_im):
            w = w.reshape(r, n_oct, S5_OCT, S5_GROUP, S5_STATE)
            out.append(jnp.einsum('okgcp,gh->kgpohc', w, eye).reshape(n_oct, S5_OCT * S5_STATE, r * 128))
        return out
    m_read = jnp.concatenate(readout(0, idx + 1) + readout(1, r - idx), axis=1)

    lam_r = jnp.stack([pw_re[r, 0].reshape(-1), pw_im[r, 0].reshape(-1),
                       pw_re[r, 1].reshape(-1), pw_im[r, 1].reshape(-1)])
    return k_oct.astype(bf16), w_end.astype(bf16), m_read.astype(bf16), lam_r


def _chunk_rows(u_ref):
    n_rows = u_ref.shape[0] // S5_CHUNK
    return [u_ref[pl.ds(i, n_rows, stride=S5_CHUNK), :] for i in range(S5_CHUNK)]


def _s5_inject_kernel(u_ref, w_ref, efr_ref, efi_ref, err_ref, eri_ref):
    x = jnp.concatenate(_chunk_rows(u_ref), axis=1).astype(bf16)
    e = _dot(x, w_ref[0])
    w = efr_ref.shape[1]
    for k, o_ref in enumerate((efr_ref, efi_ref, err_ref, eri_ref)):
        o_ref[...] = e[:, k * w:(k + 1) * w]


def _s5_inject_call(z, col_block0, w_end, steps):
    n = z.shape[0]
    n_oct = w_end.shape[0]
    sw = S5_OCT * S5_STATE
    rows = steps // S5_CHUNK
    out = jax.ShapeDtypeStruct((n // S5_CHUNK, n_oct * sw), f32)
    ospec = pl.BlockSpec((rows, sw), lambda i, k: (i, k))
    return pl.pallas_call(
        _s5_inject_kernel,
        grid=(n // steps, n_oct),
        in_specs=[pl.BlockSpec((steps, 128), lambda i, k: (i, col_block0 + k)),
                  pl.BlockSpec((1,) + w_end.shape[1:], lambda i, k: (k, 0, 0))],
        out_specs=[ospec] * 4,
        out_shape=[out] * 4,
        compiler_params=_params("parallel", "arbitrary"),
        name="s5_inject",
    )(z, w_end)


def _s5_scan_kernel(efr_ref, efi_ref, err_ref, eri_ref, lam_ref, hfr_ref, hfi_ref, hrr_ref, hri_ref, *, n_ctx):
    n_rows = efr_ref.shape[0]
    lfr, lfi, lrr, lri = (lam_ref[k:k + 1, :] for k in range(4))
    zero = jnp.zeros((1, efr_ref.shape[1]), f32)

    def body(t, carry):
        fr, fi, rr, ri = carry
        hfr_ref[pl.ds(t, 1), :] = fr
        hfi_ref[pl.ds(t, 1), :] = fi
        er, ei = efr_ref[pl.ds(t, 1), :], efi_ref[pl.ds(t, 1), :]
        fr, fi = lfr * fr - lfi * fi + er, lfr * fi + lfi * fr + ei
        u = jnp.where(t < n_ctx, n_ctx - 1 - t, n_rows - 1 - (t - n_ctx))
        hrr_ref[pl.ds(u, 1), :] = rr
        hri_ref[pl.ds(u, 1), :] = ri
        er, ei = err_ref[pl.ds(u, 1), :], eri_ref[pl.ds(u, 1), :]
        rr, ri = lrr * rr - lri * ri + er, lrr * ri + lri * rr + ei
        return fr, fi, rr, ri

    lax.fori_loop(0, n_rows, body, (zero, zero, zero, zero), unroll=4)


def _s5_scan_call(e_parts, lam_r, n_batch, n_ctx_rows, lane_block=512):
    n_rows_tot, width = e_parts[0].shape
    rows = n_rows_tot // n_batch
    blk = pl.BlockSpec((rows, lane_block), lambda b, j: (b, j))
    out = jax.ShapeDtypeStruct((n_rows_tot, width), f32)
    return pl.pallas_call(
        functools.partial(_s5_scan_kernel, n_ctx=n_ctx_rows),
        grid=(n_batch, width // lane_block),
        in_specs=[blk] * 4 + [pl.BlockSpec((4, lane_block), lambda b, j: (0, j))],
        out_specs=[blk] * 4,
        out_shape=[out] * 4,
        compiler_params=_params("parallel", "parallel"),
        name="s5_scan",
    )(*e_parts, lam_r)


def _s5_read_kernel(u_ref, hfr_ref, hfi_ref, hrr_ref, hri_ref, k_ref, m_ref, d_ref, o_ref):
    parts = _chunk_rows(u_ref)
    x = jnp.concatenate(parts, axis=1).astype(bf16)
    h = jnp.concatenate([hfr_ref[...], hfi_ref[...], hrr_ref[...], hri_ref[...]], axis=1).astype(bf16)
    y = _dot(x, k_ref[0]) + _dot(h, m_ref[0])
    n_rows = u_ref.shape[0] // S5_CHUNK
    for i in range(S5_CHUNK):
        o_ref[pl.ds(i, n_rows, stride=S5_CHUNK), :] = y[:, i * 128:(i + 1) * 128] + parts[i] * d_ref[...]


def _s5_read_call(z, col_block0, h_parts, k_oct, m_read, d_skip, steps):
    n = z.shape[0]
    n_oct = k_oct.shape[0]
    sw = S5_OCT * S5_STATE
    rows = steps // S5_CHUNK
    hspec = pl.BlockSpec((rows, sw), lambda i, k: (i, k))
    return pl.pallas_call(
        _s5_read_kernel,
        grid=(n // steps, n_oct),
        in_specs=[pl.BlockSpec((steps, 128), lambda i, k: (i, col_block0 + k))] + [hspec] * 4 +
                 [pl.BlockSpec((1,) + k_oct.shape[1:], lambda i, k: (k, 0, 0)),
                  pl.BlockSpec((1,) + m_read.shape[1:], lambda i, k: (k, 0, 0)),
                  pl.BlockSpec((1, 128), lambda i, k: (0, k))],
        out_specs=pl.BlockSpec((steps, 128), lambda i, k: (i, k)),
        out_shape=jax.ShapeDtypeStruct((n, n_oct * 128), f32),
        compiler_params=_params("parallel", "arbitrary"),
        name="s5_readout",
    )(z, *h_parts, k_oct, m_read, d_skip.reshape(1, -1))


def _top_values(s, k):
    vals = []
    for _ in range(k):
        m = jnp.max(s, axis=0, keepdims=True)
        vals.append(m)
        s = jnp.where(s == m, -jnp.inf, s)
    return vals


def _peer_route_kernel(q_ref, keys_ref, s0_ref, s1_ref, e0_ref, e1_ref, th_ref):
    s0 = _dot_nt(keys_ref[0], q_ref[:, 0:PEER_KEYS])
    s1 = _dot_nt(keys_ref[1], q_ref[:, PEER_KEYS:2 * PEER_KEYS])
    top0 = _top_values(s0, PEER_TOPK)
    top1 = jnp.concatenate(_top_values(s1, PEER_TOPK), axis=0)
    cand = jnp.concatenate([t + top1 for t in top0], axis=0)
    best = _top_values(cand, PEER_TOPK)
    z = None
    for v in best:
        e = jnp.exp(v - best[0])
        z = e if z is None else z + e
    s0_ref[0] = s0
    s1_ref[0] = s1
    e0_ref[0] = jnp.exp(s0 - top0[0]) / z
    e1_ref[0] = jnp.exp(s1 - top1[0:1])
    th_ref[0] = best[-1]


def _peer_route_call(q, keys, tt=256):
    n = q.shape[0]
    n_heads = keys.shape[0] // 2
    big = jax.ShapeDtypeStruct((n_heads, PEER_KEYS, n), f32)
    bspec = pl.BlockSpec((1, PEER_KEYS, tt), lambda t, h: (h, 0, t))
    return pl.pallas_call(
        _peer_route_kernel,
        grid=(n // tt, n_heads),
        in_specs=[pl.BlockSpec((tt, 2 * PEER_KEYS), lambda t, h: (t, h)),
                  pl.BlockSpec((2, PEER_KEYS, PEER_KEYS), lambda t, h: (h, 0, 0))],
        out_specs=[bspec] * 4 + [pl.BlockSpec((1, 1, tt), lambda t, h: (h, 0, t))],
        out_shape=[big] * 4 + [jax.ShapeDtypeStruct((n_heads, 1, n), f32)],
        compiler_params=_params("parallel", "arbitrary"),
        name="peer_route",
    )(q, keys)


PEER_I_PER_TILE = 4


def _peer_dense_kernel(xm_ref, u_ref, vt_ref, s0_ref, e0_ref, s1_ref, e1_ref, th_ref, o_ref):
    e = pl.program_id(1)

    @pl.when(e == 0)
    def _():
        o_ref[...] = jnp.zeros(o_ref.shape, f32)

    n_heads = s1_ref.shape[0]
    sub = (e % (8 // PEER_I_PER_TILE)) * PEER_I_PER_TILE
    weights = []
    for il in range(PEER_I_PER_TILE):
        w = None
        for h in range(n_heads):
            s0 = s0_ref[h, 0, pl.ds(sub + il, 1), :]
            e0 = e0_ref[h, 0, pl.ds(sub + il, 1), :]
            keep = (s0 + s1_ref[h]) >= th_ref[h]
            term = jnp.where(keep, e0 * e1_ref[h], 0.0)
            w = term if w is None else w + term
        weights.append(w)
    w_t = jnp.concatenate(weights, axis=0)
    s_t = _dot_nt(u_ref[...], xm_ref[...])
    act = (_gelu(s_t) * w_t).astype(bf16)
    o_ref[...] += _dot(vt_ref[...], act)


def _peer_dense_call(xm, u_tab, vt_tab, s0, e0, s1, e1, th, tm=512):
    n, d = xm.shape
    n_exp = u_tab.shape[0]
    te = PEER_I_PER_TILE * PEER_KEYS
    n_heads = s1.shape[0]
    s0 = s0.reshape(n_heads, PEER_KEYS // 8, 8, n)
    e0 = e0.reshape(n_heads, PEER_KEYS // 8, 8, n)
    per8 = 8 // PEER_I_PER_TILE
    row8 = pl.BlockSpec((n_heads, 1, 8, tm), lambda t, e: (0, e // per8, 0, t))
    full = pl.BlockSpec((n_heads, PEER_KEYS, tm), lambda t, e: (0, 0, t))
    return pl.pallas_call(
        _peer_dense_kernel,
        grid=(n // tm, n_exp // te),
        in_specs=[pl.BlockSpec((tm, d), lambda t, e: (t, 0)),
                  pl.BlockSpec((te, d), lambda t, e: (e, 0)),
                  pl.BlockSpec((d, te), lambda t, e: (0, e)),
                  row8, row8, full, full,
                  pl.BlockSpec((n_heads, 1, tm), lambda t, e: (0, 0, t))],
        out_specs=pl.BlockSpec((d, tm), lambda t, e: (0, t)),
        out_shape=jax.ShapeDtypeStruct((d, n), f32),
        compiler_params=_params("parallel", "arbitrary", vmem=60 * 1024 * 1024),
        name="peer_dense",
    )(xm, u_tab, vt_tab, s0, e0, s1, e1, th)


def kernel(x, c, ctx, c_ctx, w_ada, b_ada, w_in, a_q_norm, a_k_norm, b_sink, lru_conv_w, lru_conv_b, lru_w_r, lru_b_r, lru_w_i, lru_b_i, lru_lambda, s5_a_re, s5_a_im, s5_log_step, s5_b_re, s5_b_im, s5_c_re, s5_c_im, s5_d, s5_w_glu, s5_b_glu, w_gate, b_gate, w_branch, w_out, ln_g, ln_b, peer_w_q, peer_sub_keys, peer_u, peer_v):
    n_batch, n_lat, d = x.shape
    n_ctx = ctx.shape[1]
    depth = w_ada.shape[0]
    seq_tot = n_ctx + n_lat
    n_tok = n_batch * seq_tot
    mix_w = d // 4
    assert n_ctx % ROW_TILE == 0 and n_lat % ROW_TILE == 0 and n_tok % MM_TILE_M == 0
    assert n_ctx == ROW_TILE, "scan kernels treat row tile 0 of every batch element as the context"
    tiles_per_batch = seq_tot // ROW_TILE
    geom = (tiles_per_batch, n_ctx // ROW_TILE, n_batch)
    alpha = (2.0 * depth) ** 0.25
    s5_steps = seq_tot // 4
    assert seq_tot % 4 == 0 and s5_steps % (8 * S5_CHUNK) == 0 and n_ctx % S5_CHUNK == 0

    cin = jnp.zeros((8, d), f32).at[:n_batch].set(c).at[n_batch].set(c_ctx)
    mod = _ada_call(cin, w_ada, b_ada)

    def mod_vec(l, k):
        return mod[l, :, k * d:(k + 1) * d].reshape(8, 1, d)

    cos, sin = _rope_tables(n_lat, n_ctx)
    xs = jnp.concatenate([ctx, x], axis=1).reshape(n_tok, d)
    um = _mod_call(xs, mod_vec(0, 1), mod_vec(0, 0), geom)

    qkv_cols = N_QKV_SLICES * HEAD_DIM
    for l in range(depth):
        z = _mm_call(um, w_in[l].astype(bf16), f32, name="in_proj")
        gains = jnp.ones((N_QKV_SLICES, HEAD_DIM), f32)
        gains = gains.at[A_Q:A_K].set(a_q_norm[l]).at[A_K:A_V].set(a_k_norm[l]).reshape(N_QKV_SLICES, 1, HEAD_DIM)
        qkv = _prep_call(z, cos, sin, gains, n_batch, seq_tot)
        ya = _attn_a_call(qkv, n_ctx).reshape(n_tok, mix_w)
        yb = _attn_b_call(qkv, b_sink[l], n_ctx).reshape(n_tok, mix_w)

        lru_x_block = qkv_cols // mix_w
        a_coef, b_coef = _lru_gate_call(
            z, lru_x_block, lru_conv_w[l], lru_conv_b[l], lru_w_r[l].astype(bf16), lru_b_r[l],
            lru_w_i[l].astype(bf16), lru_b_i[l], jax.nn.softplus(-lru_lambda[l]), tiles_per_batch)
        hf, hr = _lru_scan_call(a_coef, b_coef, n_batch, tiles_per_batch)
        yr = _lru_out_call(hf, hr, z, lru_x_block + 1)

        s5_block0 = (qkv_cols + 2 * mix_w) // 128
        k_oct, w_end, m_read, lam_r = _s5_weights(s5_a_re[l], s5_a_im[l], s5_log_step[l], s5_b_re[l],
                                                  s5_b_im[l], s5_c_re[l], s5_c_im[l])
        e_parts = _s5_inject_call(z, s5_block0, w_end, s5_steps)
        h_parts = _s5_scan_call(e_parts, lam_r, n_batch, n_ctx // S5_CHUNK)
        y_s5 = _s5_read_call(z, s5_block0, h_parts, k_oct, m_read, s5_d[l], s5_steps)
        ys = _glu_call(y_s5, s5_w_glu[l].astype(bf16), s5_b_glu[l])

        merged = _merge_call(um, (ya, yb, yr, ys), w_gate[l].astype(bf16), b_gate[l], w_branch[l].astype(bf16))
        y = _mm_call(merged, w_out[l].astype(bf16), f32, name="out_proj")
        xs, um = _ln_call(xs, y, mod_vec(l, 2), ln_g[l, 0], ln_b[l, 0], mod_vec(l, 4), mod_vec(l, 3), geom, alpha)

        q = _mm_call(um, peer_w_q[l].astype(bf16), bf16, name="peer_query")
        n_heads = peer_sub_keys.shape[1]
        keys = peer_sub_keys[l].astype(bf16).reshape(2 * n_heads, PEER_KEYS, -1)
        s0, s1, e0, e1, th = _peer_route_call(q, keys)
        f_t = _peer_dense_call(um, peer_u[l].astype(bf16), peer_v[l].T.astype(bf16), s0, e0, s1, e1, th)
        nxt = min(l + 1, depth - 1)
        xs, um = _ln_call(xs, f_t.T, mod_vec(l, 5), ln_g[l, 1], ln_b[l, 1], mod_vec(nxt, 1), mod_vec(nxt, 0), geom, alpha)

    return xs.reshape(n_batch, seq_tot, d)[:, n_ctx:, :]
```

```python
import functools
import math

import jax
import jax.numpy as jnp
from jax import lax
from jax.experimental import pallas as pl
from jax.experimental.pallas import tpu as pltpu

f32 = jnp.float32
bf16 = jnp.bfloat16

HEAD_DIM = 128
GRID_W = 64
ROPE_THETA = 10000.0
WINDOW = 128
Q_TILE = 128
KV_GROUP = 4
LRU_C = 8.0
LRU_BLOCKS = 8
S5_GROUP = 16
S5_STATE = 64
S5_CHUNK = 8
S5_OCT = 8
PEER_TOPK = 16
PEER_KEYS = 128
LN_EPS = 1e-6
NEG = -1e30
ROW_TILE = 256
MM_TILE_M = 512
V7X_VMEM_LIMIT = 56 * 1024 * 1024


def _params(*sem, vmem=V7X_VMEM_LIMIT):
    return pltpu.CompilerParams(dimension_semantics=sem, vmem_limit_bytes=vmem)


def _dot(a, b):
    return jnp.dot(a, b, preferred_element_type=f32)


def _dot_nt(a, b):
    return lax.dot_general(a, b, (((1,), (1,)), ((), ())), preferred_element_type=f32)


def _gelu(x):
    return x * (0.5 * (1.0 + jnp.tanh(math.sqrt(2.0 / math.pi) * (x + 0.044715 * (x * x * x)))))


def _sigmoid(x):
    return 1.0 / (1.0 + jnp.exp(-x))


def _ada_kernel(c_ref, w_ref, b_ref, o_ref):
    c = c_ref[...]
    a = (c * _sigmoid(c)).astype(bf16)
    o_ref[0] = _dot(a, w_ref[0].astype(bf16)) + b_ref[0]


def _ada_call(cin, w_ada, b_ada, tn=512):
    depth, d, n = w_ada.shape
    return pl.pallas_call(
        _ada_kernel,
        grid=(depth, n // tn),
        in_specs=[pl.BlockSpec((8, d), lambda l, j: (0, 0)),
                  pl.BlockSpec((1, d, tn), lambda l, j: (l, 0, j)),
                  pl.BlockSpec((1, 1, tn), lambda l, j: (l, 0, j))],
        out_specs=pl.BlockSpec((1, 8, tn), lambda l, j: (l, 0, j)),
        out_shape=jax.ShapeDtypeStruct((depth, 8, n), f32),
        compiler_params=_params("parallel", "parallel"),
        name="adaln",
    )(cin, w_ada, b_ada.reshape(depth, 1, n))


def _row_group(i, tiles_per_batch, ctx_tiles, n_batch):
    return jnp.where(i % tiles_per_batch < ctx_tiles, n_batch, i // tiles_per_batch)


def _mod_kernel(x_ref, sc_ref, sh_ref, o_ref):
    o_ref[...] = (x_ref[...] * (1.0 + sc_ref[0]) + sh_ref[0]).astype(o_ref.dtype)


def _mod_call(x, sc, sh, geom):
    n, d = x.shape
    tpb, ct, nb = geom
    sel = lambda i: (_row_group(i, tpb, ct, nb), 0, 0)
    return pl.pallas_call(
        _mod_kernel,
        grid=(n // ROW_TILE,),
        in_specs=[pl.BlockSpec((ROW_TILE, d), lambda i: (i, 0)),
                  pl.BlockSpec((1, 1, d), sel), pl.BlockSpec((1, 1, d), sel)],
        out_specs=pl.BlockSpec((ROW_TILE, d), lambda i: (i, 0)),
        out_shape=jax.ShapeDtypeStruct((n, d), bf16),
        compiler_params=_params("parallel"),
        name="modulate",
    )(x, sc, sh)


def _ln_kernel(x_ref, y_ref, g_ref, lg_ref, lb_ref, sc_ref, sh_ref, xo_ref, xm_ref, *, alpha):
    v = alpha * x_ref[...] + g_ref[0] * y_ref[...]
    mu = jnp.mean(v, axis=-1, keepdims=True)
    vc = v - mu
    var = jnp.mean(vc * vc, axis=-1, keepdims=True)
    o = vc * lax.rsqrt(var + LN_EPS) * lg_ref[...] + lb_ref[...]
    xo_ref[...] = o
    xm_ref[...] = (o * (1.0 + sc_ref[0]) + sh_ref[0]).astype(xm_ref.dtype)


def _ln_call(x, y, gate, ln_g, ln_b, sc, sh, geom, alpha):
    n, d = x.shape
    tpb, ct, nb = geom
    sel = lambda i: (_row_group(i, tpb, ct, nb), 0, 0)
    row = pl.BlockSpec((ROW_TILE, d), lambda i: (i, 0))
    vec = pl.BlockSpec((1, d), lambda i: (0, 0))
    return pl.pallas_call(
        functools.partial(_ln_kernel, alpha=alpha),
        grid=(n // ROW_TILE,),
        in_specs=[row, row, pl.BlockSpec((1, 1, d), sel), vec, vec,
                  pl.BlockSpec((1, 1, d), sel), pl.BlockSpec((1, 1, d), sel)],
        out_specs=[row, row],
        out_shape=[jax.ShapeDtypeStruct((n, d), f32), jax.ShapeDtypeStruct((n, d), bf16)],
        compiler_params=_params("parallel"),
        name="residual_ln",
    )(x, y, gate, ln_g.reshape(1, d), ln_b.reshape(1, d), sc, sh)


def _mm_kernel(a_ref, b_ref, o_ref):
    o_ref[...] = _dot(a_ref[...], b_ref[...]).astype(o_ref.dtype)


def _mm_call(a, b, layer, out_dtype, tn=512, tm=MM_TILE_M, name="matmul"):
    m, k = a.shape
    n = b.shape[2]
    return pl.pallas_call(
        _mm_kernel,
        grid=(m // tm, n // tn),
        in_specs=[pl.BlockSpec((tm, k), lambda i, j: (i, 0)),
                  pl.BlockSpec((None, k, tn), lambda i, j: (layer, 0, j))],
        out_specs=pl.BlockSpec((tm, tn), lambda i, j: (i, j)),
        out_shape=jax.ShapeDtypeStruct((m, n), out_dtype),
        compiler_params=_params("parallel", "arbitrary"),
        name=name,
    )(a, b)


def _glu_kernel(a_ref, wv_ref, wg_ref, bv_ref, bg_ref, o_ref):
    a = a_ref[...].astype(bf16)
    val = _dot(a, wv_ref[...]) + bv_ref[...]
    gate = _dot(a, wg_ref[...]) + bg_ref[...]
    o_ref[...] = (val * _sigmoid(gate)).astype(o_ref.dtype)


def _glu_call(a, w, layer, b, tn=512, tm=MM_TILE_M):
    m, k = a.shape
    n = w.shape[2] // 2
    nj = n // tn
    b2 = b.reshape(1, 2 * n)
    return pl.pallas_call(
        _glu_kernel,
        grid=(m // tm, nj),
        in_specs=[pl.BlockSpec((tm, k), lambda i, j: (i, 0)),
                  pl.BlockSpec((None, k, tn), lambda i, j: (layer, 0, j)),
                  pl.BlockSpec((None, k, tn), lambda i, j: (layer, 0, j + nj)),
                  pl.BlockSpec((1, tn), lambda i, j: (0, j)),
                  pl.BlockSpec((1, tn), lambda i, j: (0, j + nj))],
        out_specs=pl.BlockSpec((tm, tn), lambda i, j: (i, j)),
        out_shape=jax.ShapeDtypeStruct((m, n), bf16),
        compiler_params=_params("parallel", "arbitrary"),
        name="s5_glu",
    )(a, w, w, b2, b2)


def _merge_kernel(u_ref, ya_ref, yb_ref, yr_ref, ys_ref, wg_ref, bg_ref, wb_ref, o_ref):
    u = u_ref[...]
    acc = None
    for i, y_ref in enumerate((ya_ref, yb_ref, yr_ref, ys_ref)):
        gate = _sigmoid(_dot(u, wg_ref[i]) + bg_ref[i])
        term = gate * _dot(y_ref[...], wb_ref[i])
        acc = term if acc is None else acc + term
    o_ref[...] = acc.astype(o_ref.dtype)


def _merge_call(u, branches, w_gate, b_gate, w_branch, layer, tn=256, tm=MM_TILE_M):
    m, d = u.shape
    _, nbr, bw, n = w_branch.shape
    ybr = pl.BlockSpec((tm, bw), lambda i, j: (i, 0))
    return pl.pallas_call(
        _merge_kernel,
        grid=(m // tm, n // tn),
        in_specs=[pl.BlockSpec((tm, d), lambda i, j: (i, 0)), ybr, ybr, ybr, ybr,
                  pl.BlockSpec((None, nbr, d, tn), lambda i, j: (layer, 0, 0, j)),
                  pl.BlockSpec((nbr, 1, tn), lambda i, j: (0, 0, j)),
                  pl.BlockSpec((None, nbr, bw, tn), lambda i, j: (layer, 0, 0, j))],
        out_specs=pl.BlockSpec((tm, tn), lambda i, j: (i, j)),
        out_shape=jax.ShapeDtypeStruct((m, n), bf16),
        compiler_params=_params("parallel", "arbitrary"),
        name="branch_merge",
    )(u, *branches, w_gate, b_gate.reshape(nbr, 1, n), w_branch)


def _rope_tables(n_lat, n_ctx):
    t = jnp.arange(n_lat)
    row = (t // GRID_W).astype(f32)
    col = (t % GRID_W).astype(f32)
    n_freq = HEAD_DIM // 4
    inv = ROPE_THETA ** (-jnp.arange(n_freq, dtype=f32) / n_freq)
    ang = jnp.concatenate([row[:, None] * inv, col[:, None] * inv], axis=-1)
    cos = jnp.repeat(jnp.cos(ang), 2, axis=-1)
    sin = jnp.repeat(jnp.sin(ang), 2, axis=-1) * jnp.tile(jnp.array([-1.0, 1.0], f32), HEAD_DIM // 2)
    cos = jnp.concatenate([jnp.ones((n_ctx, HEAD_DIM), f32), cos], axis=0)
    sin = jnp.concatenate([jnp.zeros((n_ctx, HEAD_DIM), f32), sin], axis=0)
    return cos, sin


N_QKV_SLICES = 24
A_Q, A_K, A_V, B_Q, B_K, B_V = 0, 8, 10, 12, 20, 22


def _prep_kernel(z_ref, cos_ref, sin_ref, gq_ref, gk_ref, o_ref):
    cos, sin = cos_ref[...], sin_ref[...]
    even = lax.broadcasted_iota(jnp.int32, cos.shape, 1) % 2 == 0
    for s in range(N_QKV_SLICES):
        y = z_ref[:, s * HEAD_DIM:(s + 1) * HEAD_DIM]
        if s < A_V:
            gain = gq_ref[...] if s < A_K else gk_ref[...]
            y = y * lax.rsqrt(jnp.mean(y * y, axis=-1, keepdims=True) + LN_EPS) * gain
        if s < A_V or B_Q <= s < B_V:
            partner = jnp.where(even, pltpu.roll(y, HEAD_DIM - 1, axis=1), pltpu.roll(y, 1, axis=1))
            y = y * cos + partner * sin
        if s < A_K or B_Q <= s < B_K:
            y = y * (HEAD_DIM ** -0.5)
        o_ref[0, s] = y.astype(o_ref.dtype)


def _prep_call(z, cos, sin, gain_q, gain_k, n_batch, seq_tot):
    n = z.shape[0]
    tr = ROW_TILE
    tpb = seq_tot // tr
    width = N_QKV_SLICES * HEAD_DIM
    vec = pl.BlockSpec((1, HEAD_DIM), lambda i: (0, 0))
    return pl.pallas_call(
        _prep_kernel,
        grid=(n // tr,),
        in_specs=[pl.BlockSpec((tr, width), lambda i: (i, 0)),
                  pl.BlockSpec((tr, HEAD_DIM), lambda i: (i % tpb, 0)),
                  pl.BlockSpec((tr, HEAD_DIM), lambda i: (i % tpb, 0)), vec, vec],
        out_specs=pl.BlockSpec((1, N_QKV_SLICES, tr, HEAD_DIM), lambda i: (i // tpb, 0, i % tpb, 0)),
        out_shape=jax.ShapeDtypeStruct((n_batch, N_QKV_SLICES, seq_tot, HEAD_DIM), bf16),
        compiler_params=_params("parallel"),
        name="qkv_prep",
    )(z, cos, sin, gain_q.reshape(1, HEAD_DIM), gain_k.reshape(1, HEAD_DIM))


def _attn_a_kernel(q_ref, k_ref, v_ref, o_ref, m_s, l_s, acc_s, *, n_ctx, tk, n_lat_chunks):
    q = q_ref[0].reshape(KV_GROUP * Q_TILE, HEAD_DIM)

    k, v = k_ref[0, 0, 0:n_ctx, :], v_ref[0, 0, 0:n_ctx, :]
    s = _dot_nt(q, k)
    m0 = jnp.max(s, axis=-1, keepdims=True)
    p = jnp.exp(s - m0)
    m_s[...] = m0
    l_s[...] = jnp.sum(p, axis=-1, keepdims=True)
    acc_s[...] = _dot(p.astype(bf16), v)

    @pl.when(pl.program_id(2) >= n_ctx // Q_TILE)
    def _():
        for c in range(n_lat_chunks):
            lo = n_ctx + c * tk
            k, v = k_ref[0, 0, lo:lo + tk, :], v_ref[0, 0, lo:lo + tk, :]
            s = _dot_nt(q, k)
            m_prev = m_s[...]
            m_new = jnp.maximum(m_prev, jnp.max(s, axis=-1, keepdims=True))
            alpha = jnp.exp(m_prev - m_new)
            p = jnp.exp(s - m_new)
            l_s[...] = alpha * l_s[...] + jnp.sum(p, axis=-1, keepdims=True)
            acc_s[...] = alpha * acc_s[...] + _dot(p.astype(bf16), v)
            m_s[...] = m_new

    out = acc_s[...] / l_s[...]
    for g in range(KV_GROUP):
        o_ref[0, :, g * HEAD_DIM:(g + 1) * HEAD_DIM] = out[g * Q_TILE:(g + 1) * Q_TILE].astype(o_ref.dtype)


def _attn_a_call(qkv, n_ctx, tk=1024):
    nb, _, seq_tot, _ = qkv.shape
    n_kv = (A_V - A_K)
    rows = KV_GROUP * Q_TILE
    n_lat = seq_tot - n_ctx
    tk = min(tk, n_lat)
    assert n_lat % tk == 0
    kern = functools.partial(_attn_a_kernel, n_ctx=n_ctx, tk=tk, n_lat_chunks=n_lat // tk)
    return pl.pallas_call(
        kern,
        grid=(nb, n_kv, seq_tot // Q_TILE),
        in_specs=[pl.BlockSpec((1, KV_GROUP, Q_TILE, HEAD_DIM), lambda b, h, i: (b, A_Q // KV_GROUP + h, i, 0)),
                  pl.BlockSpec((1, 1, seq_tot, HEAD_DIM), lambda b, h, i: (b, A_K + h, 0, 0)),
                  pl.BlockSpec((1, 1, seq_tot, HEAD_DIM), lambda b, h, i: (b, A_V + h, 0, 0))],
        out_specs=pl.BlockSpec((1, Q_TILE, KV_GROUP * HEAD_DIM), lambda b, h, i: (b, i, h)),
        out_shape=jax.ShapeDtypeStruct((nb, seq_tot, n_kv * KV_GROUP * HEAD_DIM), bf16),
        scratch_shapes=[pltpu.VMEM((rows, 1), f32), pltpu.VMEM((rows, 1), f32),
                        pltpu.VMEM((rows, HEAD_DIM), f32)],
        compiler_params=_params("parallel", "parallel", "arbitrary"),
        name="dense_attention",
    )(qkv, qkv, qkv)


def _attn_b_kernel(q_ref, k_ref, v_ref, sink_ref, o_ref, *, n_ctx, n_blk):
    i = pl.program_id(2)
    blk = i - n_ctx // Q_TILE
    rows = KV_GROUP * Q_TILE
    q = q_ref[0].reshape(rows, HEAD_DIM)
    sink = sink_ref[0]
    kc = k_ref[0, 0, 0:n_ctx, :]
    vc = v_ref[0, 0, 0:n_ctx, :]
    s_ctx = _dot_nt(q, kc)
    m = jnp.maximum(jnp.max(s_ctx, axis=-1, keepdims=True), sink)
    q_pos = lax.broadcasted_iota(jnp.int32, (rows, Q_TILE), 0) % Q_TILE
    k_off = lax.broadcasted_iota(jnp.int32, (rows, Q_TILE), 1)
    s_loc, v_loc = [], []
    for d in (-1, 0, 1):
        j = blk + d
        valid = jnp.logical_and(blk >= 0, jnp.logical_and(j >= 0, j < n_blk))
        start = pl.multiple_of(n_ctx + jnp.clip(j, 0, n_blk - 1) * Q_TILE, Q_TILE)
        k = k_ref[0, 0, pl.ds(start, Q_TILE), :]
        v_loc.append(v_ref[0, 0, pl.ds(start, Q_TILE), :])
        keep = jnp.logical_and(jnp.abs(k_off + d * Q_TILE - q_pos) <= WINDOW, valid)
        s = jnp.where(keep, _dot_nt(q, k), NEG)
        m = jnp.maximum(m, jnp.max(s, axis=-1, keepdims=True))
        s_loc.append(s)
    p = jnp.exp(s_ctx - m)
    den = jnp.exp(sink - m) + jnp.sum(p, axis=-1, keepdims=True)
    acc = _dot(p.astype(bf16), vc)
    for s, v in zip(s_loc, v_loc):
        p = jnp.exp(s - m)
        den = den + jnp.sum(p, axis=-1, keepdims=True)
        acc = acc + _dot(p.astype(bf16), v)
    out = acc / den
    for g in range(KV_GROUP):
        o_ref[0, :, g * HEAD_DIM:(g + 1) * HEAD_DIM] = out[g * Q_TILE:(g + 1) * Q_TILE].astype(o_ref.dtype)


def _attn_b_call(qkv, sink, n_ctx):
    nb, _, seq_tot, _ = qkv.shape
    n_kv = (B_V - B_K)
    rows = KV_GROUP * Q_TILE
    sink_rows = jnp.repeat(sink.astype(f32).reshape(n_kv, KV_GROUP), Q_TILE, axis=1).reshape(n_kv, rows, 1)
    kern = functools.partial(_attn_b_kernel, n_ctx=n_ctx, n_blk=(seq_tot - n_ctx) // Q_TILE)
    return pl.pallas_call(
        kern,
        grid=(nb, n_kv, seq_tot // Q_TILE),
        in_specs=[pl.BlockSpec((1, KV_GROUP, Q_TILE, HEAD_DIM), lambda b, h, i: (b, B_Q // KV_GROUP + h, i, 0)),
                  pl.BlockSpec((1, 1, seq_tot, HEAD_DIM), lambda b, h, i: (b, B_K + h, 0, 0)),
                  pl.BlockSpec((1, 1, seq_tot, HEAD_DIM), lambda b, h, i: (b, B_V + h, 0, 0)),
                  pl.BlockSpec((1, rows, 1), lambda b, h, i: (h, 0, 0))],
        out_specs=pl.BlockSpec((1, Q_TILE, KV_GROUP * HEAD_DIM), lambda b, h, i: (b, i, h)),
        out_shape=jax.ShapeDtypeStruct((nb, seq_tot, n_kv * KV_GROUP * HEAD_DIM), bf16),
        compiler_params=_params("parallel", "parallel", "arbitrary"),
        name="window_attention",
    )(qkv, qkv, qkv, sink_rows)


HALO = 8


def _lru_gate_kernel(x_ref, xp_ref, xn_ref, cw_ref, cb_ref, wr_ref, br_ref, wi_ref, bi_ref, sp_ref,
                     a_ref, b_ref, pad_s, *, tiles_per_batch):
    tr = x_ref.shape[0]
    t = pl.program_id(0) % tiles_per_batch
    has_prev = t >= 2
    has_next = jnp.logical_and(t >= 1, t < tiles_per_batch - 1)
    x = x_ref[...]
    pad_s[0:HALO, :] = jnp.where(has_prev, xp_ref[...], 0.0)
    pad_s[HALO:HALO + tr, :] = x
    pad_s[HALO + tr:2 * HALO + tr, :] = jnp.where(has_next, xn_ref[...], 0.0)
    cw = cw_ref[...]
    xl = cb_ref[...] + pad_s[HALO - 2:HALO - 2 + tr, :] * cw[0:1]
    xl = xl + pad_s[HALO - 1:HALO - 1 + tr, :] * cw[1:2]
    xl = xl + x * cw[2:3]
    xl = xl + pad_s[HALO + 1:HALO + 1 + tr, :] * cw[3:4]
    bw = xl.shape[1] // LRU_BLOCKS
    for n in range(LRU_BLOCKS):
        cols = slice(n * bw, (n + 1) * bw)
        xb = xl[:, cols]
        xb16 = xb.astype(bf16)
        for d in range(2):
            r = _sigmoid(_dot(xb16, wr_ref[d, n]) + br_ref[d:d + 1, cols])
            gi = _sigmoid(_dot(xb16, wi_ref[d, n]) + bi_ref[d:d + 1, cols])
            log_a = (-LRU_C) * r * sp_ref[d:d + 1, cols]
            a = jnp.exp(log_a)
            a_ref[d, :, cols] = a
            b_ref[d, :, cols] = jnp.sqrt(1.0 - a * a) * gi * xb


def _lru_gate_call(z, col_block, conv_w, conv_b, w_r, b_r, w_i, b_i, softplus_lam, tiles_per_batch):
    n = z.shape[0]
    w = conv_w.shape[1]
    tr = ROW_TILE
    hb = tr // HALO
    n_halo = n // HALO
    full = lambda shape: pl.BlockSpec(shape, lambda i: (0,) * len(shape))
    kern = functools.partial(_lru_gate_kernel, tiles_per_batch=tiles_per_batch)
    out = jax.ShapeDtypeStruct((2, n, w), f32)
    return pl.pallas_call(
        kern,
        grid=(n // tr,),
        in_specs=[pl.BlockSpec((tr, w), lambda i: (i, col_block)),
                  pl.BlockSpec((HALO, w), lambda i: (jnp.maximum(i * hb - 1, 0), col_block)),
                  pl.BlockSpec((HALO, w), lambda i: (jnp.minimum((i + 1) * hb, n_halo - 1), col_block)),
                  full(conv_w.shape), full((1, w)), full(w_r.shape), full(b_r.shape),
                  full(w_i.shape), full(b_i.shape), full(softplus_lam.shape)],
        out_specs=[pl.BlockSpec((2, tr, w), lambda i: (0, i, 0)), pl.BlockSpec((2, tr, w), lambda i: (0, i, 0))],
        out_shape=[out, out],
        scratch_shapes=[pltpu.VMEM((tr + 2 * HALO, w), f32)],
        compiler_params=_params("parallel"),
        name="lru_gates",
    )(z, z, z, conv_w, conv_b.reshape(1, w), w_r, b_r, w_i, b_i, softplus_lam)


def _lru_scan_kernel(af_ref, bf_ref, ar_ref, br_ref, hf_ref, hr_ref, sf, sr):
    tr = hf_ref.shape[0]

    @pl.when(pl.program_id(1) == 0)
    def _():
        sf[...] = jnp.zeros(sf.shape, f32)
        sr[...] = jnp.zeros(sr.shape, f32)

    def body(t, carry):
        hf, hr = carry
        hf = af_ref[0, pl.ds(t, 1), :] * hf + bf_ref[0, pl.ds(t, 1), :]
        hf_ref[pl.ds(t, 1), :] = hf
        u = tr - 1 - t
        hr = ar_ref[0, pl.ds(u, 1), :] * hr + br_ref[0, pl.ds(u, 1), :]
        hr_ref[pl.ds(u, 1), :] = hr
        return hf, hr

    hf, hr = lax.fori_loop(0, tr, body, (sf[...], sr[...]), unroll=8)
    sf[...] = hf
    sr[...] = hr


def _scan_tile_maps(tiles_per_batch):
    fwd = lambda b, i: b * tiles_per_batch + i
    rev = lambda b, i: b * tiles_per_batch + jnp.where(i == 0, 0, tiles_per_batch - i)
    return fwd, rev


def _lru_scan_call(a, b, n_batch, tiles_per_batch):
    _, n, w = a.shape
    tr = ROW_TILE
    fwd, rev = _scan_tile_maps(tiles_per_batch)
    out = jax.ShapeDtypeStruct((n, w), f32)
    return pl.pallas_call(
        _lru_scan_kernel,
        grid=(n_batch, tiles_per_batch),
        in_specs=[pl.BlockSpec((1, tr, w), lambda bb, i: (0, fwd(bb, i), 0)),
                  pl.BlockSpec((1, tr, w), lambda bb, i: (0, fwd(bb, i), 0)),
                  pl.BlockSpec((1, tr, w), lambda bb, i: (1, rev(bb, i), 0)),
                  pl.BlockSpec((1, tr, w), lambda bb, i: (1, rev(bb, i), 0))],
        out_specs=[pl.BlockSpec((tr, w), lambda bb, i: (fwd(bb, i), 0)),
                   pl.BlockSpec((tr, w), lambda bb, i: (rev(bb, i), 0))],
        out_shape=[out, out],
        scratch_shapes=[pltpu.VMEM((1, w), f32), pltpu.VMEM((1, w), f32)],
        compiler_params=_params("parallel", "arbitrary"),
        name="lru_scan",
    )(a, b, a, b)


def _lru_out_kernel(hf_ref, hr_ref, g_ref, o_ref):
    o_ref[...] = ((hf_ref[...] + hr_ref[...]) * _gelu(g_ref[...])).astype(o_ref.dtype)


def _lru_out_call(hf, hr, z, gate_col_block):
    n, w = hf.shape
    tr = ROW_TILE
    row = pl.BlockSpec((tr, w), lambda i: (i, 0))
    return pl.pallas_call(
        _lru_out_kernel,
        grid=(n // tr,),
        in_specs=[row, row, pl.BlockSpec((tr, w), lambda i: (i, gate_col_block))],
        out_specs=row,
        out_shape=jax.ShapeDtypeStruct((n, w), bf16),
        compiler_params=_params("parallel"),
        name="lru_out",
    )(hf, hr, z)


def _s5_weights(a_re, a_im, log_step, b_re, b_im, c_re, c_im):
    r = S5_CHUNK
    n_g = a_re.shape[1]
    n_oct = n_g // S5_OCT
    step = jnp.exp(log_step)[..., None]
    mag = jnp.exp(a_re * step)
    lb_re, lb_im = mag * jnp.cos(a_im * step), mag * jnp.sin(a_im * step)
    den = a_re * a_re + a_im * a_im
    num_re = lb_re - 1.0
    coef_re = (num_re * a_re + lb_im * a_im) / den
    coef_im = (lb_im * a_re - num_re * a_im) / den
    bb_re = coef_re[..., None] * b_re - coef_im[..., None] * b_im
    bb_im = coef_re[..., None] * b_im + coef_im[..., None] * b_re
    tau = jnp.arange(r + 1, dtype=f32)[:, None, None, None]
    pmag = jnp.exp(tau * (a_re * step))
    pw_re, pw_im = pmag * jnp.cos(tau * (a_im * step)), pmag * jnp.sin(tau * (a_im * step))

    lb_b_re = pw_re[..., None] * bb_re - pw_im[..., None] * bb_im
    lb_b_im = pw_re[..., None] * bb_im + pw_im[..., None] * bb_re
    lag = (jnp.einsum('tdgpc,dgop->tdgco', lb_b_re, c_re) - jnp.einsum('tdgpc,dgop->tdgco', lb_b_im, c_im))
    idx = jnp.arange(r)
    sh_f = (idx[None, None, :] - idx[None, :, None] == jnp.arange(r + 1)[:, None, None]).astype(f32)
    sh_r = (idx[None, :, None] - idx[None, None, :] == jnp.arange(r + 1)[:, None, None]).astype(f32)
    k_loc = jnp.einsum('tio,tgcd->gicod', sh_f, lag[:, 0]) + jnp.einsum('tio,tgcd->gicod', sh_r, lag[:, 1])
    eye = jnp.eye(S5_OCT, dtype=f32)
    k_loc = k_loc.reshape(n_oct, S5_OCT, r, S5_GROUP, r, S5_GROUP)
    k_oct = jnp.einsum('kgicod,gh->kigcohd', k_loc, eye).reshape(n_oct, r * 128, r * 128)

    def inject(d, powers):
        w_re = pw_re[powers, d][..., None] * bb_re[d] - pw_im[powers, d][..., None] * bb_im[d]
        w_im = pw_re[powers, d][..., None] * bb_im[d] + pw_im[powers, d][..., None] * bb_re[d]
        out = []
        for w in (w_re, w_im):
            w = w.reshape(r, n_oct, S5_OCT, S5_STATE, S5_GROUP)
            out.append(jnp.einsum('ikgpc,gh->kigchp', w, eye).reshape(n_oct, r * 128, S5_OCT * S5_STATE))
        return out
    w_end = jnp.concatenate(inject(0, idx[::-1]) + inject(1, idx), axis=-1)

    def readout(d, powers):
        cl_re = c_re[d][None] * pw_re[powers, d][:, :, None, :] - c_im[d][None] * pw_im[powers, d][:, :, None, :]
        cl_im = c_re[d][None] * pw_im[powers, d][:, :, None, :] + c_im[d][None] * pw_re[powers, d][:, :, None, :]
        out = []
        for w in (cl_re, -cl_im):
            w = w.reshape(r, n_oct, S5_OCT, S5_GROUP, S5_STATE)
            out.append(jnp.einsum('okgcp,gh->kgpohc', w, eye).reshape(n_oct, S5_OCT * S5_STATE, r * 128))
        return out
    m_read = jnp.concatenate(readout(0, idx + 1) + readout(1, r - idx), axis=1)

    lam_r = jnp.stack([pw_re[r, 0].reshape(-1), pw_im[r, 0].reshape(-1),
                       pw_re[r, 1].reshape(-1), pw_im[r, 1].reshape(-1)])
    return k_oct.astype(bf16), w_end.astype(bf16), m_read.astype(bf16), lam_r


def _chunk_rows(u_ref):
    n_rows = u_ref.shape[0] // S5_CHUNK
    return [u_ref[pl.ds(i, n_rows, stride=S5_CHUNK), :] for i in range(S5_CHUNK)]


def _s5_inject_kernel(u_ref, w_ref, efr_ref, efi_ref, err_ref, eri_ref):
    x = jnp.concatenate(_chunk_rows(u_ref), axis=1).astype(bf16)
    e = _dot(x, w_ref[0])
    w = efr_ref.shape[1]
    for k, o_ref in enumerate((efr_ref, efi_ref, err_ref, eri_ref)):
        o_ref[...] = e[:, k * w:(k + 1) * w]


def _s5_inject_call(z, col_block0, w_end, steps):
    n = z.shape[0]
    n_oct = w_end.shape[0]
    sw = S5_OCT * S5_STATE
    rows = steps // S5_CHUNK
    out = jax.ShapeDtypeStruct((n // S5_CHUNK, n_oct * sw), f32)
    ospec = pl.BlockSpec((rows, sw), lambda i, k: (i, k))
    return pl.pallas_call(
        _s5_inject_kernel,
        grid=(n // steps, n_oct),
        in_specs=[pl.BlockSpec((steps, 128), lambda i, k: (i, col_block0 + k)),
                  pl.BlockSpec((1,) + w_end.shape[1:], lambda i, k: (k, 0, 0))],
        out_specs=[ospec] * 4,
        out_shape=[out] * 4,
        compiler_params=_params("parallel", "arbitrary"),
        name="s5_inject",
    )(z, w_end)


def _s5_scan_kernel(efr_ref, efi_ref, err_ref, eri_ref, lam_ref, hfr_ref, hfi_ref, hrr_ref, hri_ref, *, n_ctx):
    n_rows = efr_ref.shape[0]
    lfr, lfi, lrr, lri = (lam_ref[k:k + 1, :] for k in range(4))
    zero = jnp.zeros((1, efr_ref.shape[1]), f32)

    def body(t, carry):
        fr, fi, rr, ri = carry
        hfr_ref[pl.ds(t, 1), :] = fr
        hfi_ref[pl.ds(t, 1), :] = fi
        er, ei = efr_ref[pl.ds(t, 1), :], efi_ref[pl.ds(t, 1), :]
        fr, fi = lfr * fr - lfi * fi + er, lfr * fi + lfi * fr + ei
        u = jnp.where(t < n_ctx, n_ctx - 1 - t, n_rows - 1 - (t - n_ctx))
        hrr_ref[pl.ds(u, 1), :] = rr
        hri_ref[pl.ds(u, 1), :] = ri
        er, ei = err_ref[pl.ds(u, 1), :], eri_ref[pl.ds(u, 1), :]
        rr, ri = lrr * rr - lri * ri + er, lrr * ri + lri * rr + ei
        return fr, fi, rr, ri

    lax.fori_loop(0, n_rows, body, (zero, zero, zero, zero), unroll=4)


def _s5_scan_call(e_parts, lam_r, n_batch, n_ctx_rows, lane_block=512):
    n_rows_tot, width = e_parts[0].shape
    rows = n_rows_tot // n_batch
    blk = pl.BlockSpec((rows, lane_block), lambda b, j: (b, j))
    out = jax.ShapeDtypeStruct((n_rows_tot, width), f32)
    return pl.pallas_call(
        functools.partial(_s5_scan_kernel, n_ctx=n_ctx_rows),
        grid=(n_batch, width // lane_block),
        in_specs=[blk] * 4 + [pl.BlockSpec((4, lane_block), lambda b, j: (0, j))],
        out_specs=[blk] * 4,
        out_shape=[out] * 4,
        compiler_params=_params("parallel", "parallel"),
        name="s5_scan",
    )(*e_parts, lam_r)


def _s5_read_kernel(u_ref, hfr_ref, hfi_ref, hrr_ref, hri_ref, k_ref, m_ref, d_ref, o_ref):
    parts = _chunk_rows(u_ref)
    x = jnp.concatenate(parts, axis=1).astype(bf16)
    h = jnp.concatenate([hfr_ref[...], hfi_ref[...], hrr_ref[...], hri_ref[...]], axis=1).astype(bf16)
    y = _dot(x, k_ref[0]) + _dot(h, m_ref[0])
    n_rows = u_ref.shape[0] // S5_CHUNK
    for i in range(S5_CHUNK):
        o_ref[pl.ds(i, n_rows, stride=S5_CHUNK), :] = y[:, i * 128:(i + 1) * 128] + parts[i] * d_ref[...]


def _s5_read_call(z, col_block0, h_parts, k_oct, m_read, d_skip, steps):
    n = z.shape[0]
    n_oct = k_oct.shape[0]
    sw = S5_OCT * S5_STATE
    rows = steps // S5_CHUNK
    hspec = pl.BlockSpec((rows, sw), lambda i, k: (i, k))
    return pl.pallas_call(
        _s5_read_kernel,
        grid=(n // steps, n_oct),
        in_specs=[pl.BlockSpec((steps, 128), lambda i, k: (i, col_block0 + k))] + [hspec] * 4 +
                 [pl.BlockSpec((1,) + k_oct.shape[1:], lambda i, k: (k, 0, 0)),
                  pl.BlockSpec((1,) + m_read.shape[1:], lambda i, k: (k, 0, 0)),
                  pl.BlockSpec((1, 128), lambda i, k: (0, k))],
        out_specs=pl.BlockSpec((steps, 128), lambda i, k: (i, k)),
        out_shape=jax.ShapeDtypeStruct((n, n_oct * 128), f32),
        compiler_params=_params("parallel", "arbitrary"),
        name="s5_readout",
    )(z, *h_parts, k_oct, m_read, d_skip.reshape(1, -1))


def _top_values(s, k):
    vals = []
    for _ in range(k):
        m = jnp.max(s, axis=0, keepdims=True)
        vals.append(m)
        s = jnp.where(s == m, -jnp.inf, s)
    return vals


def _peer_route_kernel(q_ref, keys_ref, s0_ref, s1_ref, e0_ref, e1_ref, th_ref):
    s0 = _dot_nt(keys_ref[0], q_ref[:, 0:PEER_KEYS])
    s1 = _dot_nt(keys_ref[1], q_ref[:, PEER_KEYS:2 * PEER_KEYS])
    top0 = _top_values(s0, PEER_TOPK)
    top1 = jnp.concatenate(_top_values(s1, PEER_TOPK), axis=0)
    cand = jnp.concatenate([t + top1 for t in top0], axis=0)
    best = _top_values(cand, PEER_TOPK)
    z = None
    for v in best:
        e = jnp.exp(v - best[0])
        z = e if z is None else z + e
    s0_ref[0] = s0.reshape(s0_ref.shape[1:])
    s1_ref[0] = s1
    e0_ref[0] = (jnp.exp(s0 - top0[0]) / z).reshape(e0_ref.shape[1:])
    e1_ref[0] = jnp.exp(s1 - top1[0:1])
    th_ref[0] = best[-1]


def _peer_route_call(q, keys, tt=256):
    n = q.shape[0]
    n_heads = keys.shape[0] // 2
    big = jax.ShapeDtypeStruct((n_heads, PEER_KEYS, n), f32)
    bspec = pl.BlockSpec((1, PEER_KEYS, tt), lambda t, h: (h, 0, t))
    grp = jax.ShapeDtypeStruct((n_heads, PEER_KEYS // 8, 8, n), f32)
    gspec = pl.BlockSpec((1, PEER_KEYS // 8, 8, tt), lambda t, h: (h, 0, 0, t))
    return pl.pallas_call(
        _peer_route_kernel,
        grid=(n // tt, n_heads),
        in_specs=[pl.BlockSpec((tt, 2 * PEER_KEYS), lambda t, h: (t, h)),
                  pl.BlockSpec((2, PEER_KEYS, PEER_KEYS), lambda t, h: (h, 0, 0))],
        out_specs=[gspec, bspec, gspec, bspec, pl.BlockSpec((1, 1, tt), lambda t, h: (h, 0, t))],
        out_shape=[grp, big, grp, big, jax.ShapeDtypeStruct((n_heads, 1, n), f32)],
        compiler_params=_params("parallel", "arbitrary"),
        name="peer_route",
    )(q, keys)


PEER_I_PER_TILE = 4


def _peer_dense_kernel(xm_ref, u_ref, vt_ref, s0_ref, e0_ref, s1_ref, e1_ref, th_ref, o_ref):
    e = pl.program_id(1)

    @pl.when(e == 0)
    def _():
        o_ref[...] = jnp.zeros(o_ref.shape, f32)

    n_heads = s1_ref.shape[0]
    sub = (e % (8 // PEER_I_PER_TILE)) * PEER_I_PER_TILE
    weights = []
    for il in range(PEER_I_PER_TILE):
        w = None
        for h in range(n_heads):
            s0 = s0_ref[h, 0, pl.ds(sub + il, 1), :]
            e0 = e0_ref[h, 0, pl.ds(sub + il, 1), :]
            keep = (s0 + s1_ref[h]) >= th_ref[h]
            term = jnp.where(keep, e0 * e1_ref[h], 0.0)
            w = term if w is None else w + term
        weights.append(w)
    w_t = jnp.concatenate(weights, axis=0)
    s_t = _dot_nt(u_ref[...], xm_ref[...])
    act = (_gelu(s_t) * w_t).astype(bf16)
    o_ref[...] += _dot(vt_ref[...], act)


def _peer_dense_call(xm, u_tab, vt_tab, layer, s0, e0, s1, e1, th, tm=512):
    n, d = xm.shape
    n_exp = u_tab.shape[1]
    te = PEER_I_PER_TILE * PEER_KEYS
    n_heads = s1.shape[0]
    per8 = 8 // PEER_I_PER_TILE
    row8 = pl.BlockSpec((n_heads, 1, 8, tm), lambda t, e: (0, e // per8, 0, t))
    full = pl.BlockSpec((n_heads, PEER_KEYS, tm), lambda t, e: (0, 0, t))
    return pl.pallas_call(
        _peer_dense_kernel,
        grid=(n // tm, n_exp // te),
        in_specs=[pl.BlockSpec((tm, d), lambda t, e: (t, 0)),
                  pl.BlockSpec((None, te, d), lambda t, e: (layer, e, 0)),
                  pl.BlockSpec((None, d, te), lambda t, e: (layer, 0, e)),
                  row8, row8, full, full,
                  pl.BlockSpec((n_heads, 1, tm), lambda t, e: (0, 0, t))],
        out_specs=pl.BlockSpec((d, tm), lambda t, e: (0, t)),
        out_shape=jax.ShapeDtypeStruct((d, n), f32),
        compiler_params=_params("parallel", "arbitrary", vmem=60 * 1024 * 1024),
        name="peer_dense",
    )(xm, u_tab, vt_tab, s0, e0, s1, e1, th)


def kernel(x, c, ctx, c_ctx, w_ada, b_ada, w_in, a_q_norm, a_k_norm, b_sink, lru_conv_w, lru_conv_b, lru_w_r, lru_b_r, lru_w_i, lru_b_i, lru_lambda, s5_a_re, s5_a_im, s5_log_step, s5_b_re, s5_b_im, s5_c_re, s5_c_im, s5_d, s5_w_glu, s5_b_glu, w_gate, b_gate, w_branch, w_out, ln_g, ln_b, peer_w_q, peer_sub_keys, peer_u, peer_v):
    n_batch, n_lat, d = x.shape
    n_ctx = ctx.shape[1]
    depth = w_ada.shape[0]
    seq_tot = n_ctx + n_lat
    n_tok = n_batch * seq_tot
    mix_w = d // 4
    assert n_ctx % ROW_TILE == 0 and n_lat % ROW_TILE == 0 and n_tok % MM_TILE_M == 0
    assert n_ctx == ROW_TILE, "scan kernels treat row tile 0 of every batch element as the context"
    tiles_per_batch = seq_tot // ROW_TILE
    geom = (tiles_per_batch, n_ctx // ROW_TILE, n_batch)
    alpha = (2.0 * depth) ** 0.25
    s5_steps = seq_tot // 4
    assert seq_tot % 4 == 0 and s5_steps % (8 * S5_CHUNK) == 0 and n_ctx % S5_CHUNK == 0

    cin = jnp.zeros((8, d), f32).at[:n_batch].set(c).at[n_batch].set(c_ctx)
    mod = _ada_call(cin, w_ada, b_ada)

    def mod_vec(l, k):
        return mod[l, :, k * d:(k + 1) * d].reshape(8, 1, d)

    cos, sin = _rope_tables(n_lat, n_ctx)
    w_in_b, w_gate_b, w_branch_b, w_out_b = (w.astype(bf16) for w in (w_in, w_gate, w_branch, w_out))
    w_glu_b, w_q_b, u_tab_b = (w.astype(bf16) for w in (s5_w_glu, peer_w_q, peer_u))
    vt_tab_b = jnp.swapaxes(peer_v.astype(bf16), 1, 2)
    xs = jnp.concatenate([ctx, x], axis=1).reshape(n_tok, d)
    um = _mod_call(xs, mod_vec(0, 1), mod_vec(0, 0), geom)

    qkv_cols = N_QKV_SLICES * HEAD_DIM
    for l in range(depth):
        z = _mm_call(um, w_in_b, l, f32, name="in_proj")
        qkv = _prep_call(z, cos, sin, a_q_norm[l], a_k_norm[l], n_batch, seq_tot)
        ya = _attn_a_call(qkv, n_ctx).reshape(n_tok, mix_w)
        yb = _attn_b_call(qkv, b_sink[l], n_ctx).reshape(n_tok, mix_w)

        lru_x_block = qkv_cols // mix_w
        a_coef, b_coef = _lru_gate_call(
            z, lru_x_block, lru_conv_w[l], lru_conv_b[l], lru_w_r[l].astype(bf16), lru_b_r[l],
            lru_w_i[l].astype(bf16), lru_b_i[l], jax.nn.softplus(-lru_lambda[l]), tiles_per_batch)
        hf, hr = _lru_scan_call(a_coef, b_coef, n_batch, tiles_per_batch)
        yr = _lru_out_call(hf, hr, z, lru_x_block + 1)

        s5_block0 = (qkv_cols + 2 * mix_w) // 128
        k_oct, w_end, m_read, lam_r = _s5_weights(s5_a_re[l], s5_a_im[l], s5_log_step[l], s5_b_re[l],
                                                  s5_b_im[l], s5_c_re[l], s5_c_im[l])
        e_parts = _s5_inject_call(z, s5_block0, w_end, s5_steps)
        h_parts = _s5_scan_call(e_parts, lam_r, n_batch, n_ctx // S5_CHUNK)
        y_s5 = _s5_read_call(z, s5_block0, h_parts, k_oct, m_read, s5_d[l], s5_steps)
        ys = _glu_call(y_s5, w_glu_b, l, s5_b_glu[l])

        merged = _merge_call(um, (ya, yb, yr, ys), w_gate_b, b_gate[l], w_branch_b, l)
        y = _mm_call(merged, w_out_b, l, f32, name="out_proj")
        xs, um = _ln_call(xs, y, mod_vec(l, 2), ln_g[l, 0], ln_b[l, 0], mod_vec(l, 4), mod_vec(l, 3), geom, alpha)

        q = _mm_call(um, w_q_b, l, bf16, name="peer_query")
        n_heads = peer_sub_keys.shape[1]
        keys = peer_sub_keys[l].astype(bf16).reshape(2 * n_heads, PEER_KEYS, -1)
        s0, s1, e0, e1, th = _peer_route_call(q, keys)
        f_t = _peer_dense_call(um, u_tab_b, vt_tab_b, l, s0, e0, s1, e1, th)
        nxt = min(l + 1, depth - 1)
        xs, um = _ln_call(xs, f_t.T, mod_vec(l, 5), ln_g[l, 1], ln_b[l, 1], mod_vec(nxt, 1), mod_vec(nxt, 0), geom, alpha)

    return xs.reshape(n_batch, seq_tot, d)[:, n_ctx:, :]
```

```python
import functools
import math

import jax
import jax.numpy as jnp
from jax import lax
from jax.experimental import pallas as pl
from jax.experimental.pallas import tpu as pltpu

f32 = jnp.float32
bf16 = jnp.bfloat16

HEAD_DIM = 128
GRID_W = 64
ROPE_THETA = 10000.0
WINDOW = 128
Q_TILE = 128
KV_GROUP = 4
LRU_C = 8.0
LRU_BLOCKS = 8
S5_GROUP = 16
S5_STATE = 64
S5_CHUNK = 8
S5_OCT = 8
PEER_TOPK = 16
PEER_KEYS = 128
LN_EPS = 1e-6
NEG = -1e30
ROW_TILE = 256
MM_TILE_M = 512
V7X_VMEM_LIMIT = 56 * 1024 * 1024


def _params(*sem, vmem=V7X_VMEM_LIMIT):
    return pltpu.CompilerParams(dimension_semantics=sem, vmem_limit_bytes=vmem)


def _dot(a, b):
    return jnp.dot(a, b, preferred_element_type=f32)


def _dot_nt(a, b):
    return lax.dot_general(a, b, (((1,), (1,)), ((), ())), preferred_element_type=f32)


def _gelu(x):
    return x * (0.5 * (1.0 + jnp.tanh(math.sqrt(2.0 / math.pi) * (x + 0.044715 * (x * x * x)))))


def _sigmoid(x):
    return 1.0 / (1.0 + jnp.exp(-x))


def _ada_kernel(c_ref, w_ref, b_ref, o_ref):
    c = c_ref[...]
    a = (c * _sigmoid(c)).astype(bf16)
    o_ref[0] = _dot(a, w_ref[0].astype(bf16)) + b_ref[0]


def _ada_call(cin, w_ada, b_ada, tn=512):
    depth, d, n = w_ada.shape
    return pl.pallas_call(
        _ada_kernel,
        grid=(depth, n // tn),
        in_specs=[pl.BlockSpec((8, d), lambda l, j: (0, 0)),
                  pl.BlockSpec((1, d, tn), lambda l, j: (l, 0, j)),
                  pl.BlockSpec((1, 1, tn), lambda l, j: (l, 0, j))],
        out_specs=pl.BlockSpec((1, 8, tn), lambda l, j: (l, 0, j)),
        out_shape=jax.ShapeDtypeStruct((depth, 8, n), f32),
        compiler_params=_params("parallel", "parallel"),
        name="adaln",
    )(cin, w_ada, b_ada.reshape(depth, 1, n))


def _row_group(i, tiles_per_batch, ctx_tiles, n_batch):
    return jnp.where(i % tiles_per_batch < ctx_tiles, n_batch, i // tiles_per_batch)


def _mod_kernel(x_ref, sc_ref, sh_ref, o_ref):
    o_ref[...] = (x_ref[...] * (1.0 + sc_ref[0]) + sh_ref[0]).astype(o_ref.dtype)


def _mod_call(x, sc, sh, geom):
    n, d = x.shape
    tpb, ct, nb = geom
    sel = lambda i: (_row_group(i, tpb, ct, nb), 0, 0)
    return pl.pallas_call(
        _mod_kernel,
        grid=(n // ROW_TILE,),
        in_specs=[pl.BlockSpec((ROW_TILE, d), lambda i: (i, 0)),
                  pl.BlockSpec((1, 1, d), sel), pl.BlockSpec((1, 1, d), sel)],
        out_specs=pl.BlockSpec((ROW_TILE, d), lambda i: (i, 0)),
        out_shape=jax.ShapeDtypeStruct((n, d), bf16),
        compiler_params=_params("parallel"),
        name="modulate",
    )(x, sc, sh)


def _ln_kernel(x_ref, y_ref, g_ref, lg_ref, lb_ref, sc_ref, sh_ref, xo_ref, xm_ref, *, alpha):
    v = alpha * x_ref[...] + g_ref[0] * y_ref[...]
    mu = jnp.mean(v, axis=-1, keepdims=True)
    vc = v - mu
    var = jnp.mean(vc * vc, axis=-1, keepdims=True)
    o = vc * lax.rsqrt(var + LN_EPS) * lg_ref[...] + lb_ref[...]
    xo_ref[...] = o
    xm_ref[...] = (o * (1.0 + sc_ref[0]) + sh_ref[0]).astype(xm_ref.dtype)


def _ln_call(x, y, gate, ln_g, ln_b, sc, sh, geom, alpha):
    n, d = x.shape
    tpb, ct, nb = geom
    sel = lambda i: (_row_group(i, tpb, ct, nb), 0, 0)
    row = pl.BlockSpec((ROW_TILE, d), lambda i: (i, 0))
    vec = pl.BlockSpec((1, d), lambda i: (0, 0))
    return pl.pallas_call(
        functools.partial(_ln_kernel, alpha=alpha),
        grid=(n // ROW_TILE,),
        in_specs=[row, row, pl.BlockSpec((1, 1, d), sel), vec, vec,
                  pl.BlockSpec((1, 1, d), sel), pl.BlockSpec((1, 1, d), sel)],
        out_specs=[row, row],
        out_shape=[jax.ShapeDtypeStruct((n, d), f32), jax.ShapeDtypeStruct((n, d), bf16)],
        compiler_params=_params("parallel"),
        name="residual_ln",
    )(x, y, gate, ln_g.reshape(1, d), ln_b.reshape(1, d), sc, sh)


def _mm_kernel(a_ref, b_ref, o_ref):
    o_ref[...] = _dot(a_ref[...], b_ref[...]).astype(o_ref.dtype)


def _mm_call(a, b, layer, out_dtype, tn=512, tm=MM_TILE_M, name="matmul"):
    m, k = a.shape
    n = b.shape[2]
    return pl.pallas_call(
        _mm_kernel,
        grid=(m // tm, n // tn),
        in_specs=[pl.BlockSpec((tm, k), lambda i, j: (i, 0)),
                  pl.BlockSpec((None, k, tn), lambda i, j: (layer, 0, j))],
        out_specs=pl.BlockSpec((tm, tn), lambda i, j: (i, j)),
        out_shape=jax.ShapeDtypeStruct((m, n), out_dtype),
        compiler_params=_params("parallel", "arbitrary"),
        name=name,
    )(a, b)


def _glu_kernel(a_ref, wv_ref, wg_ref, bv_ref, bg_ref, o_ref):
    a = a_ref[...].astype(bf16)
    val = _dot(a, wv_ref[...]) + bv_ref[...]
    gate = _dot(a, wg_ref[...]) + bg_ref[...]
    o_ref[...] = (val * _sigmoid(gate)).astype(o_ref.dtype)


def _glu_call(a, w, layer, b, tn=512, tm=MM_TILE_M):
    m, k = a.shape
    n = w.shape[2] // 2
    nj = n // tn
    b2 = b.reshape(1, 2 * n)
    return pl.pallas_call(
        _glu_kernel,
        grid=(m // tm, nj),
        in_specs=[pl.BlockSpec((tm, k), lambda i, j: (i, 0)),
                  pl.BlockSpec((None, k, tn), lambda i, j: (layer, 0, j)),
                  pl.BlockSpec((None, k, tn), lambda i, j: (layer, 0, j + nj)),
                  pl.BlockSpec((1, tn), lambda i, j: (0, j)),
                  pl.BlockSpec((1, tn), lambda i, j: (0, j + nj))],
        out_specs=pl.BlockSpec((tm, tn), lambda i, j: (i, j)),
        out_shape=jax.ShapeDtypeStruct((m, n), bf16),
        compiler_params=_params("parallel", "arbitrary"),
        name="s5_glu",
    )(a, w, w, b2, b2)


def _merge_kernel(u_ref, ya_ref, yb_ref, yr_ref, ys_ref, wg_ref, bg_ref, wb_ref, o_ref):
    u = u_ref[...]
    acc = None
    for i, y_ref in enumerate((ya_ref, yb_ref, yr_ref, ys_ref)):
        gate = _sigmoid(_dot(u, wg_ref[i]) + bg_ref[i])
        term = gate * _dot(y_ref[...], wb_ref[i])
        acc = term if acc is None else acc + term
    o_ref[...] = acc.astype(o_ref.dtype)


def _merge_call(u, branches, w_gate, b_gate, w_branch, layer, tn=256, tm=MM_TILE_M):
    m, d = u.shape
    _, nbr, bw, n = w_branch.shape
    ybr = pl.BlockSpec((tm, bw), lambda i, j: (i, 0))
    return pl.pallas_call(
        _merge_kernel,
        grid=(m // tm, n // tn),
        in_specs=[pl.BlockSpec((tm, d), lambda i, j: (i, 0)), ybr, ybr, ybr, ybr,
                  pl.BlockSpec((None, nbr, d, tn), lambda i, j: (layer, 0, 0, j)),
                  pl.BlockSpec((nbr, 1, tn), lambda i, j: (0, 0, j)),
                  pl.BlockSpec((None, nbr, bw, tn), lambda i, j: (layer, 0, 0, j))],
        out_specs=pl.BlockSpec((tm, tn), lambda i, j: (i, j)),
        out_shape=jax.ShapeDtypeStruct((m, n), bf16),
        compiler_params=_params("parallel", "arbitrary"),
        name="branch_merge",
    )(u, *branches, w_gate, b_gate.reshape(nbr, 1, n), w_branch)


def _rope_tables(n_lat, n_ctx):
    t = jnp.arange(n_lat)
    row = (t // GRID_W).astype(f32)
    col = (t % GRID_W).astype(f32)
    n_freq = HEAD_DIM // 4
    inv = ROPE_THETA ** (-jnp.arange(n_freq, dtype=f32) / n_freq)
    ang = jnp.concatenate([row[:, None] * inv, col[:, None] * inv], axis=-1)
    cos = jnp.repeat(jnp.cos(ang), 2, axis=-1)
    sin = jnp.repeat(jnp.sin(ang), 2, axis=-1) * jnp.tile(jnp.array([-1.0, 1.0], f32), HEAD_DIM // 2)
    cos = jnp.concatenate([jnp.ones((n_ctx, HEAD_DIM), f32), cos], axis=0)
    sin = jnp.concatenate([jnp.zeros((n_ctx, HEAD_DIM), f32), sin], axis=0)
    return cos, sin


N_QKV_SLICES = 24
A_Q, A_K, A_V, B_Q, B_K, B_V = 0, 8, 10, 12, 20, 22


def _prep_kernel(z_ref, cos_ref, sin_ref, gq_ref, gk_ref, o_ref):
    cos, sin = cos_ref[...], sin_ref[...]
    even = lax.broadcasted_iota(jnp.int32, cos.shape, 1) % 2 == 0
    for s in range(N_QKV_SLICES):
        y = z_ref[:, s * HEAD_DIM:(s + 1) * HEAD_DIM]
        if s < A_V:
            gain = gq_ref[...] if s < A_K else gk_ref[...]
            y = y * lax.rsqrt(jnp.mean(y * y, axis=-1, keepdims=True) + LN_EPS) * gain
        if s < A_V or B_Q <= s < B_V:
            partner = jnp.where(even, pltpu.roll(y, HEAD_DIM - 1, axis=1), pltpu.roll(y, 1, axis=1))
            y = y * cos + partner * sin
        if s < A_K or B_Q <= s < B_K:
            y = y * (HEAD_DIM ** -0.5)
        o_ref[0, s] = y.astype(o_ref.dtype)


def _prep_call(z, cos, sin, gain_q, gain_k, n_batch, seq_tot):
    n = z.shape[0]
    tr = ROW_TILE
    tpb = seq_tot // tr
    width = N_QKV_SLICES * HEAD_DIM
    vec = pl.BlockSpec((1, HEAD_DIM), lambda i: (0, 0))
    return pl.pallas_call(
        _prep_kernel,
        grid=(n // tr,),
        in_specs=[pl.BlockSpec((tr, width), lambda i: (i, 0)),
                  pl.BlockSpec((tr, HEAD_DIM), lambda i: (i % tpb, 0)),
                  pl.BlockSpec((tr, HEAD_DIM), lambda i: (i % tpb, 0)), vec, vec],
        out_specs=pl.BlockSpec((1, N_QKV_SLICES, tr, HEAD_DIM), lambda i: (i // tpb, 0, i % tpb, 0)),
        out_shape=jax.ShapeDtypeStruct((n_batch, N_QKV_SLICES, seq_tot, HEAD_DIM), bf16),
        compiler_params=_params("parallel"),
        name="qkv_prep",
    )(z, cos, sin, gain_q.reshape(1, HEAD_DIM), gain_k.reshape(1, HEAD_DIM))


def _attn_a_kernel(q_ref, k_ref, v_ref, o_ref, m_s, acc_s, *, n_ctx, tk, n_lat_chunks):
    q = q_ref[0].reshape(KV_GROUP * Q_TILE, HEAD_DIM)

    def values(lo, n):
        return jnp.concatenate([v_ref[0, 0, lo:lo + n, :], jnp.ones((n, HEAD_DIM), bf16)], axis=1)

    s = _dot_nt(q, k_ref[0, 0, 0:n_ctx, :])
    m0 = jnp.max(s, axis=-1, keepdims=True)
    m_s[...] = m0
    acc_s[...] = _dot(jnp.exp((s - m0).astype(bf16)), values(0, n_ctx))

    @pl.when(pl.program_id(2) >= n_ctx // Q_TILE)
    def _():
        for c in range(n_lat_chunks):
            lo = n_ctx + c * tk
            s = _dot_nt(q, k_ref[0, 0, lo:lo + tk, :])
            m_prev = m_s[...]
            m_new = jnp.maximum(m_prev, jnp.max(s, axis=-1, keepdims=True))
            p = jnp.exp((s - m_new).astype(bf16))
            acc_s[...] = jnp.exp(m_prev - m_new) * acc_s[...] + _dot(p, values(lo, tk))
            m_s[...] = m_new

    out = acc_s[:, 0:HEAD_DIM] / acc_s[:, HEAD_DIM:2 * HEAD_DIM]
    for g in range(KV_GROUP):
        o_ref[0, :, g * HEAD_DIM:(g + 1) * HEAD_DIM] = out[g * Q_TILE:(g + 1) * Q_TILE].astype(o_ref.dtype)


def _attn_a_call(qkv, n_ctx, tk=1024):
    nb, _, seq_tot, _ = qkv.shape
    n_kv = (A_V - A_K)
    rows = KV_GROUP * Q_TILE
    n_lat = seq_tot - n_ctx
    tk = min(tk, n_lat)
    assert n_lat % tk == 0
    kern = functools.partial(_attn_a_kernel, n_ctx=n_ctx, tk=tk, n_lat_chunks=n_lat // tk)
    return pl.pallas_call(
        kern,
        grid=(nb, n_kv, seq_tot // Q_TILE),
        in_specs=[pl.BlockSpec((1, KV_GROUP, Q_TILE, HEAD_DIM), lambda b, h, i: (b, A_Q // KV_GROUP + h, i, 0)),
                  pl.BlockSpec((1, 1, seq_tot, HEAD_DIM), lambda b, h, i: (b, A_K + h, 0, 0)),
                  pl.BlockSpec((1, 1, seq_tot, HEAD_DIM), lambda b, h, i: (b, A_V + h, 0, 0))],
        out_specs=pl.BlockSpec((1, Q_TILE, KV_GROUP * HEAD_DIM), lambda b, h, i: (b, i, h)),
        out_shape=jax.ShapeDtypeStruct((nb, seq_tot, n_kv * KV_GROUP * HEAD_DIM), bf16),
        scratch_shapes=[pltpu.VMEM((rows, 1), f32), pltpu.VMEM((rows, 2 * HEAD_DIM), f32)],
        compiler_params=_params("parallel", "parallel", "arbitrary"),
        name="dense_attention",
    )(qkv, qkv, qkv)


def _attn_b_kernel(q_ref, k_ref, v_ref, sink_ref, o_ref, *, n_ctx, n_blk):
    i = pl.program_id(2)
    blk = i - n_ctx // Q_TILE
    rows = KV_GROUP * Q_TILE
    q = q_ref[0].reshape(rows, HEAD_DIM)
    sink = sink_ref[0]
    kc = k_ref[0, 0, 0:n_ctx, :]
    vc = v_ref[0, 0, 0:n_ctx, :]
    s_ctx = _dot_nt(q, kc)
    m = jnp.maximum(jnp.max(s_ctx, axis=-1, keepdims=True), sink)
    q_pos = lax.broadcasted_iota(jnp.int32, (rows, Q_TILE), 0) % Q_TILE
    k_off = lax.broadcasted_iota(jnp.int32, (rows, Q_TILE), 1)
    s_loc, v_loc = [], []
    for d in (-1, 0, 1):
        j = blk + d
        valid = jnp.logical_and(blk >= 0, jnp.logical_and(j >= 0, j < n_blk))
        start = pl.multiple_of(n_ctx + jnp.clip(j, 0, n_blk - 1) * Q_TILE, Q_TILE)
        k = k_ref[0, 0, pl.ds(start, Q_TILE), :]
        v_loc.append(v_ref[0, 0, pl.ds(start, Q_TILE), :])
        keep = jnp.logical_and(jnp.abs(k_off + d * Q_TILE - q_pos) <= WINDOW, valid)
        s = jnp.where(keep, _dot_nt(q, k), NEG)
        m = jnp.maximum(m, jnp.max(s, axis=-1, keepdims=True))
        s_loc.append(s)
    p = jnp.exp(s_ctx - m)
    den = jnp.exp(sink - m) + jnp.sum(p, axis=-1, keepdims=True)
    acc = _dot(p.astype(bf16), vc)
    for s, v in zip(s_loc, v_loc):
        p = jnp.exp(s - m)
        den = den + jnp.sum(p, axis=-1, keepdims=True)
        acc = acc + _dot(p.astype(bf16), v)
    out = acc / den
    for g in range(KV_GROUP):
        o_ref[0, :, g * HEAD_DIM:(g + 1) * HEAD_DIM] = out[g * Q_TILE:(g + 1) * Q_TILE].astype(o_ref.dtype)


def _attn_b_call(qkv, sink, n_ctx):
    nb, _, seq_tot, _ = qkv.shape
    n_kv = (B_V - B_K)
    rows = KV_GROUP * Q_TILE
    sink_rows = jnp.repeat(sink.astype(f32).reshape(n_kv, KV_GROUP), Q_TILE, axis=1).reshape(n_kv, rows, 1)
    kern = functools.partial(_attn_b_kernel, n_ctx=n_ctx, n_blk=(seq_tot - n_ctx) // Q_TILE)
    return pl.pallas_call(
        kern,
        grid=(nb, n_kv, seq_tot // Q_TILE),
        in_specs=[pl.BlockSpec((1, KV_GROUP, Q_TILE, HEAD_DIM), lambda b, h, i: (b, B_Q // KV_GROUP + h, i, 0)),
                  pl.BlockSpec((1, 1, seq_tot, HEAD_DIM), lambda b, h, i: (b, B_K + h, 0, 0)),
                  pl.BlockSpec((1, 1, seq_tot, HEAD_DIM), lambda b, h, i: (b, B_V + h, 0, 0)),
                  pl.BlockSpec((1, rows, 1), lambda b, h, i: (h, 0, 0))],
        out_specs=pl.BlockSpec((1, Q_TILE, KV_GROUP * HEAD_DIM), lambda b, h, i: (b, i, h)),
        out_shape=jax.ShapeDtypeStruct((nb, seq_tot, n_kv * KV_GROUP * HEAD_DIM), bf16),
        compiler_params=_params("parallel", "parallel", "arbitrary"),
        name="window_attention",
    )(qkv, qkv, qkv, sink_rows)


HALO = 8


def _lru_gate_kernel(x_ref, xp_ref, xn_ref, cw_ref, cb_ref, wr_ref, br_ref, wi_ref, bi_ref, sp_ref,
                     a_ref, b_ref, pad_s, *, tiles_per_batch):
    tr = x_ref.shape[0]
    t = pl.program_id(0) % tiles_per_batch
    has_prev = t >= 2
    has_next = jnp.logical_and(t >= 1, t < tiles_per_batch - 1)
    x = x_ref[...]
    pad_s[0:HALO, :] = jnp.where(has_prev, xp_ref[...], 0.0)
    pad_s[HALO:HALO + tr, :] = x
    pad_s[HALO + tr:2 * HALO + tr, :] = jnp.where(has_next, xn_ref[...], 0.0)
    cw = cw_ref[...]
    xl = cb_ref[...] + pad_s[HALO - 2:HALO - 2 + tr, :] * cw[0:1]
    xl = xl + pad_s[HALO - 1:HALO - 1 + tr, :] * cw[1:2]
    xl = xl + x * cw[2:3]
    xl = xl + pad_s[HALO + 1:HALO + 1 + tr, :] * cw[3:4]
    bw = xl.shape[1] // LRU_BLOCKS
    for n in range(LRU_BLOCKS):
        cols = slice(n * bw, (n + 1) * bw)
        xb = xl[:, cols]
        xb16 = xb.astype(bf16)
        for d in range(2):
            r = _sigmoid(_dot(xb16, wr_ref[d, n]) + br_ref[d:d + 1, cols])
            gi = _sigmoid(_dot(xb16, wi_ref[d, n]) + bi_ref[d:d + 1, cols])
            log_a = (-LRU_C) * r * sp_ref[d:d + 1, cols]
            a = jnp.exp(log_a)
            a_ref[d, :, cols] = a
            b_ref[d, :, cols] = jnp.sqrt(1.0 - a * a) * gi * xb


def _lru_gate_call(z, col_block, conv_w, conv_b, w_r, b_r, w_i, b_i, softplus_lam, tiles_per_batch):
    n = z.shape[0]
    w = conv_w.shape[1]
    tr = ROW_TILE
    hb = tr // HALO
    n_halo = n // HALO
    full = lambda shape: pl.BlockSpec(shape, lambda i: (0,) * len(shape))
    kern = functools.partial(_lru_gate_kernel, tiles_per_batch=tiles_per_batch)
    out = jax.ShapeDtypeStruct((2, n, w), f32)
    return pl.pallas_call(
        kern,
        grid=(n // tr,),
        in_specs=[pl.BlockSpec((tr, w), lambda i: (i, col_block)),
                  pl.BlockSpec((HALO, w), lambda i: (jnp.maximum(i * hb - 1, 0), col_block)),
                  pl.BlockSpec((HALO, w), lambda i: (jnp.minimum((i + 1) * hb, n_halo - 1), col_block)),
                  full(conv_w.shape), full((1, w)), full(w_r.shape), full(b_r.shape),
                  full(w_i.shape), full(b_i.shape), full(softplus_lam.shape)],
        out_specs=[pl.BlockSpec((2, tr, w), lambda i: (0, i, 0)), pl.BlockSpec((2, tr, w), lambda i: (0, i, 0))],
        out_shape=[out, out],
        scratch_shapes=[pltpu.VMEM((tr + 2 * HALO, w), f32)],
        compiler_params=_params("parallel"),
        name="lru_gates",
    )(z, z, z, conv_w, conv_b.reshape(1, w), w_r, b_r, w_i, b_i, softplus_lam)


def _lru_scan_kernel(af_ref, bf_ref, ar_ref, br_ref, hf_ref, hr_ref, sf, sr):
    tr = hf_ref.shape[0]

    @pl.when(pl.program_id(1) == 0)
    def _():
        sf[...] = jnp.zeros(sf.shape, f32)
        sr[...] = jnp.zeros(sr.shape, f32)

    def body(t, carry):
        hf, hr = carry
        hf = af_ref[0, pl.ds(t, 1), :] * hf + bf_ref[0, pl.ds(t, 1), :]
        hf_ref[pl.ds(t, 1), :] = hf
        u = tr - 1 - t
        hr = ar_ref[0, pl.ds(u, 1), :] * hr + br_ref[0, pl.ds(u, 1), :]
        hr_ref[pl.ds(u, 1), :] = hr
        return hf, hr

    hf, hr = lax.fori_loop(0, tr, body, (sf[...], sr[...]), unroll=8)
    sf[...] = hf
    sr[...] = hr


def _scan_tile_maps(tiles_per_batch):
    fwd = lambda b, i: b * tiles_per_batch + i
    rev = lambda b, i: b * tiles_per_batch + jnp.where(i == 0, 0, tiles_per_batch - i)
    return fwd, rev


def _lru_scan_call(a, b, n_batch, tiles_per_batch):
    _, n, w = a.shape
    tr = ROW_TILE
    fwd, rev = _scan_tile_maps(tiles_per_batch)
    out = jax.ShapeDtypeStruct((n, w), f32)
    return pl.pallas_call(
        _lru_scan_kernel,
        grid=(n_batch, tiles_per_batch),
        in_specs=[pl.BlockSpec((1, tr, w), lambda bb, i: (0, fwd(bb, i), 0)),
                  pl.BlockSpec((1, tr, w), lambda bb, i: (0, fwd(bb, i), 0)),
                  pl.BlockSpec((1, tr, w), lambda bb, i: (1, rev(bb, i), 0)),
                  pl.BlockSpec((1, tr, w), lambda bb, i: (1, rev(bb, i), 0))],
        out_specs=[pl.BlockSpec((tr, w), lambda bb, i: (fwd(bb, i), 0)),
                   pl.BlockSpec((tr, w), lambda bb, i: (rev(bb, i), 0))],
        out_shape=[out, out],
        scratch_shapes=[pltpu.VMEM((1, w), f32), pltpu.VMEM((1, w), f32)],
        compiler_params=_params("parallel", "arbitrary"),
        name="lru_scan",
    )(a, b, a, b)


def _lru_out_kernel(hf_ref, hr_ref, g_ref, o_ref):
    o_ref[...] = ((hf_ref[...] + hr_ref[...]) * _gelu(g_ref[...])).astype(o_ref.dtype)


def _lru_out_call(hf, hr, z, gate_col_block):
    n, w = hf.shape
    tr = ROW_TILE
    row = pl.BlockSpec((tr, w), lambda i: (i, 0))
    return pl.pallas_call(
        _lru_out_kernel,
        grid=(n // tr,),
        in_specs=[row, row, pl.BlockSpec((tr, w), lambda i: (i, gate_col_block))],
        out_specs=row,
        out_shape=jax.ShapeDtypeStruct((n, w), bf16),
        compiler_params=_params("parallel"),
        name="lru_out",
    )(hf, hr, z)


def _s5_weights(a_re, a_im, log_step, b_re, b_im, c_re, c_im):
    r = S5_CHUNK
    n_g = a_re.shape[1]
    n_oct = n_g // S5_OCT
    step = jnp.exp(log_step)[..., None]
    mag = jnp.exp(a_re * step)
    lb_re, lb_im = mag * jnp.cos(a_im * step), mag * jnp.sin(a_im * step)
    den = a_re * a_re + a_im * a_im
    num_re = lb_re - 1.0
    coef_re = (num_re * a_re + lb_im * a_im) / den
    coef_im = (lb_im * a_re - num_re * a_im) / den
    bb_re = coef_re[..., None] * b_re - coef_im[..., None] * b_im
    bb_im = coef_re[..., None] * b_im + coef_im[..., None] * b_re
    tau = jnp.arange(r + 1, dtype=f32)[:, None, None, None]
    pmag = jnp.exp(tau * (a_re * step))
    pw_re, pw_im = pmag * jnp.cos(tau * (a_im * step)), pmag * jnp.sin(tau * (a_im * step))

    lb_b_re = pw_re[..., None] * bb_re - pw_im[..., None] * bb_im
    lb_b_im = pw_re[..., None] * bb_im + pw_im[..., None] * bb_re
    lag = (jnp.einsum('tdgpc,dgop->tdgco', lb_b_re, c_re) - jnp.einsum('tdgpc,dgop->tdgco', lb_b_im, c_im))
    idx = jnp.arange(r)
    sh_f = (idx[None, None, :] - idx[None, :, None] == jnp.arange(r + 1)[:, None, None]).astype(f32)
    sh_r = (idx[None, :, None] - idx[None, None, :] == jnp.arange(r + 1)[:, None, None]).astype(f32)
    k_loc = jnp.einsum('tio,tgcd->gicod', sh_f, lag[:, 0]) + jnp.einsum('tio,tgcd->gicod', sh_r, lag[:, 1])
    eye = jnp.eye(S5_OCT, dtype=f32)
    k_loc = k_loc.reshape(n_oct, S5_OCT, r, S5_GROUP, r, S5_GROUP)
    k_oct = jnp.einsum('kgicod,gh->kigcohd', k_loc, eye).reshape(n_oct, r * 128, r * 128)

    def inject(d, powers):
        w_re = pw_re[powers, d][..., None] * bb_re[d] - pw_im[powers, d][..., None] * bb_im[d]
        w_im = pw_re[powers, d][..., None] * bb_im[d] + pw_im[powers, d][..., None] * bb_re[d]
        out = []
        for w in (w_re, w_im):
            w = w.reshape(r, n_oct, S5_OCT, S5_STATE, S5_GROUP)
            out.append(jnp.einsum('ikgpc,gh->kigchp', w, eye).reshape(n_oct, r * 128, S5_OCT * S5_STATE))
        return out
    w_end = jnp.concatenate(inject(0, idx[::-1]) + inject(1, idx), axis=-1)

    def readout(d, powers):
        cl_re = c_re[d][None] * pw_re[powers, d][:, :, None, :] - c_im[d][None] * pw_im[powers, d][:, :, None, :]
        cl_im = c_re[d][None] * pw_im[powers, d][:, :, None, :] + c_im[d][None] * pw_re[powers, d][:, :, None, :]
        out = []
        for w in (cl_re, -cl_im):
            w = w.reshape(r, n_oct, S5_OCT, S5_GROUP, S5_STATE)
            out.append(jnp.einsum('okgcp,gh->kgpohc', w, eye).reshape(n_oct, S5_OCT * S5_STATE, r * 128))
        return out
    m_read = jnp.concatenate(readout(0, idx + 1) + readout(1, r - idx), axis=1)

    lam_r = jnp.stack([pw_re[r, 0].reshape(-1), pw_im[r, 0].reshape(-1),
                       pw_re[r, 1].reshape(-1), pw_im[r, 1].reshape(-1)])
    return k_oct.astype(bf16), w_end.astype(bf16), m_read.astype(bf16), lam_r


def _chunk_rows(u_ref):
    n_rows = u_ref.shape[0] // S5_CHUNK
    return [u_ref[pl.ds(i, n_rows, stride=S5_CHUNK), :] for i in range(S5_CHUNK)]


def _s5_inject_kernel(u_ref, w_ref, efr_ref, efi_ref, err_ref, eri_ref):
    x = jnp.concatenate(_chunk_rows(u_ref), axis=1).astype(bf16)
    e = _dot(x, w_ref[0])
    w = efr_ref.shape[1]
    for k, o_ref in enumerate((efr_ref, efi_ref, err_ref, eri_ref)):
        o_ref[...] = e[:, k * w:(k + 1) * w]


def _s5_inject_call(z, col_block0, w_end, steps):
    n = z.shape[0]
    n_oct = w_end.shape[0]
    sw = S5_OCT * S5_STATE
    rows = steps // S5_CHUNK
    out = jax.ShapeDtypeStruct((n // S5_CHUNK, n_oct * sw), f32)
    ospec = pl.BlockSpec((rows, sw), lambda i, k: (i, k))
    return pl.pallas_call(
        _s5_inject_kernel,
        grid=(n // steps, n_oct),
        in_specs=[pl.BlockSpec((steps, 128), lambda i, k: (i, col_block0 + k)),
                  pl.BlockSpec((1,) + w_end.shape[1:], lambda i, k: (k, 0, 0))],
        out_specs=[ospec] * 4,
        out_shape=[out] * 4,
        compiler_params=_params("parallel", "arbitrary"),
        name="s5_inject",
    )(z, w_end)


def _s5_scan_kernel(efr_ref, efi_ref, err_ref, eri_ref, lam_ref, hfr_ref, hfi_ref, hrr_ref, hri_ref, *, n_ctx):
    n_rows = efr_ref.shape[0]
    lfr, lfi, lrr, lri = (lam_ref[k:k + 1, :] for k in range(4))
    zero = jnp.zeros((1, efr_ref.shape[1]), f32)

    def body(t, carry):
        fr, fi, rr, ri = carry
        hfr_ref[pl.ds(t, 1), :] = fr
        hfi_ref[pl.ds(t, 1), :] = fi
        er, ei = efr_ref[pl.ds(t, 1), :], efi_ref[pl.ds(t, 1), :]
        fr, fi = lfr * fr - lfi * fi + er, lfr * fi + lfi * fr + ei
        u = jnp.where(t < n_ctx, n_ctx - 1 - t, n_rows - 1 - (t - n_ctx))
        hrr_ref[pl.ds(u, 1), :] = rr
        hri_ref[pl.ds(u, 1), :] = ri
        er, ei = err_ref[pl.ds(u, 1), :], eri_ref[pl.ds(u, 1), :]
        rr, ri = lrr * rr - lri * ri + er, lrr * ri + lri * rr + ei
        return fr, fi, rr, ri

    lax.fori_loop(0, n_rows, body, (zero, zero, zero, zero), unroll=4)


def _s5_scan_call(e_parts, lam_r, n_batch, n_ctx_rows, lane_block=512):
    n_rows_tot, width = e_parts[0].shape
    rows = n_rows_tot // n_batch
    blk = pl.BlockSpec((rows, lane_block), lambda b, j: (b, j))
    out = jax.ShapeDtypeStruct((n_rows_tot, width), f32)
    return pl.pallas_call(
        functools.partial(_s5_scan_kernel, n_ctx=n_ctx_rows),
        grid=(n_batch, width // lane_block),
        in_specs=[blk] * 4 + [pl.BlockSpec((4, lane_block), lambda b, j: (0, j))],
        out_specs=[blk] * 4,
        out_shape=[out] * 4,
        compiler_params=_params("parallel", "parallel"),
        name="s5_scan",
    )(*e_parts, lam_r)


def _s5_read_kernel(u_ref, hfr_ref, hfi_ref, hrr_ref, hri_ref, k_ref, m_ref, d_ref, o_ref):
    parts = _chunk_rows(u_ref)
    x = jnp.concatenate(parts, axis=1).astype(bf16)
    h = jnp.concatenate([hfr_ref[...], hfi_ref[...], hrr_ref[...], hri_ref[...]], axis=1).astype(bf16)
    y = _dot(x, k_ref[0]) + _dot(h, m_ref[0])
    n_rows = u_ref.shape[0] // S5_CHUNK
    for i in range(S5_CHUNK):
        o_ref[pl.ds(i, n_rows, stride=S5_CHUNK), :] = y[:, i * 128:(i + 1) * 128] + parts[i] * d_ref[...]


def _s5_read_call(z, col_block0, h_parts, k_oct, m_read, d_skip, steps):
    n = z.shape[0]
    n_oct = k_oct.shape[0]
    sw = S5_OCT * S5_STATE
    rows = steps // S5_CHUNK
    hspec = pl.BlockSpec((rows, sw), lambda i, k: (i, k))
    return pl.pallas_call(
        _s5_read_kernel,
        grid=(n // steps, n_oct),
        in_specs=[pl.BlockSpec((steps, 128), lambda i, k: (i, col_block0 + k))] + [hspec] * 4 +
                 [pl.BlockSpec((1,) + k_oct.shape[1:], lambda i, k: (k, 0, 0)),
                  pl.BlockSpec((1,) + m_read.shape[1:], lambda i, k: (k, 0, 0)),
                  pl.BlockSpec((1, 128), lambda i, k: (0, k))],
        out_specs=pl.BlockSpec((steps, 128), lambda i, k: (i, k)),
        out_shape=jax.ShapeDtypeStruct((n, n_oct * 128), f32),
        compiler_params=_params("parallel", "arbitrary"),
        name="s5_readout",
    )(z, *h_parts, k_oct, m_read, d_skip.reshape(1, -1))


def _top_values(s, k):
    vals = []
    for _ in range(k):
        m = jnp.max(s, axis=0, keepdims=True)
        vals.append(m)
        s = jnp.where(s == m, -jnp.inf, s)
    return vals


def _peer_route_kernel(q_ref, keys_ref, s0_ref, s1_ref, e0_ref, e1_ref, th_ref):
    s0 = _dot_nt(keys_ref[0], q_ref[:, 0:PEER_KEYS])
    s1 = _dot_nt(keys_ref[1], q_ref[:, PEER_KEYS:2 * PEER_KEYS])
    top0 = _top_values(s0, PEER_TOPK)
    top1 = jnp.concatenate(_top_values(s1, PEER_TOPK), axis=0)
    cand = jnp.concatenate([t + top1 for t in top0], axis=0)
    best = _top_values(cand, PEER_TOPK)
    z = None
    for v in best:
        e = jnp.exp(v - best[0])
        z = e if z is None else z + e
    s0_ref[0] = s0.reshape(s0_ref.shape[1:])
    s1_ref[0] = s1
    e0_ref[0] = (jnp.exp(s0 - top0[0]) / z).reshape(e0_ref.shape[1:])
    e1_ref[0] = jnp.exp(s1 - top1[0:1])
    th_ref[0] = best[-1]


def _peer_route_call(q, keys, tt=256):
    n = q.shape[0]
    n_heads = keys.shape[0] // 2
    big = jax.ShapeDtypeStruct((n_heads, PEER_KEYS, n), f32)
    bspec = pl.BlockSpec((1, PEER_KEYS, tt), lambda t, h: (h, 0, t))
    grp = jax.ShapeDtypeStruct((n_heads, PEER_KEYS // 8, 8, n), f32)
    gspec = pl.BlockSpec((1, PEER_KEYS // 8, 8, tt), lambda t, h: (h, 0, 0, t))
    return pl.pallas_call(
        _peer_route_kernel,
        grid=(n // tt, n_heads),
        in_specs=[pl.BlockSpec((tt, 2 * PEER_KEYS), lambda t, h: (t, h)),
                  pl.BlockSpec((2, PEER_KEYS, PEER_KEYS), lambda t, h: (h, 0, 0))],
        out_specs=[gspec, bspec, gspec, bspec, pl.BlockSpec((1, 1, tt), lambda t, h: (h, 0, t))],
        out_shape=[grp, big, grp, big, jax.ShapeDtypeStruct((n_heads, 1, n), f32)],
        compiler_params=_params("parallel", "arbitrary"),
        name="peer_route",
    )(q, keys)


PEER_I_PER_TILE = 4


def _peer_dense_kernel(xm_ref, u_ref, v_ref, s0_ref, e0_ref, s1_ref, e1_ref, th_ref, o_ref, s_s, w_s):
    e = pl.program_id(1)
    n_steps = pl.num_programs(1)

    @pl.when(e == 0)
    def _():
        o_ref[...] = jnp.zeros(o_ref.shape, f32)

    @pl.when(e > 0)
    def _():
        n_heads = s1_ref.shape[0]
        sub = ((e - 1) % (8 // PEER_I_PER_TILE)) * PEER_I_PER_TILE
        for il in range(PEER_I_PER_TILE):
            w = None
            for h in range(n_heads):
                s0 = s0_ref[h, 0, pl.ds(sub + il, 1), :]
                e0 = e0_ref[h, 0, pl.ds(sub + il, 1), :]
                keep = (s0 + s1_ref[h]) >= th_ref[h]
                term = jnp.where(keep, e0 * e1_ref[h], 0.0)
                w = term if w is None else w + term
            w_s[il * PEER_KEYS:(il + 1) * PEER_KEYS, :] = w
        act = (_gelu(s_s[...]) * w_s[...]).T.astype(bf16)
        o_ref[...] += _dot(act, v_ref[...])

    @pl.when(e < n_steps - 1)
    def _():
        s_s[...] = _dot_nt(u_ref[...], xm_ref[...])


def _peer_dense_call(xm, u_tab, v_tab, layer, s0, e0, s1, e1, th, tm=512):
    n, d = xm.shape
    n_exp = u_tab.shape[1]
    te = PEER_I_PER_TILE * PEER_KEYS
    n_tiles = n_exp // te
    n_heads = s1.shape[0]
    per8 = 8 // PEER_I_PER_TILE
    prev = lambda e: jnp.maximum(e - 1, 0)
    row8 = pl.BlockSpec((n_heads, 1, 8, tm), lambda t, e: (0, prev(e) // per8, 0, t))
    full = pl.BlockSpec((n_heads, PEER_KEYS, tm), lambda t, e: (0, 0, t))
    return pl.pallas_call(
        _peer_dense_kernel,
        grid=(n // tm, n_tiles + 1),
        in_specs=[pl.BlockSpec((tm, d), lambda t, e: (t, 0)),
                  pl.BlockSpec((None, te, d), lambda t, e: (layer, jnp.minimum(e, n_tiles - 1), 0)),
                  pl.BlockSpec((None, te, d), lambda t, e: (layer, prev(e), 0)),
                  row8, row8, full, full,
                  pl.BlockSpec((n_heads, 1, tm), lambda t, e: (0, 0, t))],
        out_specs=pl.BlockSpec((tm, d), lambda t, e: (t, 0)),
        out_shape=jax.ShapeDtypeStruct((n, d), f32),
        scratch_shapes=[pltpu.VMEM((te, tm), f32), pltpu.VMEM((te, tm), f32)],
        compiler_params=_params("parallel", "arbitrary", vmem=60 * 1024 * 1024),
        name="peer_dense",
    )(xm, u_tab, v_tab, s0, e0, s1, e1, th)


def kernel(x, c, ctx, c_ctx, w_ada, b_ada, w_in, a_q_norm, a_k_norm, b_sink, lru_conv_w, lru_conv_b, lru_w_r, lru_b_r, lru_w_i, lru_b_i, lru_lambda, s5_a_re, s5_a_im, s5_log_step, s5_b_re, s5_b_im, s5_c_re, s5_c_im, s5_d, s5_w_glu, s5_b_glu, w_gate, b_gate, w_branch, w_out, ln_g, ln_b, peer_w_q, peer_sub_keys, peer_u, peer_v):
    n_batch, n_lat, d = x.shape
    n_ctx = ctx.shape[1]
    depth = w_ada.shape[0]
    seq_tot = n_ctx + n_lat
    n_tok = n_batch * seq_tot
    mix_w = d // 4
    assert n_ctx % ROW_TILE == 0 and n_lat % ROW_TILE == 0 and n_tok % MM_TILE_M == 0
    assert n_ctx == ROW_TILE, "scan kernels treat row tile 0 of every batch element as the context"
    tiles_per_batch = seq_tot // ROW_TILE
    geom = (tiles_per_batch, n_ctx // ROW_TILE, n_batch)
    alpha = (2.0 * depth) ** 0.25
    s5_steps = seq_tot // 4
    assert seq_tot % 4 == 0 and s5_steps % (8 * S5_CHUNK) == 0 and n_ctx % S5_CHUNK == 0

    cin = jnp.zeros((8, d), f32).at[:n_batch].set(c).at[n_batch].set(c_ctx)
    mod = _ada_call(cin, w_ada, b_ada)

    def mod_vec(l, k):
        return mod[l, :, k * d:(k + 1) * d].reshape(8, 1, d)

    cos, sin = _rope_tables(n_lat, n_ctx)
    w_in_b, w_gate_b, w_branch_b, w_out_b = (w.astype(bf16) for w in (w_in, w_gate, w_branch, w_out))
    w_glu_b, w_q_b, u_tab_b, v_tab_b = (w.astype(bf16) for w in (s5_w_glu, peer_w_q, peer_u, peer_v))
    xs = jnp.concatenate([ctx, x], axis=1).reshape(n_tok, d)
    um = _mod_call(xs, mod_vec(0, 1), mod_vec(0, 0), geom)

    qkv_cols = N_QKV_SLICES * HEAD_DIM
    for l in range(depth):
        z = _mm_call(um, w_in_b, l, f32, name="in_proj")
        qkv = _prep_call(z, cos, sin, a_q_norm[l], a_k_norm[l], n_batch, seq_tot)
        ya = _attn_a_call(qkv, n_ctx).reshape(n_tok, mix_w)
        yb = _attn_b_call(qkv, b_sink[l], n_ctx).reshape(n_tok, mix_w)

        lru_x_block = qkv_cols // mix_w
        a_coef, b_coef = _lru_gate_call(
            z, lru_x_block, lru_conv_w[l], lru_conv_b[l], lru_w_r[l].astype(bf16), lru_b_r[l],
            lru_w_i[l].astype(bf16), lru_b_i[l], jax.nn.softplus(-lru_lambda[l]), tiles_per_batch)
        hf, hr = _lru_scan_call(a_coef, b_coef, n_batch, tiles_per_batch)
        yr = _lru_out_call(hf, hr, z, lru_x_block + 1)

        s5_block0 = (qkv_cols + 2 * mix_w) // 128
        k_oct, w_end, m_read, lam_r = _s5_weights(s5_a_re[l], s5_a_im[l], s5_log_step[l], s5_b_re[l],
                                                  s5_b_im[l], s5_c_re[l], s5_c_im[l])
        e_parts = _s5_inject_call(z, s5_block0, w_end, s5_steps)
        h_parts = _s5_scan_call(e_parts, lam_r, n_batch, n_ctx // S5_CHUNK)
        y_s5 = _s5_read_call(z, s5_block0, h_parts, k_oct, m_read, s5_d[l], s5_steps)
        ys = _glu_call(y_s5, w_glu_b, l, s5_b_glu[l])

        merged = _merge_call(um, (ya, yb, yr, ys), w_gate_b, b_gate[l], w_branch_b, l)
        y = _mm_call(merged, w_out_b, l, f32, name="out_proj")
        xs, um = _ln_call(xs, y, mod_vec(l, 2), ln_g[l, 0], ln_b[l, 0], mod_vec(l, 4), mod_vec(l, 3), geom, alpha)

        q = _mm_call(um, w_q_b, l, bf16, name="peer_query")
        n_heads = peer_sub_keys.shape[1]
        keys = peer_sub_keys[l].astype(bf16).reshape(2 * n_heads, PEER_KEYS, -1)
        s0, s1, e0, e1, th = _peer_route_call(q, keys)
        f = _peer_dense_call(um, u_tab_b, v_tab_b, l, s0, e0, s1, e1, th)
        nxt = min(l + 1, depth - 1)
        xs, um = _ln_call(xs, f, mod_vec(l, 5), ln_g[l, 1], ln_b[l, 1], mod_vec(nxt, 1), mod_vec(nxt, 0), geom, alpha)

    return xs.reshape(n_batch, seq_tot, d)[:, n_ctx:, :]
```

```python
import functools
import math

import jax
import jax.numpy as jnp
from jax import lax
from jax.experimental import pallas as pl
from jax.experimental.pallas import tpu as pltpu

f32 = jnp.float32
bf16 = jnp.bfloat16

HEAD_DIM = 128
GRID_W = 64
ROPE_THETA = 10000.0
WINDOW = 128
Q_TILE = 128
KV_GROUP = 4
LRU_C = 8.0
LRU_BLOCKS = 8
S5_GROUP = 16
S5_STATE = 64
S5_CHUNK = 8
S5_OCT = 8
PEER_TOPK = 16
PEER_KEYS = 128
LN_EPS = 1e-6
NEG = -1e30
ROW_TILE = 256
MM_TILE_M = 512
V7X_VMEM_LIMIT = 56 * 1024 * 1024


def _params(*sem, vmem=V7X_VMEM_LIMIT):
    return pltpu.CompilerParams(dimension_semantics=sem, vmem_limit_bytes=vmem)


def _dot(a, b):
    return jnp.dot(a, b, preferred_element_type=f32)


def _dot_nt(a, b):
    return lax.dot_general(a, b, (((1,), (1,)), ((), ())), preferred_element_type=f32)


def _gelu(x):
    return x * (0.5 * (1.0 + jnp.tanh(math.sqrt(2.0 / math.pi) * (x + 0.044715 * (x * x * x)))))


def _sigmoid(x):
    return 1.0 / (1.0 + jnp.exp(-x))


def _ada_kernel(c_ref, w_ref, b_ref, o_ref):
    c = c_ref[...]
    a = (c * _sigmoid(c)).astype(bf16)
    o_ref[0] = _dot(a, w_ref[0].astype(bf16)) + b_ref[0]


def _ada_call(cin, w_ada, b_ada, tn=512):
    depth, d, n = w_ada.shape
    return pl.pallas_call(
        _ada_kernel,
        grid=(depth, n // tn),
        in_specs=[pl.BlockSpec((8, d), lambda l, j: (0, 0)),
                  pl.BlockSpec((1, d, tn), lambda l, j: (l, 0, j)),
                  pl.BlockSpec((1, 1, tn), lambda l, j: (l, 0, j))],
        out_specs=pl.BlockSpec((1, 8, tn), lambda l, j: (l, 0, j)),
        out_shape=jax.ShapeDtypeStruct((depth, 8, n), f32),
        compiler_params=_params("parallel", "parallel"),
        name="adaln",
    )(cin, w_ada, b_ada.reshape(depth, 1, n))


def _row_group(i, tiles_per_batch, ctx_tiles, n_batch):
    return jnp.where(i % tiles_per_batch < ctx_tiles, n_batch, i // tiles_per_batch)


def _stream_specs(stream, geom, d):
    tpb = geom[0]
    if len(stream) == 1:
        return [pl.BlockSpec((ROW_TILE, d), lambda i: (i, 0))]
    return [pl.BlockSpec((1, ROW_TILE, d), lambda i: (i // tpb, jnp.maximum(i % tpb - 1, 0), 0)),
            pl.BlockSpec((1, ROW_TILE, d), lambda i: (i // tpb, 0, 0))]


def _read_stream(refs, is_ctx):
    if len(refs) == 1:
        return refs[0][...]
    return jnp.where(is_ctx, refs[1][0], refs[0][0])


def _mod_kernel(*refs, n_stream, tiles_per_batch):
    sc_ref, sh_ref, o_ref = refs[n_stream:]
    x = _read_stream(refs[:n_stream], pl.program_id(0) % tiles_per_batch == 0)
    o_ref[...] = (x * (1.0 + sc_ref[0]) + sh_ref[0]).astype(o_ref.dtype)


def _mod_call(stream, n, sc, sh, geom):
    d = sc.shape[-1]
    tpb, ct, nb = geom
    sel = lambda i: (_row_group(i, tpb, ct, nb), 0, 0)
    return pl.pallas_call(
        functools.partial(_mod_kernel, n_stream=len(stream), tiles_per_batch=tpb),
        grid=(n // ROW_TILE,),
        in_specs=_stream_specs(stream, geom, d) + [pl.BlockSpec((1, 1, d), sel), pl.BlockSpec((1, 1, d), sel)],
        out_specs=pl.BlockSpec((ROW_TILE, d), lambda i: (i, 0)),
        out_shape=jax.ShapeDtypeStruct((n, d), bf16),
        compiler_params=_params("parallel"),
        name="modulate",
    )(*stream, sc, sh)


def _ln_kernel(*refs, n_stream, tiles_per_batch, alpha, last):
    y_ref, g_ref, lg_ref, lb_ref = refs[n_stream:n_stream + 4]
    is_ctx = pl.program_id(0) % tiles_per_batch == 0
    v = alpha * _read_stream(refs[:n_stream], is_ctx) + g_ref[0] * y_ref[...]
    mu = jnp.mean(v, axis=-1, keepdims=True)
    vc = v - mu
    var = jnp.mean(vc * vc, axis=-1, keepdims=True)
    o = vc * lax.rsqrt(var + LN_EPS) * lg_ref[...] + lb_ref[...]
    if last:
        xo_ref, = refs[n_stream + 4:]

        @pl.when(jnp.logical_not(is_ctx))
        def _():
            xo_ref[0] = o
    else:
        sc_ref, sh_ref, xo_ref, xm_ref = refs[n_stream + 4:]
        xo_ref[...] = o
        xm_ref[...] = (o * (1.0 + sc_ref[0]) + sh_ref[0]).astype(xm_ref.dtype)


def _ln_call(stream, y, gate, ln_g, ln_b, nxt_mod, geom, alpha):
    n, d = y.shape
    tpb, ct, nb = geom
    sel = lambda i: (_row_group(i, tpb, ct, nb), 0, 0)
    row = pl.BlockSpec((ROW_TILE, d), lambda i: (i, 0))
    vec = pl.BlockSpec((1, d), lambda i: (0, 0))
    mod = pl.BlockSpec((1, 1, d), sel)
    last = nxt_mod is None
    in_specs = _stream_specs(stream, geom, d) + [row, mod, vec, vec]
    args = list(stream) + [y, gate, ln_g.reshape(1, d), ln_b.reshape(1, d)]
    if last:
        out_specs = [pl.BlockSpec((1, ROW_TILE, d), lambda i: (i // tpb, jnp.maximum(i % tpb - 1, 0), 0))]
        out_shape = [jax.ShapeDtypeStruct((nb, n // nb - ct * ROW_TILE, d), f32)]
    else:
        in_specs += [mod, mod]
        args += list(nxt_mod)
        out_specs = [row, row]
        out_shape = [jax.ShapeDtypeStruct((n, d), f32), jax.ShapeDtypeStruct((n, d), bf16)]
    return pl.pallas_call(
        functools.partial(_ln_kernel, n_stream=len(stream), tiles_per_batch=tpb, alpha=alpha, last=last),
        grid=(n // ROW_TILE,),
        in_specs=in_specs, out_specs=out_specs, out_shape=out_shape,
        compiler_params=_params("arbitrary" if last else "parallel"),
        name="residual_ln",
    )(*args)


def _mm_kernel(a_ref, b_ref, o_ref):
    o_ref[...] = _dot(a_ref[...], b_ref[...]).astype(o_ref.dtype)


def _mm_call(a, b, layer, out_dtype, tn=512, tm=MM_TILE_M, name="matmul"):
    m, k = a.shape
    n = b.shape[2]
    return pl.pallas_call(
        _mm_kernel,
        grid=(m // tm, n // tn),
        in_specs=[pl.BlockSpec((tm, k), lambda i, j: (i, 0)),
                  pl.BlockSpec((None, k, tn), lambda i, j: (layer, 0, j))],
        out_specs=pl.BlockSpec((tm, tn), lambda i, j: (i, j)),
        out_shape=jax.ShapeDtypeStruct((m, n), out_dtype),
        compiler_params=_params("parallel", "arbitrary"),
        name=name,
    )(a, b)


def _glu_kernel(a_ref, wv_ref, wg_ref, bv_ref, bg_ref, o_ref):
    a = a_ref[...].astype(bf16)
    val = _dot(a, wv_ref[...]) + bv_ref[...]
    gate = _dot(a, wg_ref[...]) + bg_ref[...]
    o_ref[...] = (val * _sigmoid(gate)).astype(o_ref.dtype)


def _glu_call(a, w, layer, b, tn=512, tm=MM_TILE_M):
    m, k = a.shape
    n = w.shape[2] // 2
    nj = n // tn
    b2 = b.reshape(1, 2 * n)
    return pl.pallas_call(
        _glu_kernel,
        grid=(m // tm, nj),
        in_specs=[pl.BlockSpec((tm, k), lambda i, j: (i, 0)),
                  pl.BlockSpec((None, k, tn), lambda i, j: (layer, 0, j)),
                  pl.BlockSpec((None, k, tn), lambda i, j: (layer, 0, j + nj)),
                  pl.BlockSpec((1, tn), lambda i, j: (0, j)),
                  pl.BlockSpec((1, tn), lambda i, j: (0, j + nj))],
        out_specs=pl.BlockSpec((tm, tn), lambda i, j: (i, j)),
        out_shape=jax.ShapeDtypeStruct((m, n), bf16),
        compiler_params=_params("parallel", "arbitrary"),
        name="s5_glu",
    )(a, w, w, b2, b2)


def _merge_kernel(u_ref, ya_ref, yb_ref, yr_ref, ys_ref, wg_ref, bg_ref, wb_ref, o_ref):
    u = u_ref[...]
    acc = None
    for i, y_ref in enumerate((ya_ref, yb_ref, yr_ref, ys_ref)):
        gate = _sigmoid(_dot(u, wg_ref[i]) + bg_ref[i])
        term = gate * _dot(y_ref[...], wb_ref[i])
        acc = term if acc is None else acc + term
    o_ref[...] = acc.astype(o_ref.dtype)


def _merge_call(u, branches, w_gate, b_gate, w_branch, layer, tn=256, tm=MM_TILE_M):
    m, d = u.shape
    _, nbr, bw, n = w_branch.shape
    ybr = pl.BlockSpec((tm, bw), lambda i, j: (i, 0))
    return pl.pallas_call(
        _merge_kernel,
        grid=(m // tm, n // tn),
        in_specs=[pl.BlockSpec((tm, d), lambda i, j: (i, 0)), ybr, ybr, ybr, ybr,
                  pl.BlockSpec((None, nbr, d, tn), lambda i, j: (layer, 0, 0, j)),
                  pl.BlockSpec((nbr, 1, tn), lambda i, j: (0, 0, j)),
                  pl.BlockSpec((None, nbr, bw, tn), lambda i, j: (layer, 0, 0, j))],
        out_specs=pl.BlockSpec((tm, tn), lambda i, j: (i, j)),
        out_shape=jax.ShapeDtypeStruct((m, n), bf16),
        compiler_params=_params("parallel", "arbitrary"),
        name="branch_merge",
    )(u, *branches, w_gate, b_gate.reshape(nbr, 1, n), w_branch)


def _rope_tables(n_lat, n_ctx):
    t = jnp.arange(n_lat)
    row = (t // GRID_W).astype(f32)
    col = (t % GRID_W).astype(f32)
    n_freq = HEAD_DIM // 4
    inv = ROPE_THETA ** (-jnp.arange(n_freq, dtype=f32) / n_freq)
    ang = jnp.concatenate([row[:, None] * inv, col[:, None] * inv], axis=-1)
    cos = jnp.repeat(jnp.cos(ang), 2, axis=-1)
    sin = jnp.repeat(jnp.sin(ang), 2, axis=-1) * jnp.tile(jnp.array([-1.0, 1.0], f32), HEAD_DIM // 2)
    cos = jnp.concatenate([jnp.ones((n_ctx, HEAD_DIM), f32), cos], axis=0)
    sin = jnp.concatenate([jnp.zeros((n_ctx, HEAD_DIM), f32), sin], axis=0)
    return cos, sin


N_QKV_SLICES = 24
A_Q, A_K, A_V, B_Q, B_K, B_V = 0, 8, 10, 12, 20, 22


def _prep_kernel(z_ref, cos_ref, sin_ref, gq_ref, gk_ref, o_ref):
    cos, sin = cos_ref[...], sin_ref[...]
    even = lax.broadcasted_iota(jnp.int32, cos.shape, 1) % 2 == 0
    for s in range(N_QKV_SLICES):
        y = z_ref[:, s * HEAD_DIM:(s + 1) * HEAD_DIM]
        if s < A_V:
            gain = gq_ref[...] if s < A_K else gk_ref[...]
            y = y * lax.rsqrt(jnp.mean(y * y, axis=-1, keepdims=True) + LN_EPS) * gain
        if s < A_V or B_Q <= s < B_V:
            partner = jnp.where(even, pltpu.roll(y, HEAD_DIM - 1, axis=1), pltpu.roll(y, 1, axis=1))
            y = y * cos + partner * sin
        if s < A_K or B_Q <= s < B_K:
            y = y * (HEAD_DIM ** -0.5)
        o_ref[0, s] = y.astype(o_ref.dtype)


def _prep_call(z, cos, sin, gain_q, gain_k, n_batch, seq_tot):
    n = z.shape[0]
    tr = ROW_TILE
    tpb = seq_tot // tr
    width = N_QKV_SLICES * HEAD_DIM
    vec = pl.BlockSpec((1, HEAD_DIM), lambda i: (0, 0))
    return pl.pallas_call(
        _prep_kernel,
        grid=(n // tr,),
        in_specs=[pl.BlockSpec((tr, width), lambda i: (i, 0)),
                  pl.BlockSpec((tr, HEAD_DIM), lambda i: (i % tpb, 0)),
                  pl.BlockSpec((tr, HEAD_DIM), lambda i: (i % tpb, 0)), vec, vec],
        out_specs=pl.BlockSpec((1, N_QKV_SLICES, tr, HEAD_DIM), lambda i: (i // tpb, 0, i % tpb, 0)),
        out_shape=jax.ShapeDtypeStruct((n_batch, N_QKV_SLICES, seq_tot, HEAD_DIM), bf16),
        compiler_params=_params("parallel"),
        name="qkv_prep",
    )(z, cos, sin, gain_q.reshape(1, HEAD_DIM), gain_k.reshape(1, HEAD_DIM))


def _attn_a_kernel(q_ref, k_ref, v_ref, o_ref, m_s, acc_s, *, n_ctx, tk, n_lat_chunks):
    q = q_ref[0].reshape(KV_GROUP * Q_TILE, HEAD_DIM)

    def values(lo, n):
        return jnp.concatenate([v_ref[0, 0, lo:lo + n, :], jnp.ones((n, HEAD_DIM), bf16)], axis=1)

    s = _dot_nt(q, k_ref[0, 0, 0:n_ctx, :])
    m0 = jnp.max(s, axis=-1, keepdims=True)
    m_s[...] = m0
    acc_s[...] = _dot(jnp.exp((s - m0).astype(bf16)), values(0, n_ctx))

    @pl.when(pl.program_id(2) >= n_ctx // Q_TILE)
    def _():
        for c in range(n_lat_chunks):
            lo = n_ctx + c * tk
            s = _dot_nt(q, k_ref[0, 0, lo:lo + tk, :])
            m_prev = m_s[...]
            m_new = jnp.maximum(m_prev, jnp.max(s, axis=-1, keepdims=True))
            p = jnp.exp((s - m_new).astype(bf16))
            acc_s[...] = jnp.exp(m_prev - m_new) * acc_s[...] + _dot(p, values(lo, tk))
            m_s[...] = m_new

    out = acc_s[:, 0:HEAD_DIM] / acc_s[:, HEAD_DIM:2 * HEAD_DIM]
    for g in range(KV_GROUP):
        o_ref[0, :, g * HEAD_DIM:(g + 1) * HEAD_DIM] = out[g * Q_TILE:(g + 1) * Q_TILE].astype(o_ref.dtype)


def _attn_a_call(qkv, n_ctx, tk=1024):
    nb, _, seq_tot, _ = qkv.shape
    n_kv = (A_V - A_K)
    rows = KV_GROUP * Q_TILE
    n_lat = seq_tot - n_ctx
    tk = min(tk, n_lat)
    assert n_lat % tk == 0
    kern = functools.partial(_attn_a_kernel, n_ctx=n_ctx, tk=tk, n_lat_chunks=n_lat // tk)
    return pl.pallas_call(
        kern,
        grid=(nb, n_kv, seq_tot // Q_TILE),
        in_specs=[pl.BlockSpec((1, KV_GROUP, Q_TILE, HEAD_DIM), lambda b, h, i: (b, A_Q // KV_GROUP + h, i, 0)),
                  pl.BlockSpec((1, 1, seq_tot, HEAD_DIM), lambda b, h, i: (b, A_K + h, 0, 0)),
                  pl.BlockSpec((1, 1, seq_tot, HEAD_DIM), lambda b, h, i: (b, A_V + h, 0, 0))],
        out_specs=pl.BlockSpec((1, Q_TILE, KV_GROUP * HEAD_DIM), lambda b, h, i: (b, i, h)),
        out_shape=jax.ShapeDtypeStruct((nb, seq_tot, n_kv * KV_GROUP * HEAD_DIM), bf16),
        scratch_shapes=[pltpu.VMEM((rows, 1), f32), pltpu.VMEM((rows, 2 * HEAD_DIM), f32)],
        compiler_params=_params("parallel", "parallel", "arbitrary"),
        name="dense_attention",
    )(qkv, qkv, qkv)


def _attn_b_kernel(q_ref, k_ref, v_ref, sink_ref, o_ref, *, n_ctx, n_blk):
    i = pl.program_id(2)
    blk = i - n_ctx // Q_TILE
    rows = KV_GROUP * Q_TILE
    q = q_ref[0].reshape(rows, HEAD_DIM)
    sink = sink_ref[0]
    kc = k_ref[0, 0, 0:n_ctx, :]
    vc = v_ref[0, 0, 0:n_ctx, :]
    s_ctx = _dot_nt(q, kc)
    m = jnp.maximum(jnp.max(s_ctx, axis=-1, keepdims=True), sink)
    q_pos = lax.broadcasted_iota(jnp.int32, (rows, Q_TILE), 0) % Q_TILE
    k_off = lax.broadcasted_iota(jnp.int32, (rows, Q_TILE), 1)
    s_loc, v_loc = [], []
    for d in (-1, 0, 1):
        j = blk + d
        valid = jnp.logical_and(blk >= 0, jnp.logical_and(j >= 0, j < n_blk))
        start = pl.multiple_of(n_ctx + jnp.clip(j, 0, n_blk - 1) * Q_TILE, Q_TILE)
        k = k_ref[0, 0, pl.ds(start, Q_TILE), :]
        v_loc.append(v_ref[0, 0, pl.ds(start, Q_TILE), :])
        keep = jnp.logical_and(jnp.abs(k_off + d * Q_TILE - q_pos) <= WINDOW, valid)
        s = jnp.where(keep, _dot_nt(q, k), NEG)
        m = jnp.maximum(m, jnp.max(s, axis=-1, keepdims=True))
        s_loc.append(s)
    ones = jnp.ones((Q_TILE, HEAD_DIM), bf16)
    acc = _dot(jnp.exp((s_ctx - m).astype(bf16)), jnp.concatenate([vc, jnp.ones((n_ctx, HEAD_DIM), bf16)], axis=1))
    for s, v in zip(s_loc, v_loc):
        acc = acc + _dot(jnp.exp((s - m).astype(bf16)), jnp.concatenate([v, ones], axis=1))
    out = acc[:, 0:HEAD_DIM] / (jnp.exp(sink - m) + acc[:, HEAD_DIM:2 * HEAD_DIM])
    for g in range(KV_GROUP):
        o_ref[0, :, g * HEAD_DIM:(g + 1) * HEAD_DIM] = out[g * Q_TILE:(g + 1) * Q_TILE].astype(o_ref.dtype)


def _attn_b_call(qkv, sink, n_ctx):
    nb, _, seq_tot, _ = qkv.shape
    n_kv = (B_V - B_K)
    rows = KV_GROUP * Q_TILE
    sink_rows = jnp.repeat(sink.astype(f32).reshape(n_kv, KV_GROUP), Q_TILE, axis=1).reshape(n_kv, rows, 1)
    kern = functools.partial(_attn_b_kernel, n_ctx=n_ctx, n_blk=(seq_tot - n_ctx) // Q_TILE)
    return pl.pallas_call(
        kern,
        grid=(nb, n_kv, seq_tot // Q_TILE),
        in_specs=[pl.BlockSpec((1, KV_GROUP, Q_TILE, HEAD_DIM), lambda b, h, i: (b, B_Q // KV_GROUP + h, i, 0)),
                  pl.BlockSpec((1, 1, seq_tot, HEAD_DIM), lambda b, h, i: (b, B_K + h, 0, 0)),
                  pl.BlockSpec((1, 1, seq_tot, HEAD_DIM), lambda b, h, i: (b, B_V + h, 0, 0)),
                  pl.BlockSpec((1, rows, 1), lambda b, h, i: (h, 0, 0))],
        out_specs=pl.BlockSpec((1, Q_TILE, KV_GROUP * HEAD_DIM), lambda b, h, i: (b, i, h)),
        out_shape=jax.ShapeDtypeStruct((nb, seq_tot, n_kv * KV_GROUP * HEAD_DIM), bf16),
        compiler_params=_params("parallel", "parallel", "arbitrary"),
        name="window_attention",
    )(qkv, qkv, qkv, sink_rows)


HALO = 8


def _lru_gate_kernel(x_ref, xp_ref, xn_ref, cw_ref, cb_ref, wr_ref, br_ref, wi_ref, bi_ref, sp_ref,
                     a_ref, b_ref, pad_s, *, tiles_per_batch):
    tr = x_ref.shape[0]
    t = pl.program_id(0) % tiles_per_batch
    has_prev = t >= 2
    has_next = jnp.logical_and(t >= 1, t < tiles_per_batch - 1)
    x = x_ref[...]
    pad_s[0:HALO, :] = jnp.where(has_prev, xp_ref[...], 0.0)
    pad_s[HALO:HALO + tr, :] = x
    pad_s[HALO + tr:2 * HALO + tr, :] = jnp.where(has_next, xn_ref[...], 0.0)
    cw = cw_ref[...]
    xl = cb_ref[...] + pad_s[HALO - 2:HALO - 2 + tr, :] * cw[0:1]
    xl = xl + pad_s[HALO - 1:HALO - 1 + tr, :] * cw[1:2]
    xl = xl + x * cw[2:3]
    xl = xl + pad_s[HALO + 1:HALO + 1 + tr, :] * cw[3:4]
    bw = xl.shape[1] // LRU_BLOCKS
    for n in range(LRU_BLOCKS):
        cols = slice(n * bw, (n + 1) * bw)
        xb = xl[:, cols]
        xb16 = xb.astype(bf16)
        for d in range(2):
            r = _sigmoid(_dot(xb16, wr_ref[d, n]) + br_ref[d:d + 1, cols])
            gi = _sigmoid(_dot(xb16, wi_ref[d, n]) + bi_ref[d:d + 1, cols])
            log_a = (-LRU_C) * r * sp_ref[d:d + 1, cols]
            a = jnp.exp(log_a)
            a_ref[d, :, cols] = a
            b_ref[d, :, cols] = jnp.sqrt(1.0 - a * a) * gi * xb


def _lru_gate_call(z, col_block, conv_w, conv_b, w_r, b_r, w_i, b_i, softplus_lam, tiles_per_batch):
    n = z.shape[0]
    w = conv_w.shape[1]
    tr = ROW_TILE
    hb = tr // HALO
    n_halo = n // HALO
    full = lambda shape: pl.BlockSpec(shape, lambda i: (0,) * len(shape))
    kern = functools.partial(_lru_gate_kernel, tiles_per_batch=tiles_per_batch)
    out = jax.ShapeDtypeStruct((2, n, w), f32)
    return pl.pallas_call(
        kern,
        grid=(n // tr,),
        in_specs=[pl.BlockSpec((tr, w), lambda i: (i, col_block)),
                  pl.BlockSpec((HALO, w), lambda i: (jnp.maximum(i * hb - 1, 0), col_block)),
                  pl.BlockSpec((HALO, w), lambda i: (jnp.minimum((i + 1) * hb, n_halo - 1), col_block)),
                  full(conv_w.shape), full((1, w)), full(w_r.shape), full(b_r.shape),
                  full(w_i.shape), full(b_i.shape), full(softplus_lam.shape)],
        out_specs=[pl.BlockSpec((2, tr, w), lambda i: (0, i, 0)), pl.BlockSpec((2, tr, w), lambda i: (0, i, 0))],
        out_shape=[out, out],
        scratch_shapes=[pltpu.VMEM((tr + 2 * HALO, w), f32)],
        compiler_params=_params("parallel"),
        name="lru_gates",
    )(z, z, z, conv_w, conv_b.reshape(1, w), w_r, b_r, w_i, b_i, softplus_lam)


def _lru_scan_kernel(af_ref, bf_ref, ar_ref, br_ref, hf_ref, hr_ref, sf, sr):
    tr = hf_ref.shape[0]

    @pl.when(pl.program_id(1) == 0)
    def _():
        sf[...] = jnp.zeros(sf.shape, f32)
        sr[...] = jnp.zeros(sr.shape, f32)

    def body(t, carry):
        hf, hr = carry
        hf = af_ref[0, pl.ds(t, 1), :] * hf + bf_ref[0, pl.ds(t, 1), :]
        hf_ref[pl.ds(t, 1), :] = hf
        u = tr - 1 - t
        hr = ar_ref[0, pl.ds(u, 1), :] * hr + br_ref[0, pl.ds(u, 1), :]
        hr_ref[pl.ds(u, 1), :] = hr
        return hf, hr

    hf, hr = lax.fori_loop(0, tr, body, (sf[...], sr[...]), unroll=8)
    sf[...] = hf
    sr[...] = hr


def _scan_tile_maps(tiles_per_batch):
    fwd = lambda b, i: b * tiles_per_batch + i
    rev = lambda b, i: b * tiles_per_batch + jnp.where(i == 0, 0, tiles_per_batch - i)
    return fwd, rev


def _lru_scan_call(a, b, n_batch, tiles_per_batch):
    _, n, w = a.shape
    tr = ROW_TILE
    fwd, rev = _scan_tile_maps(tiles_per_batch)
    out = jax.ShapeDtypeStruct((n, w), f32)
    return pl.pallas_call(
        _lru_scan_kernel,
        grid=(n_batch, tiles_per_batch),
        in_specs=[pl.BlockSpec((1, tr, w), lambda bb, i: (0, fwd(bb, i), 0)),
                  pl.BlockSpec((1, tr, w), lambda bb, i: (0, fwd(bb, i), 0)),
                  pl.BlockSpec((1, tr, w), lambda bb, i: (1, rev(bb, i), 0)),
                  pl.BlockSpec((1, tr, w), lambda bb, i: (1, rev(bb, i), 0))],
        out_specs=[pl.BlockSpec((tr, w), lambda bb, i: (fwd(bb, i), 0)),
                   pl.BlockSpec((tr, w), lambda bb, i: (rev(bb, i), 0))],
        out_shape=[out, out],
        scratch_shapes=[pltpu.VMEM((1, w), f32), pltpu.VMEM((1, w), f32)],
        compiler_params=_params("parallel", "arbitrary"),
        name="lru_scan",
    )(a, b, a, b)


def _lru_out_kernel(hf_ref, hr_ref, g_ref, o_ref):
    o_ref[...] = ((hf_ref[...] + hr_ref[...]) * _gelu(g_ref[...])).astype(o_ref.dtype)


def _lru_out_call(hf, hr, z, gate_col_block):
    n, w = hf.shape
    tr = ROW_TILE
    row = pl.BlockSpec((tr, w), lambda i: (i, 0))
    return pl.pallas_call(
        _lru_out_kernel,
        grid=(n // tr,),
        in_specs=[row, row, pl.BlockSpec((tr, w), lambda i: (i, gate_col_block))],
        out_specs=row,
        out_shape=jax.ShapeDtypeStruct((n, w), bf16),
        compiler_params=_params("parallel"),
        name="lru_out",
    )(hf, hr, z)


def _s5_weights(a_re, a_im, log_step, b_re, b_im, c_re, c_im):
    r = S5_CHUNK
    n_g = a_re.shape[1]
    n_oct = n_g // S5_OCT
    step = jnp.exp(log_step)[..., None]
    mag = jnp.exp(a_re * step)
    lb_re, lb_im = mag * jnp.cos(a_im * step), mag * jnp.sin(a_im * step)
    den = a_re * a_re + a_im * a_im
    num_re = lb_re - 1.0
    coef_re = (num_re * a_re + lb_im * a_im) / den
    coef_im = (lb_im * a_re - num_re * a_im) / den
    bb_re = coef_re[..., None] * b_re - coef_im[..., None] * b_im
    bb_im = coef_re[..., None] * b_im + coef_im[..., None] * b_re
    tau = jnp.arange(r + 1, dtype=f32)[:, None, None, None]
    pmag = jnp.exp(tau * (a_re * step))
    pw_re, pw_im = pmag * jnp.cos(tau * (a_im * step)), pmag * jnp.sin(tau * (a_im * step))

    lb_b_re = pw_re[..., None] * bb_re - pw_im[..., None] * bb_im
    lb_b_im = pw_re[..., None] * bb_im + pw_im[..., None] * bb_re
    lag = (jnp.einsum('tdgpc,dgop->tdgco', lb_b_re, c_re) - jnp.einsum('tdgpc,dgop->tdgco', lb_b_im, c_im))
    idx = jnp.arange(r)
    sh_f = (idx[None, None, :] - idx[None, :, None] == jnp.arange(r + 1)[:, None, None]).astype(f32)
    sh_r = (idx[None, :, None] - idx[None, None, :] == jnp.arange(r + 1)[:, None, None]).astype(f32)
    k_loc = jnp.einsum('tio,tgcd->gicod', sh_f, lag[:, 0]) + jnp.einsum('tio,tgcd->gicod', sh_r, lag[:, 1])
    eye = jnp.eye(S5_OCT, dtype=f32)
    k_loc = k_loc.reshape(n_oct, S5_OCT, r, S5_GROUP, r, S5_GROUP)
    k_oct = jnp.einsum('kgicod,gh->kigcohd', k_loc, eye).reshape(n_oct, r * 128, r * 128)

    def inject(d, powers):
        w_re = pw_re[powers, d][..., None] * bb_re[d] - pw_im[powers, d][..., None] * bb_im[d]
        w_im = pw_re[powers, d][..., None] * bb_im[d] + pw_im[powers, d][..., None] * bb_re[d]
        out = []
        for w in (w_re, w_im):
            w = w.reshape(r, n_oct, S5_OCT, S5_STATE, S5_GROUP)
            out.append(jnp.einsum('ikgpc,gh->kigchp', w, eye).reshape(n_oct, r * 128, S5_OCT * S5_STATE))
        return out
    w_end = jnp.concatenate(inject(0, idx[::-1]) + inject(1, idx), axis=-1)

    def readout(d, powers):
        cl_re = c_re[d][None] * pw_re[powers, d][:, :, None, :] - c_im[d][None] * pw_im[powers, d][:, :, None, :]
        cl_im = c_re[d][None] * pw_im[powers, d][:, :, None, :] + c_im[d][None] * pw_re[powers, d][:, :, None, :]
        out = []
        for w in (cl_re, -cl_im):
            w = w.reshape(r, n_oct, S5_OCT, S5_GROUP, S5_STATE)
            out.append(jnp.einsum('okgcp,gh->kgpohc', w, eye).reshape(n_oct, S5_OCT * S5_STATE, r * 128))
        return out
    m_read = jnp.concatenate(readout(0, idx + 1) + readout(1, r - idx), axis=1)

    lam_r = jnp.stack([pw_re[r, 0].reshape(-1), pw_im[r, 0].reshape(-1),
                       pw_re[r, 1].reshape(-1), pw_im[r, 1].reshape(-1)])
    return k_oct.astype(bf16), w_end.astype(bf16), m_read.astype(bf16), lam_r


def _chunk_rows(u_ref):
    n_rows = u_ref.shape[0] // S5_CHUNK
    return [u_ref[pl.ds(i, n_rows, stride=S5_CHUNK), :] for i in range(S5_CHUNK)]


def _s5_inject_kernel(u_ref, w_ref, efr_ref, efi_ref, err_ref, eri_ref):
    x = jnp.concatenate(_chunk_rows(u_ref), axis=1).astype(bf16)
    e = _dot(x, w_ref[0])
    w = efr_ref.shape[1]
    for k, o_ref in enumerate((efr_ref, efi_ref, err_ref, eri_ref)):
        o_ref[...] = e[:, k * w:(k + 1) * w]


def _s5_inject_call(z, col_block0, w_end, steps):
    n = z.shape[0]
    n_oct = w_end.shape[0]
    sw = S5_OCT * S5_STATE
    rows = steps // S5_CHUNK
    out = jax.ShapeDtypeStruct((n // S5_CHUNK, n_oct * sw), f32)
    ospec = pl.BlockSpec((rows, sw), lambda i, k: (i, k))
    return pl.pallas_call(
        _s5_inject_kernel,
        grid=(n // steps, n_oct),
        in_specs=[pl.BlockSpec((steps, 128), lambda i, k: (i, col_block0 + k)),
                  pl.BlockSpec((1,) + w_end.shape[1:], lambda i, k: (k, 0, 0))],
        out_specs=[ospec] * 4,
        out_shape=[out] * 4,
        compiler_params=_params("parallel", "arbitrary"),
        name="s5_inject",
    )(z, w_end)


def _s5_scan_kernel(efr_ref, efi_ref, err_ref, eri_ref, lam_ref, hfr_ref, hfi_ref, hrr_ref, hri_ref, *, n_ctx):
    n_rows = efr_ref.shape[0]
    lfr, lfi, lrr, lri = (lam_ref[k:k + 1, :] for k in range(4))
    zero = jnp.zeros((1, efr_ref.shape[1]), f32)

    def body(t, carry):
        fr, fi, rr, ri = carry
        hfr_ref[pl.ds(t, 1), :] = fr
        hfi_ref[pl.ds(t, 1), :] = fi
        er, ei = efr_ref[pl.ds(t, 1), :], efi_ref[pl.ds(t, 1), :]
        fr, fi = lfr * fr - lfi * fi + er, lfr * fi + lfi * fr + ei
        u = jnp.where(t < n_ctx, n_ctx - 1 - t, n_rows - 1 - (t - n_ctx))
        hrr_ref[pl.ds(u, 1), :] = rr
        hri_ref[pl.ds(u, 1), :] = ri
        er, ei = err_ref[pl.ds(u, 1), :], eri_ref[pl.ds(u, 1), :]
        rr, ri = lrr * rr - lri * ri + er, lrr * ri + lri * rr + ei
        return fr, fi, rr, ri

    lax.fori_loop(0, n_rows, body, (zero, zero, zero, zero), unroll=4)


def _s5_scan_call(e_parts, lam_r, n_batch, n_ctx_rows, lane_block=512):
    n_rows_tot, width = e_parts[0].shape
    rows = n_rows_tot // n_batch
    blk = pl.BlockSpec((rows, lane_block), lambda b, j: (b, j))
    out = jax.ShapeDtypeStruct((n_rows_tot, width), f32)
    return pl.pallas_call(
        functools.partial(_s5_scan_kernel, n_ctx=n_ctx_rows),
        grid=(n_batch, width // lane_block),
        in_specs=[blk] * 4 + [pl.BlockSpec((4, lane_block), lambda b, j: (0, j))],
        out_specs=[blk] * 4,
        out_shape=[out] * 4,
        compiler_params=_params("parallel", "parallel"),
        name="s5_scan",
    )(*e_parts, lam_r)


def _s5_read_kernel(u_ref, hfr_ref, hfi_ref, hrr_ref, hri_ref, k_ref, m_ref, d_ref, o_ref):
    parts = _chunk_rows(u_ref)
    x = jnp.concatenate(parts, axis=1).astype(bf16)
    h = jnp.concatenate([hfr_ref[...], hfi_ref[...], hrr_ref[...], hri_ref[...]], axis=1).astype(bf16)
    y = _dot(x, k_ref[0]) + _dot(h, m_ref[0])
    n_rows = u_ref.shape[0] // S5_CHUNK
    for i in range(S5_CHUNK):
        o_ref[pl.ds(i, n_rows, stride=S5_CHUNK), :] = y[:, i * 128:(i + 1) * 128] + parts[i] * d_ref[...]


def _s5_read_call(z, col_block0, h_parts, k_oct, m_read, d_skip, steps):
    n = z.shape[0]
    n_oct = k_oct.shape[0]
    sw = S5_OCT * S5_STATE
    rows = steps // S5_CHUNK
    hspec = pl.BlockSpec((rows, sw), lambda i, k: (i, k))
    return pl.pallas_call(
        _s5_read_kernel,
        grid=(n // steps, n_oct),
        in_specs=[pl.BlockSpec((steps, 128), lambda i, k: (i, col_block0 + k))] + [hspec] * 4 +
                 [pl.BlockSpec((1,) + k_oct.shape[1:], lambda i, k: (k, 0, 0)),
                  pl.BlockSpec((1,) + m_read.shape[1:], lambda i, k: (k, 0, 0)),
                  pl.BlockSpec((1, 128), lambda i, k: (0, k))],
        out_specs=pl.BlockSpec((steps, 128), lambda i, k: (i, k)),
        out_shape=jax.ShapeDtypeStruct((n, n_oct * 128), f32),
        compiler_params=_params("parallel", "arbitrary"),
        name="s5_readout",
    )(z, *h_parts, k_oct, m_read, d_skip.reshape(1, -1))


def _top_values(s, k):
    vals = []
    for _ in range(k):
        m = jnp.max(s, axis=0, keepdims=True)
        vals.append(m)
        s = jnp.where(s == m, -jnp.inf, s)
    return vals


def _peer_route_kernel(q_ref, keys_ref, s0_ref, s1_ref, e0_ref, e1_ref, th_ref):
    k = PEER_TOPK
    s0 = _dot_nt(keys_ref[0], q_ref[:, 0:PEER_KEYS])
    s1 = _dot_nt(keys_ref[1], q_ref[:, PEER_KEYS:2 * PEER_KEYS])
    top0, top1 = _top_values(s0, k + 1), _top_values(s1, k + 1)
    pad = [jnp.full_like(top0[0], -jnp.inf)] * 7
    t0, t1 = jnp.concatenate(top0 + pad, axis=0), jnp.concatenate(top1 + pad, axis=0)
    cand = jnp.concatenate([top0[0] + t1] + [top0[i] + t1[0:8] for i in range(1, 8)] + [t0[8:24] + top1[0]], axis=0)
    best = _top_values(cand, k + 1)
    z = None
    for v in best[:k]:
        e = jnp.exp(v - best[0])
        z = e if z is None else z + e
    s0_ref[0] = s0.reshape(s0_ref.shape[1:])
    s1_ref[0] = s1
    e0_ref[0] = (jnp.exp(s0 - top0[0]) / z).reshape(e0_ref.shape[1:])
    e1_ref[0] = jnp.exp(s1 - top1[0])
    th_ref[0] = 0.5 * (best[k - 1] + best[k])


def _peer_route_call(q, keys, tt=256):
    n = q.shape[0]
    n_heads = keys.shape[0] // 2
    big = jax.ShapeDtypeStruct((n_heads, PEER_KEYS, n), f32)
    bspec = pl.BlockSpec((1, PEER_KEYS, tt), lambda t, h: (h, 0, t))
    grp = jax.ShapeDtypeStruct((n_heads, PEER_KEYS // 8, 8, n), f32)
    gspec = pl.BlockSpec((1, PEER_KEYS // 8, 8, tt), lambda t, h: (h, 0, 0, t))
    return pl.pallas_call(
        _peer_route_kernel,
        grid=(n // tt, n_heads),
        in_specs=[pl.BlockSpec((tt, 2 * PEER_KEYS), lambda t, h: (t, h)),
                  pl.BlockSpec((2, PEER_KEYS, PEER_KEYS), lambda t, h: (h, 0, 0))],
        out_specs=[gspec, bspec, gspec, bspec, pl.BlockSpec((1, 1, tt), lambda t, h: (h, 0, t))],
        out_shape=[grp, big, grp, big, jax.ShapeDtypeStruct((n_heads, 1, n), f32)],
        compiler_params=_params("parallel", "arbitrary"),
        name="peer_route",
    )(q, keys)


PEER_I_PER_TILE = 4


def _peer_dense_kernel(xm_ref, u_ref, v_ref, s0_ref, e0_ref, s1_ref, e1_ref, th_ref, o_ref, s_s, w_s):
    e = pl.program_id(1)
    n_steps = pl.num_programs(1)

    @pl.when(e == 0)
    def _():
        o_ref[...] = jnp.zeros(o_ref.shape, f32)

    @pl.when(e > 0)
    def _():
        n_heads = s1_ref.shape[0]
        sub = ((e - 1) % (8 // PEER_I_PER_TILE)) * PEER_I_PER_TILE
        for il in range(PEER_I_PER_TILE):
            w = None
            for h in range(n_heads):
                s0 = s0_ref[h, 0, pl.ds(sub + il, 1), :]
                e0 = e0_ref[h, 0, pl.ds(sub + il, 1), :]
                keep = s1_ref[h] >= th_ref[h] - s0
                term = jnp.where(keep, e0 * e1_ref[h], 0.0)
                w = term if w is None else w + term
            w_s[il * PEER_KEYS:(il + 1) * PEER_KEYS, :] = w
        act = (_gelu(s_s[...]) * w_s[...]).T.astype(bf16)
        o_ref[...] += _dot(act, v_ref[...])

    @pl.when(e < n_steps - 1)
    def _():
        s_s[...] = _dot_nt(u_ref[...], xm_ref[...])


def _peer_dense_call(xm, u_tab, v_tab, layer, s0, e0, s1, e1, th, tm=512):
    n, d = xm.shape
    n_exp = u_tab.shape[1]
    te = PEER_I_PER_TILE * PEER_KEYS
    n_tiles = n_exp // te
    n_heads = s1.shape[0]
    per8 = 8 // PEER_I_PER_TILE
    prev = lambda e: jnp.maximum(e - 1, 0)
    row8 = pl.BlockSpec((n_heads, 1, 8, tm), lambda t, e: (0, prev(e) // per8, 0, t))
    full = pl.BlockSpec((n_heads, PEER_KEYS, tm), lambda t, e: (0, 0, t))
    return pl.pallas_call(
        _peer_dense_kernel,
        grid=(n // tm, n_tiles + 1),
        in_specs=[pl.BlockSpec((tm, d), lambda t, e: (t, 0)),
                  pl.BlockSpec((None, te, d), lambda t, e: (layer, jnp.minimum(e, n_tiles - 1), 0)),
                  pl.BlockSpec((None, te, d), lambda t, e: (layer, prev(e), 0)),
                  row8, row8, full, full,
                  pl.BlockSpec((n_heads, 1, tm), lambda t, e: (0, 0, t))],
        out_specs=pl.BlockSpec((tm, d), lambda t, e: (t, 0)),
        out_shape=jax.ShapeDtypeStruct((n, d), f32),
        scratch_shapes=[pltpu.VMEM((te, tm), f32), pltpu.VMEM((te, tm), f32)],
        compiler_params=_params("parallel", "arbitrary", vmem=60 * 1024 * 1024),
        name="peer_dense",
    )(xm, u_tab, v_tab, s0, e0, s1, e1, th)


def kernel(x, c, ctx, c_ctx, w_ada, b_ada, w_in, a_q_norm, a_k_norm, b_sink, lru_conv_w, lru_conv_b, lru_w_r, lru_b_r, lru_w_i, lru_b_i, lru_lambda, s5_a_re, s5_a_im, s5_log_step, s5_b_re, s5_b_im, s5_c_re, s5_c_im, s5_d, s5_w_glu, s5_b_glu, w_gate, b_gate, w_branch, w_out, ln_g, ln_b, peer_w_q, peer_sub_keys, peer_u, peer_v):
    n_batch, n_lat, d = x.shape
    n_ctx = ctx.shape[1]
    depth = w_ada.shape[0]
    seq_tot = n_ctx + n_lat
    n_tok = n_batch * seq_tot
    mix_w = d // 4
    assert n_ctx % ROW_TILE == 0 and n_lat % ROW_TILE == 0 and n_tok % MM_TILE_M == 0
    assert n_ctx == ROW_TILE, "scan kernels treat row tile 0 of every batch element as the context"
    tiles_per_batch = seq_tot // ROW_TILE
    geom = (tiles_per_batch, n_ctx // ROW_TILE, n_batch)
    alpha = (2.0 * depth) ** 0.25
    s5_steps = seq_tot // 4
    assert seq_tot % 4 == 0 and s5_steps % (8 * S5_CHUNK) == 0 and n_ctx % S5_CHUNK == 0

    cin = jnp.zeros((8, d), f32).at[:n_batch].set(c).at[n_batch].set(c_ctx)
    mod = _ada_call(cin, w_ada, b_ada)

    def mod_vec(l, k):
        return mod[l, :, k * d:(k + 1) * d].reshape(8, 1, d)

    cos, sin = _rope_tables(n_lat, n_ctx)
    w_in_b, w_gate_b, w_branch_b, w_out_b = (w.astype(bf16) for w in (w_in, w_gate, w_branch, w_out))
    w_glu_b, w_q_b, u_tab_b, v_tab_b = (w.astype(bf16) for w in (s5_w_glu, peer_w_q, peer_u, peer_v))
    stream = (x, ctx)
    um = _mod_call(stream, n_tok, mod_vec(0, 1), mod_vec(0, 0), geom)

    qkv_cols = N_QKV_SLICES * HEAD_DIM
    for l in range(depth):
        z = _mm_call(um, w_in_b, l, f32, name="in_proj")
        qkv = _prep_call(z, cos, sin, a_q_norm[l], a_k_norm[l], n_batch, seq_tot)
        ya = _attn_a_call(qkv, n_ctx).reshape(n_tok, mix_w)
        yb = _attn_b_call(qkv, b_sink[l], n_ctx).reshape(n_tok, mix_w)

        lru_x_block = qkv_cols // mix_w
        a_coef, b_coef = _lru_gate_call(
            z, lru_x_block, lru_conv_w[l], lru_conv_b[l], lru_w_r[l].astype(bf16), lru_b_r[l],
            lru_w_i[l].astype(bf16), lru_b_i[l], jax.nn.softplus(-lru_lambda[l]), tiles_per_batch)
        hf, hr = _lru_scan_call(a_coef, b_coef, n_batch, tiles_per_batch)
        yr = _lru_out_call(hf, hr, z, lru_x_block + 1)

        s5_block0 = (qkv_cols + 2 * mix_w) // 128
        k_oct, w_end, m_read, lam_r = _s5_weights(s5_a_re[l], s5_a_im[l], s5_log_step[l], s5_b_re[l],
                                                  s5_b_im[l], s5_c_re[l], s5_c_im[l])
        e_parts = _s5_inject_call(z, s5_block0, w_end, s5_steps)
        h_parts = _s5_scan_call(e_parts, lam_r, n_batch, n_ctx // S5_CHUNK)
        y_s5 = _s5_read_call(z, s5_block0, h_parts, k_oct, m_read, s5_d[l], s5_steps)
        ys = _glu_call(y_s5, w_glu_b, l, s5_b_glu[l])

        merged = _merge_call(um, (ya, yb, yr, ys), w_gate_b, b_gate[l], w_branch_b, l)
        y = _mm_call(merged, w_out_b, l, f32, name="out_proj")
        xs, um = _ln_call(stream, y, mod_vec(l, 2), ln_g[l, 0], ln_b[l, 0], (mod_vec(l, 4), mod_vec(l, 3)), geom, alpha)

        q = _mm_call(um, w_q_b, l, bf16, name="peer_query")
        n_heads = peer_sub_keys.shape[1]
        keys = peer_sub_keys[l].astype(bf16).reshape(2 * n_heads, PEER_KEYS, -1)
        s0, s1, e0, e1, th = _peer_route_call(q, keys)
        f = _peer_dense_call(um, u_tab_b, v_tab_b, l, s0, e0, s1, e1, th)
        if l == depth - 1:
            return _ln_call((xs,), f, mod_vec(l, 5), ln_g[l, 1], ln_b[l, 1], None, geom, alpha)[0]
        xs, um = _ln_call((xs,), f, mod_vec(l, 5), ln_g[l, 1], ln_b[l, 1],
                          (mod_vec(l + 1, 1), mod_vec(l + 1, 0)), geom, alpha)
        stream = (xs,)
```

```python
import functools
import math

import jax
import jax.numpy as jnp
from jax import lax
from jax.experimental import pallas as pl
from jax.experimental.pallas import tpu as pltpu

f32 = jnp.float32
bf16 = jnp.bfloat16

HEAD_DIM = 128
GRID_W = 64
ROPE_THETA = 10000.0
WINDOW = 128
Q_TILE = 128
KV_GROUP = 4
LRU_C = 8.0
LRU_BLOCKS = 8
S5_GROUP = 16
S5_STATE = 64
S5_CHUNK = 8
S5_OCT = 8
PEER_TOPK = 16
PEER_KEYS = 128
LN_EPS = 1e-6
NEG = -1e30
ROW_TILE = 256
MM_TILE_M = 512
V7X_VMEM_LIMIT = 56 * 1024 * 1024


def _params(*sem, vmem=V7X_VMEM_LIMIT):
    return pltpu.CompilerParams(dimension_semantics=sem, vmem_limit_bytes=vmem)


def _dot(a, b):
    return jnp.dot(a, b, preferred_element_type=f32)


def _dot_nt(a, b):
    return lax.dot_general(a, b, (((1,), (1,)), ((), ())), preferred_element_type=f32)


def _gelu(x):
    return x * (0.5 * (1.0 + jnp.tanh(math.sqrt(2.0 / math.pi) * (x + 0.044715 * (x * x * x)))))


def _sigmoid(x):
    return 1.0 / (1.0 + jnp.exp(-x))


def _ada_kernel(c_ref, w_ref, b_ref, o_ref):
    c = c_ref[...]
    a = (c * _sigmoid(c)).astype(bf16)
    o_ref[0] = _dot(a, w_ref[0].astype(bf16)) + b_ref[0]


def _ada_call(cin, w_ada, b_ada, tn=512):
    depth, d, n = w_ada.shape
    return pl.pallas_call(
        _ada_kernel,
        grid=(depth, n // tn),
        in_specs=[pl.BlockSpec((8, d), lambda l, j: (0, 0)),
                  pl.BlockSpec((1, d, tn), lambda l, j: (l, 0, j)),
                  pl.BlockSpec((1, 1, tn), lambda l, j: (l, 0, j))],
        out_specs=pl.BlockSpec((1, 8, tn), lambda l, j: (l, 0, j)),
        out_shape=jax.ShapeDtypeStruct((depth, 8, n), f32),
        compiler_params=_params("parallel", "parallel"),
        name="adaln",
    )(cin, w_ada, b_ada.reshape(depth, 1, n))


def _row_group(i, tiles_per_batch, ctx_tiles, n_batch):
    return jnp.where(i % tiles_per_batch < ctx_tiles, n_batch, i // tiles_per_batch)


def _stream_specs(stream, geom, d):
    tpb = geom[0]
    if len(stream) == 1:
        return [pl.BlockSpec((ROW_TILE, d), lambda i: (i, 0))]
    return [pl.BlockSpec((1, ROW_TILE, d), lambda i: (i // tpb, jnp.maximum(i % tpb - 1, 0), 0)),
            pl.BlockSpec((1, ROW_TILE, d), lambda i: (i // tpb, 0, 0))]


def _read_stream(refs, is_ctx):
    if len(refs) == 1:
        return refs[0][...]
    return jnp.where(is_ctx, refs[1][0], refs[0][0])


def _mod_kernel(*refs, n_stream, tiles_per_batch):
    sc_ref, sh_ref, o_ref = refs[n_stream:]
    x = _read_stream(refs[:n_stream], pl.program_id(0) % tiles_per_batch == 0)
    o_ref[...] = (x * (1.0 + sc_ref[0]) + sh_ref[0]).astype(o_ref.dtype)


def _mod_call(stream, n, sc, sh, geom):
    d = sc.shape[-1]
    tpb, ct, nb = geom
    sel = lambda i: (_row_group(i, tpb, ct, nb), 0, 0)
    return pl.pallas_call(
        functools.partial(_mod_kernel, n_stream=len(stream), tiles_per_batch=tpb),
        grid=(n // ROW_TILE,),
        in_specs=_stream_specs(stream, geom, d) + [pl.BlockSpec((1, 1, d), sel), pl.BlockSpec((1, 1, d), sel)],
        out_specs=pl.BlockSpec((ROW_TILE, d), lambda i: (i, 0)),
        out_shape=jax.ShapeDtypeStruct((n, d), bf16),
        compiler_params=_params("parallel"),
        name="modulate",
    )(*stream, sc, sh)


def _ln_kernel(*refs, n_stream, tiles_per_batch, alpha, last):
    y_ref, g_ref, lg_ref, lb_ref = refs[n_stream:n_stream + 4]
    is_ctx = pl.program_id(0) % tiles_per_batch == 0
    v = alpha * _read_stream(refs[:n_stream], is_ctx) + g_ref[0] * y_ref[...]
    mu = jnp.mean(v, axis=-1, keepdims=True)
    vc = v - mu
    var = jnp.mean(vc * vc, axis=-1, keepdims=True)
    o = vc * lax.rsqrt(var + LN_EPS) * lg_ref[...] + lb_ref[...]
    if last:
        xo_ref, = refs[n_stream + 4:]

        @pl.when(jnp.logical_not(is_ctx))
        def _():
            xo_ref[0] = o
    else:
        sc_ref, sh_ref, xo_ref, xm_ref = refs[n_stream + 4:]
        xo_ref[...] = o
        xm_ref[...] = (o * (1.0 + sc_ref[0]) + sh_ref[0]).astype(xm_ref.dtype)


def _ln_call(stream, y, gate, ln_g, ln_b, nxt_mod, geom, alpha):
    n, d = y.shape
    tpb, ct, nb = geom
    sel = lambda i: (_row_group(i, tpb, ct, nb), 0, 0)
    row = pl.BlockSpec((ROW_TILE, d), lambda i: (i, 0))
    vec = pl.BlockSpec((1, d), lambda i: (0, 0))
    mod = pl.BlockSpec((1, 1, d), sel)
    last = nxt_mod is None
    in_specs = _stream_specs(stream, geom, d) + [row, mod, vec, vec]
    args = list(stream) + [y, gate, ln_g.reshape(1, d), ln_b.reshape(1, d)]
    if last:
        out_specs = [pl.BlockSpec((1, ROW_TILE, d), lambda i: (i // tpb, jnp.maximum(i % tpb - 1, 0), 0))]
        out_shape = [jax.ShapeDtypeStruct((nb, n // nb - ct * ROW_TILE, d), f32)]
    else:
        in_specs += [mod, mod]
        args += list(nxt_mod)
        out_specs = [row, row]
        out_shape = [jax.ShapeDtypeStruct((n, d), f32), jax.ShapeDtypeStruct((n, d), bf16)]
    return pl.pallas_call(
        functools.partial(_ln_kernel, n_stream=len(stream), tiles_per_batch=tpb, alpha=alpha, last=last),
        grid=(n // ROW_TILE,),
        in_specs=in_specs, out_specs=out_specs, out_shape=out_shape,
        compiler_params=_params("arbitrary" if last else "parallel"),
        name="residual_ln",
    )(*args)


def _mm_kernel(a_ref, b_ref, o_ref):
    o_ref[...] = _dot(a_ref[...], b_ref[...]).astype(o_ref.dtype)


def _mm_call(a, b, layer, out_dtype, tn=512, tm=MM_TILE_M, name="matmul"):
    m, k = a.shape
    n = b.shape[2]
    return pl.pallas_call(
        _mm_kernel,
        grid=(m // tm, n // tn),
        in_specs=[pl.BlockSpec((tm, k), lambda i, j: (i, 0)),
                  pl.BlockSpec((None, k, tn), lambda i, j: (layer, 0, j))],
        out_specs=pl.BlockSpec((tm, tn), lambda i, j: (i, j)),
        out_shape=jax.ShapeDtypeStruct((m, n), out_dtype),
        compiler_params=_params("parallel", "arbitrary"),
        name=name,
    )(a, b)


def _glu_kernel(a_ref, wv_ref, wg_ref, bv_ref, bg_ref, o_ref):
    a = a_ref[...].astype(bf16)
    val = _dot(a, wv_ref[...]) + bv_ref[...]
    gate = _dot(a, wg_ref[...]) + bg_ref[...]
    o_ref[...] = (val * _sigmoid(gate)).astype(o_ref.dtype)


def _glu_call(a, w, layer, b, tn=512, tm=MM_TILE_M):
    m, k = a.shape
    n = w.shape[2] // 2
    nj = n // tn
    b2 = b.reshape(1, 2 * n)
    return pl.pallas_call(
        _glu_kernel,
        grid=(m // tm, nj),
        in_specs=[pl.BlockSpec((tm, k), lambda i, j: (i, 0)),
                  pl.BlockSpec((None, k, tn), lambda i, j: (layer, 0, j)),
                  pl.BlockSpec((None, k, tn), lambda i, j: (layer, 0, j + nj)),
                  pl.BlockSpec((1, tn), lambda i, j: (0, j)),
                  pl.BlockSpec((1, tn), lambda i, j: (0, j + nj))],
        out_specs=pl.BlockSpec((tm, tn), lambda i, j: (i, j)),
        out_shape=jax.ShapeDtypeStruct((m, n), bf16),
        compiler_params=_params("parallel", "arbitrary"),
        name="s5_glu",
    )(a, w, w, b2, b2)


def _merge_kernel(u_ref, ya_ref, yb_ref, yr_ref, ys_ref, wg_ref, bg_ref, wb_ref, o_ref):
    u = u_ref[...]
    acc = None
    for i, y_ref in enumerate((ya_ref, yb_ref, yr_ref, ys_ref)):
        gate = _sigmoid(_dot(u, wg_ref[i]) + bg_ref[i])
        term = gate * _dot(y_ref[...], wb_ref[i])
        acc = term if acc is None else acc + term
    o_ref[...] = acc.astype(o_ref.dtype)


def _merge_call(u, branches, w_gate, b_gate, w_branch, layer, tn=256, tm=MM_TILE_M):
    m, d = u.shape
    _, nbr, bw, n = w_branch.shape
    ybr = pl.BlockSpec((tm, bw), lambda i, j: (i, 0))
    return pl.pallas_call(
        _merge_kernel,
        grid=(m // tm, n // tn),
        in_specs=[pl.BlockSpec((tm, d), lambda i, j: (i, 0)), ybr, ybr, ybr, ybr,
                  pl.BlockSpec((None, nbr, d, tn), lambda i, j: (layer, 0, 0, j)),
                  pl.BlockSpec((nbr, 1, tn), lambda i, j: (0, 0, j)),
                  pl.BlockSpec((None, nbr, bw, tn), lambda i, j: (layer, 0, 0, j))],
        out_specs=pl.BlockSpec((tm, tn), lambda i, j: (i, j)),
        out_shape=jax.ShapeDtypeStruct((m, n), bf16),
        compiler_params=_params("parallel", "arbitrary"),
        name="branch_merge",
    )(u, *branches, w_gate, b_gate.reshape(nbr, 1, n), w_branch)


def _rope_tables(n_lat, n_ctx):
    t = jnp.arange(n_lat)
    row = (t // GRID_W).astype(f32)
    col = (t % GRID_W).astype(f32)
    n_freq = HEAD_DIM // 4
    inv = ROPE_THETA ** (-jnp.arange(n_freq, dtype=f32) / n_freq)
    ang = jnp.concatenate([row[:, None] * inv, col[:, None] * inv], axis=-1)
    cos = jnp.repeat(jnp.cos(ang), 2, axis=-1)
    sin = jnp.repeat(jnp.sin(ang), 2, axis=-1) * jnp.tile(jnp.array([-1.0, 1.0], f32), HEAD_DIM // 2)
    cos = jnp.concatenate([jnp.ones((n_ctx, HEAD_DIM), f32), cos], axis=0)
    sin = jnp.concatenate([jnp.zeros((n_ctx, HEAD_DIM), f32), sin], axis=0)
    return cos, sin


N_QKV_SLICES = 24
A_Q, A_K, A_V, B_Q, B_K, B_V = 0, 8, 10, 12, 20, 22


def _prep_kernel(z_ref, cos_ref, sin_ref, gq_ref, gk_ref, o_ref):
    cos, sin = cos_ref[...], sin_ref[...]
    even = lax.broadcasted_iota(jnp.int32, cos.shape, 1) % 2 == 0
    for s in range(N_QKV_SLICES):
        y = z_ref[:, s * HEAD_DIM:(s + 1) * HEAD_DIM]
        if s < A_V:
            gain = gq_ref[...] if s < A_K else gk_ref[...]
            y = y * lax.rsqrt(jnp.mean(y * y, axis=-1, keepdims=True) + LN_EPS) * gain
        if s < A_V or B_Q <= s < B_V:
            partner = jnp.where(even, pltpu.roll(y, HEAD_DIM - 1, axis=1), pltpu.roll(y, 1, axis=1))
            y = y * cos + partner * sin
        if s < A_K or B_Q <= s < B_K:
            y = y * (HEAD_DIM ** -0.5)
        o_ref[0, s] = y.astype(o_ref.dtype)


def _prep_call(z, cos, sin, gain_q, gain_k, n_batch, seq_tot):
    n = z.shape[0]
    tr = ROW_TILE
    tpb = seq_tot // tr
    width = N_QKV_SLICES * HEAD_DIM
    vec = pl.BlockSpec((1, HEAD_DIM), lambda i: (0, 0))
    return pl.pallas_call(
        _prep_kernel,
        grid=(n // tr,),
        in_specs=[pl.BlockSpec((tr, width), lambda i: (i, 0)),
                  pl.BlockSpec((tr, HEAD_DIM), lambda i: (i % tpb, 0)),
                  pl.BlockSpec((tr, HEAD_DIM), lambda i: (i % tpb, 0)), vec, vec],
        out_specs=pl.BlockSpec((1, N_QKV_SLICES, tr, HEAD_DIM), lambda i: (i // tpb, 0, i % tpb, 0)),
        out_shape=jax.ShapeDtypeStruct((n_batch, N_QKV_SLICES, seq_tot, HEAD_DIM), bf16),
        compiler_params=_params("parallel"),
        name="qkv_prep",
    )(z, cos, sin, gain_q.reshape(1, HEAD_DIM), gain_k.reshape(1, HEAD_DIM))


def _attn_a_kernel(q_ref, k_ref, v_ref, o_ref, m_s, acc_s, *, n_ctx, tk, n_lat_chunks):
    q = q_ref[0].reshape(KV_GROUP * Q_TILE, HEAD_DIM)

    def values(lo, n):
        return jnp.concatenate([v_ref[0, 0, lo:lo + n, :], jnp.ones((n, HEAD_DIM), bf16)], axis=1)

    s = _dot_nt(q, k_ref[0, 0, 0:n_ctx, :])
    m0 = jnp.max(s, axis=-1, keepdims=True)
    m_s[...] = m0
    acc_s[...] = _dot(jnp.exp((s - m0).astype(bf16)), values(0, n_ctx))

    @pl.when(pl.program_id(2) >= n_ctx // Q_TILE)
    def _():
        for c in range(n_lat_chunks):
            lo = n_ctx + c * tk
            s = _dot_nt(q, k_ref[0, 0, lo:lo + tk, :])
            m_prev = m_s[...]
            m_new = jnp.maximum(m_prev, jnp.max(s, axis=-1, keepdims=True))
            p = jnp.exp((s - m_new).astype(bf16))
            acc_s[...] = jnp.exp(m_prev - m_new) * acc_s[...] + _dot(p, values(lo, tk))
            m_s[...] = m_new

    out = acc_s[:, 0:HEAD_DIM] / acc_s[:, HEAD_DIM:2 * HEAD_DIM]
    for g in range(KV_GROUP):
        o_ref[0, :, g * HEAD_DIM:(g + 1) * HEAD_DIM] = out[g * Q_TILE:(g + 1) * Q_TILE].astype(o_ref.dtype)


def _attn_a_call(qkv, n_ctx, tk=1024):
    nb, _, seq_tot, _ = qkv.shape
    n_kv = (A_V - A_K)
    rows = KV_GROUP * Q_TILE
    n_lat = seq_tot - n_ctx
    tk = min(tk, n_lat)
    assert n_lat % tk == 0
    kern = functools.partial(_attn_a_kernel, n_ctx=n_ctx, tk=tk, n_lat_chunks=n_lat // tk)
    return pl.pallas_call(
        kern,
        grid=(nb, n_kv, seq_tot // Q_TILE),
        in_specs=[pl.BlockSpec((1, KV_GROUP, Q_TILE, HEAD_DIM), lambda b, h, i: (b, A_Q // KV_GROUP + h, i, 0)),
                  pl.BlockSpec((1, 1, seq_tot, HEAD_DIM), lambda b, h, i: (b, A_K + h, 0, 0)),
                  pl.BlockSpec((1, 1, seq_tot, HEAD_DIM), lambda b, h, i: (b, A_V + h, 0, 0))],
        out_specs=pl.BlockSpec((1, Q_TILE, KV_GROUP * HEAD_DIM), lambda b, h, i: (b, i, h)),
        out_shape=jax.ShapeDtypeStruct((nb, seq_tot, n_kv * KV_GROUP * HEAD_DIM), bf16),
        scratch_shapes=[pltpu.VMEM((rows, 1), f32), pltpu.VMEM((rows, 2 * HEAD_DIM), f32)],
        compiler_params=_params("parallel", "parallel", "arbitrary"),
        name="dense_attention",
    )(qkv, qkv, qkv)


def _attn_b_kernel(q_ref, k_ref, v_ref, sink_ref, o_ref, *, n_ctx, n_blk):
    i = pl.program_id(2)
    blk = i - n_ctx // Q_TILE
    rows = KV_GROUP * Q_TILE
    q = q_ref[0].reshape(rows, HEAD_DIM)
    sink = sink_ref[0]
    kc = k_ref[0, 0, 0:n_ctx, :]
    vc = v_ref[0, 0, 0:n_ctx, :]
    s_ctx = _dot_nt(q, kc)
    m = jnp.maximum(jnp.max(s_ctx, axis=-1, keepdims=True), sink)
    q_pos = lax.broadcasted_iota(jnp.int32, (rows, Q_TILE), 0) % Q_TILE
    k_off = lax.broadcasted_iota(jnp.int32, (rows, Q_TILE), 1)
    s_loc, v_loc = [], []
    for d in (-1, 0, 1):
        j = blk + d
        valid = jnp.logical_and(blk >= 0, jnp.logical_and(j >= 0, j < n_blk))
        start = pl.multiple_of(n_ctx + jnp.clip(j, 0, n_blk - 1) * Q_TILE, Q_TILE)
        k = k_ref[0, 0, pl.ds(start, Q_TILE), :]
        v_loc.append(v_ref[0, 0, pl.ds(start, Q_TILE), :])
        keep = jnp.logical_and(jnp.abs(k_off + d * Q_TILE - q_pos) <= WINDOW, valid)
        s = jnp.where(keep, _dot_nt(q, k), NEG)
        m = jnp.maximum(m, jnp.max(s, axis=-1, keepdims=True))
        s_loc.append(s)
    ones = jnp.ones((Q_TILE, HEAD_DIM), bf16)
    acc = _dot(jnp.exp((s_ctx - m).astype(bf16)), jnp.concatenate([vc, jnp.ones((n_ctx, HEAD_DIM), bf16)], axis=1))
    for s, v in zip(s_loc, v_loc):
        acc = acc + _dot(jnp.exp((s - m).astype(bf16)), jnp.concatenate([v, ones], axis=1))
    out = acc[:, 0:HEAD_DIM] / (jnp.exp(sink - m) + acc[:, HEAD_DIM:2 * HEAD_DIM])
    for g in range(KV_GROUP):
        o_ref[0, :, g * HEAD_DIM:(g + 1) * HEAD_DIM] = out[g * Q_TILE:(g + 1) * Q_TILE].astype(o_ref.dtype)


def _attn_b_call(qkv, sink, n_ctx):
    nb, _, seq_tot, _ = qkv.shape
    n_kv = (B_V - B_K)
    rows = KV_GROUP * Q_TILE
    sink_rows = jnp.repeat(sink.astype(f32).reshape(n_kv, KV_GROUP), Q_TILE, axis=1).reshape(n_kv, rows, 1)
    kern = functools.partial(_attn_b_kernel, n_ctx=n_ctx, n_blk=(seq_tot - n_ctx) // Q_TILE)
    return pl.pallas_call(
        kern,
        grid=(nb, n_kv, seq_tot // Q_TILE),
        in_specs=[pl.BlockSpec((1, KV_GROUP, Q_TILE, HEAD_DIM), lambda b, h, i: (b, B_Q // KV_GROUP + h, i, 0)),
                  pl.BlockSpec((1, 1, seq_tot, HEAD_DIM), lambda b, h, i: (b, B_K + h, 0, 0)),
                  pl.BlockSpec((1, 1, seq_tot, HEAD_DIM), lambda b, h, i: (b, B_V + h, 0, 0)),
                  pl.BlockSpec((1, rows, 1), lambda b, h, i: (h, 0, 0))],
        out_specs=pl.BlockSpec((1, Q_TILE, KV_GROUP * HEAD_DIM), lambda b, h, i: (b, i, h)),
        out_shape=jax.ShapeDtypeStruct((nb, seq_tot, n_kv * KV_GROUP * HEAD_DIM), bf16),
        compiler_params=_params("parallel", "parallel", "arbitrary"),
        name="window_attention",
    )(qkv, qkv, qkv, sink_rows)


HALO = 8


def _lru_gate_kernel(x_ref, xp_ref, xn_ref, cw_ref, cb_ref, wr_ref, br_ref, wi_ref, bi_ref, sp_ref,
                     a_ref, b_ref, pad_s, *, tiles_per_batch):
    tr = x_ref.shape[0]
    t = pl.program_id(0) % tiles_per_batch
    has_prev = t >= 2
    has_next = jnp.logical_and(t >= 1, t < tiles_per_batch - 1)
    x = x_ref[...]
    pad_s[0:HALO, :] = jnp.where(has_prev, xp_ref[...], 0.0)
    pad_s[HALO:HALO + tr, :] = x
    pad_s[HALO + tr:2 * HALO + tr, :] = jnp.where(has_next, xn_ref[...], 0.0)
    cw = cw_ref[...]
    xl = cb_ref[...] + pad_s[HALO - 2:HALO - 2 + tr, :] * cw[0:1]
    xl = xl + pad_s[HALO - 1:HALO - 1 + tr, :] * cw[1:2]
    xl = xl + x * cw[2:3]
    xl = xl + pad_s[HALO + 1:HALO + 1 + tr, :] * cw[3:4]
    bw = xl.shape[1] // LRU_BLOCKS
    for n in range(LRU_BLOCKS):
        cols = slice(n * bw, (n + 1) * bw)
        xb = xl[:, cols]
        xb16 = xb.astype(bf16)
        for d in range(2):
            r = _sigmoid(_dot(xb16, wr_ref[d, n]) + br_ref[d:d + 1, cols])
            gi = _sigmoid(_dot(xb16, wi_ref[d, n]) + bi_ref[d:d + 1, cols])
            log_a = (-LRU_C) * r * sp_ref[d:d + 1, cols]
            a = jnp.exp(log_a)
            a_ref[d, :, cols] = a
            b_ref[d, :, cols] = jnp.sqrt(1.0 - a * a) * gi * xb


def _lru_gate_call(z, col_block, conv_w, conv_b, w_r, b_r, w_i, b_i, softplus_lam, tiles_per_batch):
    n = z.shape[0]
    w = conv_w.shape[1]
    tr = ROW_TILE
    hb = tr // HALO
    n_halo = n // HALO
    full = lambda shape: pl.BlockSpec(shape, lambda i: (0,) * len(shape))
    kern = functools.partial(_lru_gate_kernel, tiles_per_batch=tiles_per_batch)
    out = jax.ShapeDtypeStruct((2, n, w), f32)
    return pl.pallas_call(
        kern,
        grid=(n // tr,),
        in_specs=[pl.BlockSpec((tr, w), lambda i: (i, col_block)),
                  pl.BlockSpec((HALO, w), lambda i: (jnp.maximum(i * hb - 1, 0), col_block)),
                  pl.BlockSpec((HALO, w), lambda i: (jnp.minimum((i + 1) * hb, n_halo - 1), col_block)),
                  full(conv_w.shape), full((1, w)), full(w_r.shape), full(b_r.shape),
                  full(w_i.shape), full(b_i.shape), full(softplus_lam.shape)],
        out_specs=[pl.BlockSpec((2, tr, w), lambda i: (0, i, 0)), pl.BlockSpec((2, tr, w), lambda i: (0, i, 0))],
        out_shape=[out, out],
        scratch_shapes=[pltpu.VMEM((tr + 2 * HALO, w), f32)],
        compiler_params=_params("parallel"),
        name="lru_gates",
    )(z, z, z, conv_w, conv_b.reshape(1, w), w_r, b_r, w_i, b_i, softplus_lam)


def _lru_scan_kernel(af_ref, bf_ref, ar_ref, br_ref, hf_ref, hr_ref, sf, sr):
    tr = hf_ref.shape[0]

    @pl.when(pl.program_id(1) == 0)
    def _():
        sf[...] = jnp.zeros(sf.shape, f32)
        sr[...] = jnp.zeros(sr.shape, f32)

    def body(t, carry):
        hf, hr = carry
        hf = af_ref[0, pl.ds(t, 1), :] * hf + bf_ref[0, pl.ds(t, 1), :]
        hf_ref[pl.ds(t, 1), :] = hf
        u = tr - 1 - t
        hr = ar_ref[0, pl.ds(u, 1), :] * hr + br_ref[0, pl.ds(u, 1), :]
        hr_ref[pl.ds(u, 1), :] = hr
        return hf, hr

    hf, hr = lax.fori_loop(0, tr, body, (sf[...], sr[...]), unroll=8)
    sf[...] = hf
    sr[...] = hr


def _scan_tile_maps(tiles_per_batch):
    fwd = lambda b, i: b * tiles_per_batch + i
    rev = lambda b, i: b * tiles_per_batch + jnp.where(i == 0, 0, tiles_per_batch - i)
    return fwd, rev


def _lru_scan_call(a, b, n_batch, tiles_per_batch):
    _, n, w = a.shape
    tr = ROW_TILE
    fwd, rev = _scan_tile_maps(tiles_per_batch)
    out = jax.ShapeDtypeStruct((n, w), f32)
    return pl.pallas_call(
        _lru_scan_kernel,
        grid=(n_batch, tiles_per_batch),
        in_specs=[pl.BlockSpec((1, tr, w), lambda bb, i: (0, fwd(bb, i), 0)),
                  pl.BlockSpec((1, tr, w), lambda bb, i: (0, fwd(bb, i), 0)),
                  pl.BlockSpec((1, tr, w), lambda bb, i: (1, rev(bb, i), 0)),
                  pl.BlockSpec((1, tr, w), lambda bb, i: (1, rev(bb, i), 0))],
        out_specs=[pl.BlockSpec((tr, w), lambda bb, i: (fwd(bb, i), 0)),
                   pl.BlockSpec((tr, w), lambda bb, i: (rev(bb, i), 0))],
        out_shape=[out, out],
        scratch_shapes=[pltpu.VMEM((1, w), f32), pltpu.VMEM((1, w), f32)],
        compiler_params=_params("parallel", "arbitrary"),
        name="lru_scan",
    )(a, b, a, b)


def _lru_out_kernel(hf_ref, hr_ref, g_ref, o_ref):
    o_ref[...] = ((hf_ref[...] + hr_ref[...]) * _gelu(g_ref[...])).astype(o_ref.dtype)


def _lru_out_call(hf, hr, z, gate_col_block):
    n, w = hf.shape
    tr = ROW_TILE
    row = pl.BlockSpec((tr, w), lambda i: (i, 0))
    return pl.pallas_call(
        _lru_out_kernel,
        grid=(n // tr,),
        in_specs=[row, row, pl.BlockSpec((tr, w), lambda i: (i, gate_col_block))],
        out_specs=row,
        out_shape=jax.ShapeDtypeStruct((n, w), bf16),
        compiler_params=_params("parallel"),
        name="lru_out",
    )(hf, hr, z)


def _s5_weights(a_re, a_im, log_step, b_re, b_im, c_re, c_im):
    r = S5_CHUNK
    n_g = a_re.shape[1]
    n_oct = n_g // S5_OCT
    step = jnp.exp(log_step)[..., None]
    mag = jnp.exp(a_re * step)
    lb_re, lb_im = mag * jnp.cos(a_im * step), mag * jnp.sin(a_im * step)
    den = a_re * a_re + a_im * a_im
    num_re = lb_re - 1.0
    coef_re = (num_re * a_re + lb_im * a_im) / den
    coef_im = (lb_im * a_re - num_re * a_im) / den
    bb_re = coef_re[..., None] * b_re - coef_im[..., None] * b_im
    bb_im = coef_re[..., None] * b_im + coef_im[..., None] * b_re
    tau = jnp.arange(r + 1, dtype=f32)[:, None, None, None]
    pmag = jnp.exp(tau * (a_re * step))
    pw_re, pw_im = pmag * jnp.cos(tau * (a_im * step)), pmag * jnp.sin(tau * (a_im * step))

    lb_b_re = pw_re[..., None] * bb_re - pw_im[..., None] * bb_im
    lb_b_im = pw_re[..., None] * bb_im + pw_im[..., None] * bb_re
    lag = (jnp.einsum('tdgpc,dgop->tdgco', lb_b_re, c_re) - jnp.einsum('tdgpc,dgop->tdgco', lb_b_im, c_im))
    idx = jnp.arange(r)
    sh_f = (idx[None, None, :] - idx[None, :, None] == jnp.arange(r + 1)[:, None, None]).astype(f32)
    sh_r = (idx[None, :, None] - idx[None, None, :] == jnp.arange(r + 1)[:, None, None]).astype(f32)
    k_loc = jnp.einsum('tio,tgcd->gicod', sh_f, lag[:, 0]) + jnp.einsum('tio,tgcd->gicod', sh_r, lag[:, 1])
    k_cmp = k_loc.reshape(n_oct, S5_OCT, r, S5_GROUP, r * S5_GROUP)
    k_cmp = k_cmp.transpose(0, 2, 1, 3, 4).reshape(n_oct, r * 128, r * S5_GROUP)

    def inject(d, powers):
        w_re = pw_re[powers, d][..., None] * bb_re[d] - pw_im[powers, d][..., None] * bb_im[d]
        w_im = pw_re[powers, d][..., None] * bb_im[d] + pw_im[powers, d][..., None] * bb_re[d]
        out = []
        for w in (w_re, w_im):
            w = w.reshape(r, n_oct, S5_OCT, S5_STATE, S5_GROUP).transpose(1, 0, 2, 4, 3)
            out.append(w.reshape(n_oct, r * 128, S5_STATE))
        return out
    w_cmp = jnp.concatenate(inject(0, idx[::-1]) + inject(1, idx), axis=-1)

    def readout(d, powers):
        cl_re = c_re[d][None] * pw_re[powers, d][:, :, None, :] - c_im[d][None] * pw_im[powers, d][:, :, None, :]
        cl_im = c_re[d][None] * pw_im[powers, d][:, :, None, :] + c_im[d][None] * pw_re[powers, d][:, :, None, :]
        out = []
        for w in (cl_re, -cl_im):
            w = w.reshape(r, n_oct, S5_OCT, S5_GROUP, S5_STATE).transpose(1, 2, 4, 0, 3)
            out.append(w.reshape(n_oct, S5_OCT * S5_STATE, r * S5_GROUP))
        return out
    m_cmp = jnp.concatenate(readout(0, idx + 1) + readout(1, r - idx), axis=1)

    lam_r = jnp.stack([pw_re[r, 0].reshape(-1), pw_im[r, 0].reshape(-1),
                       pw_re[r, 1].reshape(-1), pw_im[r, 1].reshape(-1)])
    return k_cmp.astype(bf16), w_cmp.astype(bf16), m_cmp.astype(bf16), lam_r


def _expand_octet(compact, inner, row_inner):
    n_rows, n_cols = compact.shape
    wide = n_cols * S5_OCT
    log_inner, log_row = inner.bit_length() - 1, row_inner.bit_length() - 1
    oct_bits = S5_OCT.bit_length() - 1
    src = lax.broadcasted_iota(jnp.int32, (n_cols, wide), 0)
    dst = lax.broadcasted_iota(jnp.int32, (n_cols, wide), 1)
    dst_compact = ((dst >> (log_inner + oct_bits)) << log_inner) + (dst & (inner - 1))
    spread = _dot(compact, (src == dst_compact).astype(compact.dtype))
    row_g = (lax.broadcasted_iota(jnp.int32, (n_rows, wide), 0) >> log_row) & (S5_OCT - 1)
    col_g = (lax.broadcasted_iota(jnp.int32, (n_rows, wide), 1) >> log_inner) & (S5_OCT - 1)
    return jnp.where(row_g == col_g, spread, 0.0).astype(compact.dtype)


def _chunk_rows(u_ref):
    n_rows = u_ref.shape[0] // S5_CHUNK
    return [u_ref[pl.ds(i, n_rows, stride=S5_CHUNK), :] for i in range(S5_CHUNK)]


def _s5_inject_kernel(u_ref, w_ref, efr_ref, efi_ref, err_ref, eri_ref, w_s):
    @pl.when(pl.program_id(1) == 0)
    def _():
        w_s[...] = _expand_octet(w_ref[0], S5_STATE, S5_GROUP)

    x = jnp.concatenate(_chunk_rows(u_ref), axis=1).astype(bf16)
    e = _dot(x, w_s[...])
    w = efr_ref.shape[1]
    for k, o_ref in enumerate((efr_ref, efi_ref, err_ref, eri_ref)):
        o_ref[...] = e[:, k * w:(k + 1) * w]


def _s5_inject_call(z, col_block0, w_cmp, steps):
    n = z.shape[0]
    n_oct, n_in, n_cmp = w_cmp.shape
    sw = S5_OCT * S5_STATE
    rows = steps // S5_CHUNK
    out = jax.ShapeDtypeStruct((n // S5_CHUNK, n_oct * sw), f32)
    ospec = pl.BlockSpec((rows, sw), lambda k, i: (i, k))
    return pl.pallas_call(
        _s5_inject_kernel,
        grid=(n_oct, n // steps),
        in_specs=[pl.BlockSpec((steps, 128), lambda k, i: (i, col_block0 + k)),
                  pl.BlockSpec((1, n_in, n_cmp), lambda k, i: (k, 0, 0))],
        out_specs=[ospec] * 4,
        out_shape=[out] * 4,
        scratch_shapes=[pltpu.VMEM((n_in, n_cmp * S5_OCT), bf16)],
        compiler_params=_params("arbitrary", "arbitrary"),
        name="s5_inject",
    )(z, w_cmp)


def _s5_scan_kernel(efr_ref, efi_ref, err_ref, eri_ref, lam_ref, hfr_ref, hfi_ref, hrr_ref, hri_ref, *, n_ctx):
    n_rows = efr_ref.shape[0]
    lfr, lfi, lrr, lri = (lam_ref[k:k + 1, :] for k in range(4))
    zero = jnp.zeros((1, efr_ref.shape[1]), f32)

    def body(t, carry):
        fr, fi, rr, ri = carry
        hfr_ref[pl.ds(t, 1), :] = fr
        hfi_ref[pl.ds(t, 1), :] = fi
        er, ei = efr_ref[pl.ds(t, 1), :], efi_ref[pl.ds(t, 1), :]
        fr, fi = lfr * fr - lfi * fi + er, lfr * fi + lfi * fr + ei
        u = jnp.where(t < n_ctx, n_ctx - 1 - t, n_rows - 1 - (t - n_ctx))
        hrr_ref[pl.ds(u, 1), :] = rr
        hri_ref[pl.ds(u, 1), :] = ri
        er, ei = err_ref[pl.ds(u, 1), :], eri_ref[pl.ds(u, 1), :]
        rr, ri = lrr * rr - lri * ri + er, lrr * ri + lri * rr + ei
        return fr, fi, rr, ri

    lax.fori_loop(0, n_rows, body, (zero, zero, zero, zero), unroll=4)


def _s5_scan_call(e_parts, lam_r, n_batch, n_ctx_rows, lane_block=512):
    n_rows_tot, width = e_parts[0].shape
    rows = n_rows_tot // n_batch
    blk = pl.BlockSpec((rows, lane_block), lambda b, j: (b, j))
    out = jax.ShapeDtypeStruct((n_rows_tot, width), f32)
    return pl.pallas_call(
        functools.partial(_s5_scan_kernel, n_ctx=n_ctx_rows),
        grid=(n_batch, width // lane_block),
        in_specs=[blk] * 4 + [pl.BlockSpec((4, lane_block), lambda b, j: (0, j))],
        out_specs=[blk] * 4,
        out_shape=[out] * 4,
        compiler_params=_params("parallel", "parallel"),
        name="s5_scan",
    )(*e_parts, lam_r)


def _s5_read_kernel(u_ref, hfr_ref, hfi_ref, hrr_ref, hri_ref, k_ref, m_ref, d_ref, o_ref, k_s, m_s):
    @pl.when(pl.program_id(1) == 0)
    def _():
        k_s[...] = _expand_octet(k_ref[0], S5_GROUP, S5_GROUP)
        m_s[...] = _expand_octet(m_ref[0], S5_GROUP, S5_STATE)

    parts = _chunk_rows(u_ref)
    x = jnp.concatenate(parts, axis=1).astype(bf16)
    h = jnp.concatenate([hfr_ref[...], hfi_ref[...], hrr_ref[...], hri_ref[...]], axis=1).astype(bf16)
    y = _dot(x, k_s[...]) + _dot(h, m_s[...])
    n_rows = u_ref.shape[0] // S5_CHUNK
    for i in range(S5_CHUNK):
        o_ref[pl.ds(i, n_rows, stride=S5_CHUNK), :] = y[:, i * 128:(i + 1) * 128] + parts[i] * d_ref[...]


def _s5_read_call(z, col_block0, h_parts, k_cmp, m_cmp, d_skip, steps):
    n = z.shape[0]
    n_oct = k_cmp.shape[0]
    sw = S5_OCT * S5_STATE
    rows = steps // S5_CHUNK
    hspec = pl.BlockSpec((rows, sw), lambda k, i: (i, k))
    return pl.pallas_call(
        _s5_read_kernel,
        grid=(n_oct, n // steps),
        in_specs=[pl.BlockSpec((steps, 128), lambda k, i: (i, col_block0 + k))] + [hspec] * 4 +
                 [pl.BlockSpec((1,) + k_cmp.shape[1:], lambda k, i: (k, 0, 0)),
                  pl.BlockSpec((1,) + m_cmp.shape[1:], lambda k, i: (k, 0, 0)),
                  pl.BlockSpec((1, 128), lambda k, i: (0, k))],
        out_specs=pl.BlockSpec((steps, 128), lambda k, i: (i, k)),
        out_shape=jax.ShapeDtypeStruct((n, n_oct * 128), f32),
        scratch_shapes=[pltpu.VMEM((k_cmp.shape[1], k_cmp.shape[2] * S5_OCT), bf16),
                        pltpu.VMEM((m_cmp.shape[1], m_cmp.shape[2] * S5_OCT), bf16)],
        compiler_params=_params("arbitrary", "arbitrary"),
        name="s5_readout",
    )(z, *h_parts, k_cmp, m_cmp, d_skip.reshape(1, -1))


def _top_values(s, k):
    vals = []
    for _ in range(k):
        m = jnp.max(s, axis=0, keepdims=True)
        vals.append(m)
        s = jnp.where(s == m, -jnp.inf, s)
    return vals


def _merge_sort_pairs(n):
    pairs = []

    def merge(lo, hi, r):
        step = r * 2
        if step < hi - lo:
            merge(lo, hi, step)
            merge(lo + r, hi, step)
            pairs.extend((i, i + r) for i in range(lo + r, hi - r, step))
        else:
            pairs.append((lo, lo + r))

    def sort(lo, hi):
        if hi - lo >= 1:
            mid = lo + (hi - lo) // 2
            sort(lo, mid)
            sort(mid + 1, hi)
            merge(lo, hi, 1)

    sort(0, n - 1)
    return pairs


def _top_values_sorted(s, k):
    n_grp = s.shape[0] // 8
    lists = [s[8 * v:8 * v + 8, :] for v in range(n_grp)]
    for i, j in _merge_sort_pairs(n_grp):
        lists[i], lists[j] = jnp.maximum(lists[i], lists[j]), jnp.minimum(lists[i], lists[j])
    lists.append(jnp.full_like(lists[0], -jnp.inf))
    vals = []
    for t in range(k):
        m = jnp.max(lists[0], axis=0, keepdims=True)
        vals.append(m)
        hit = lists[0] == m
        for v in range(min(k - t, n_grp)):
            lists[v] = jnp.where(hit, lists[v + 1], lists[v])
    return vals


def _peer_route_kernel(q_ref, keys_ref, s0_ref, s1_ref, e0_ref, e1_ref, th_ref):
    k = PEER_TOPK
    s0 = _dot_nt(keys_ref[0], q_ref[:, 0:PEER_KEYS])
    s1 = _dot_nt(keys_ref[1], q_ref[:, PEER_KEYS:2 * PEER_KEYS])
    top0, top1 = _top_values_sorted(s0, k + 1), _top_values_sorted(s1, k + 1)
    pad = [jnp.full_like(top0[0], -jnp.inf)] * 7
    t0, t1 = jnp.concatenate(top0 + pad, axis=0), jnp.concatenate(top1 + pad, axis=0)
    cand = jnp.concatenate([top0[0] + t1] + [top0[i] + t1[0:8] for i in range(1, 8)] + [t0[8:24] + top1[0]], axis=0)
    best = _top_values(cand, k + 1)
    z = None
    for v in best[:k]:
        e = jnp.exp(v - best[0])
        z = e if z is None else z + e
    s0_ref[0] = s0.reshape(s0_ref.shape[1:])
    s1_ref[0] = s1
    e0_ref[0] = (jnp.exp(s0 - top0[0]) / z).reshape(e0_ref.shape[1:])
    e1_ref[0] = jnp.exp(s1 - top1[0])
    th_ref[0] = 0.5 * (best[k - 1] + best[k])


def _peer_route_call(q, keys, tt=256):
    n = q.shape[0]
    n_heads = keys.shape[0] // 2
    big = jax.ShapeDtypeStruct((n_heads, PEER_KEYS, n), f32)
    bspec = pl.BlockSpec((1, PEER_KEYS, tt), lambda t, h: (h, 0, t))
    grp = jax.ShapeDtypeStruct((n_heads, PEER_KEYS // 8, 8, n), f32)
    gspec = pl.BlockSpec((1, PEER_KEYS // 8, 8, tt), lambda t, h: (h, 0, 0, t))
    return pl.pallas_call(
        _peer_route_kernel,
        grid=(n // tt, n_heads),
        in_specs=[pl.BlockSpec((tt, 2 * PEER_KEYS), lambda t, h: (t, h)),
                  pl.BlockSpec((2, PEER_KEYS, PEER_KEYS), lambda t, h: (h, 0, 0))],
        out_specs=[gspec, bspec, gspec, bspec, pl.BlockSpec((1, 1, tt), lambda t, h: (h, 0, t))],
        out_shape=[grp, big, grp, big, jax.ShapeDtypeStruct((n_heads, 1, n), f32)],
        compiler_params=_params("parallel", "arbitrary"),
        name="peer_route",
    )(q, keys)


PEER_I_PER_TILE = 4


def _peer_dense_kernel(xm_ref, u_ref, v_ref, s0_ref, e0_ref, s1_ref, e1_ref, th_ref, o_ref, s_s, w_s):
    e = pl.program_id(1)
    n_steps = pl.num_programs(1)

    @pl.when(e == 0)
    def _():
        o_ref[...] = jnp.zeros(o_ref.shape, f32)

    @pl.when(e > 0)
    def _():
        n_heads = s1_ref.shape[0]
        sub = ((e - 1) % (8 // PEER_I_PER_TILE)) * PEER_I_PER_TILE
        for il in range(PEER_I_PER_TILE):
            w = None
            for h in range(n_heads):
                s0 = s0_ref[h, 0, pl.ds(sub + il, 1), :]
                e0 = e0_ref[h, 0, pl.ds(sub + il, 1), :]
                keep = s1_ref[h] >= th_ref[h] - s0
                term = jnp.where(keep, e0 * e1_ref[h], 0.0)
                w = term if w is None else w + term
            w_s[il * PEER_KEYS:(il + 1) * PEER_KEYS, :] = w
        act = (_gelu(s_s[...]) * w_s[...]).T.astype(bf16)
        o_ref[...] += _dot(act, v_ref[...])

    @pl.when(e < n_steps - 1)
    def _():
        s_s[...] = _dot_nt(u_ref[...], xm_ref[...])


def _peer_dense_call(xm, u_tab, v_tab, layer, s0, e0, s1, e1, th, tm=512):
    n, d = xm.shape
    n_exp = u_tab.shape[1]
    te = PEER_I_PER_TILE * PEER_KEYS
    n_tiles = n_exp // te
    n_heads = s1.shape[0]
    per8 = 8 // PEER_I_PER_TILE
    prev = lambda e: jnp.maximum(e - 1, 0)
    row8 = pl.BlockSpec((n_heads, 1, 8, tm), lambda t, e: (0, prev(e) // per8, 0, t))
    full = pl.BlockSpec((n_heads, PEER_KEYS, tm), lambda t, e: (0, 0, t))
    return pl.pallas_call(
        _peer_dense_kernel,
        grid=(n // tm, n_tiles + 1),
        in_specs=[pl.BlockSpec((tm, d), lambda t, e: (t, 0)),
                  pl.BlockSpec((None, te, d), lambda t, e: (layer, jnp.minimum(e, n_tiles - 1), 0)),
                  pl.BlockSpec((None, te, d), lambda t, e: (layer, prev(e), 0)),
                  row8, row8, full, full,
                  pl.BlockSpec((n_heads, 1, tm), lambda t, e: (0, 0, t))],
        out_specs=pl.BlockSpec((tm, d), lambda t, e: (t, 0)),
        out_shape=jax.ShapeDtypeStruct((n, d), f32),
        scratch_shapes=[pltpu.VMEM((te, tm), f32), pltpu.VMEM((te, tm), f32)],
        compiler_params=_params("parallel", "arbitrary", vmem=60 * 1024 * 1024),
        name="peer_dense",
    )(xm, u_tab, v_tab, s0, e0, s1, e1, th)


def kernel(x, c, ctx, c_ctx, w_ada, b_ada, w_in, a_q_norm, a_k_norm, b_sink, lru_conv_w, lru_conv_b, lru_w_r, lru_b_r, lru_w_i, lru_b_i, lru_lambda, s5_a_re, s5_a_im, s5_log_step, s5_b_re, s5_b_im, s5_c_re, s5_c_im, s5_d, s5_w_glu, s5_b_glu, w_gate, b_gate, w_branch, w_out, ln_g, ln_b, peer_w_q, peer_sub_keys, peer_u, peer_v):
    n_batch, n_lat, d = x.shape
    n_ctx = ctx.shape[1]
    depth = w_ada.shape[0]
    seq_tot = n_ctx + n_lat
    n_tok = n_batch * seq_tot
    mix_w = d // 4
    assert n_ctx % ROW_TILE == 0 and n_lat % ROW_TILE == 0 and n_tok % MM_TILE_M == 0
    assert n_ctx == ROW_TILE, "scan kernels treat row tile 0 of every batch element as the context"
    tiles_per_batch = seq_tot // ROW_TILE
    geom = (tiles_per_batch, n_ctx // ROW_TILE, n_batch)
    alpha = (2.0 * depth) ** 0.25
    s5_steps = seq_tot // 4
    assert seq_tot % 4 == 0 and s5_steps % (8 * S5_CHUNK) == 0 and n_ctx % S5_CHUNK == 0

    cin = jnp.zeros((8, d), f32).at[:n_batch].set(c).at[n_batch].set(c_ctx)
    mod = _ada_call(cin, w_ada, b_ada)

    def mod_vec(l, k):
        return mod[l, :, k * d:(k + 1) * d].reshape(8, 1, d)

    cos, sin = _rope_tables(n_lat, n_ctx)
    w_in_b, w_gate_b, w_branch_b, w_out_b = (w.astype(bf16) for w in (w_in, w_gate, w_branch, w_out))
    w_glu_b, w_q_b, u_tab_b, v_tab_b = (w.astype(bf16) for w in (s5_w_glu, peer_w_q, peer_u, peer_v))
    stream = (x, ctx)
    um = _mod_call(stream, n_tok, mod_vec(0, 1), mod_vec(0, 0), geom)

    qkv_cols = N_QKV_SLICES * HEAD_DIM
    for l in range(depth):
        z = _mm_call(um, w_in_b, l, f32, name="in_proj")
        qkv = _prep_call(z, cos, sin, a_q_norm[l], a_k_norm[l], n_batch, seq_tot)
        ya = _attn_a_call(qkv, n_ctx).reshape(n_tok, mix_w)
        yb = _attn_b_call(qkv, b_sink[l], n_ctx).reshape(n_tok, mix_w)

        lru_x_block = qkv_cols // mix_w
        a_coef, b_coef = _lru_gate_call(
            z, lru_x_block, lru_conv_w[l], lru_conv_b[l], lru_w_r[l].astype(bf16), lru_b_r[l],
            lru_w_i[l].astype(bf16), lru_b_i[l], jax.nn.softplus(-lru_lambda[l]), tiles_per_batch)
        hf, hr = _lru_scan_call(a_coef, b_coef, n_batch, tiles_per_batch)
        yr = _lru_out_call(hf, hr, z, lru_x_block + 1)

        s5_block0 = (qkv_cols + 2 * mix_w) // 128
        k_cmp, w_cmp, m_cmp, lam_r = _s5_weights(s5_a_re[l], s5_a_im[l], s5_log_step[l], s5_b_re[l],
                                                  s5_b_im[l], s5_c_re[l], s5_c_im[l])
        e_parts = _s5_inject_call(z, s5_block0, w_cmp, s5_steps)
        h_parts = _s5_scan_call(e_parts, lam_r, n_batch, n_ctx // S5_CHUNK)
        y_s5 = _s5_read_call(z, s5_block0, h_parts, k_cmp, m_cmp, s5_d[l], s5_steps)
        ys = _glu_call(y_s5, w_glu_b, l, s5_b_glu[l])

        merged = _merge_call(um, (ya, yb, yr, ys), w_gate_b, b_gate[l], w_branch_b, l)
        y = _mm_call(merged, w_out_b, l, f32, name="out_proj")
        xs, um = _ln_call(stream, y, mod_vec(l, 2), ln_g[l, 0], ln_b[l, 0], (mod_vec(l, 4), mod_vec(l, 3)), geom, alpha)

        q = _mm_call(um, w_q_b, l, bf16, name="peer_query")
        n_heads = peer_sub_keys.shape[1]
        keys = peer_sub_keys[l].astype(bf16).reshape(2 * n_heads, PEER_KEYS, -1)
        s0, s1, e0, e1, th = _peer_route_call(q, keys)
        f = _peer_dense_call(um, u_tab_b, v_tab_b, l, s0, e0, s1, e1, th)
        if l == depth - 1:
            return _ln_call((xs,), f, mod_vec(l, 5), ln_g[l, 1], ln_b[l, 1], None, geom, alpha)[0]
        xs, um = _ln_call((xs,), f, mod_vec(l, 5), ln_g[l, 1], ln_b[l, 1],
                          (mod_vec(l + 1, 1), mod_vec(l + 1, 0)), geom, alpha)
        stream = (xs,)
```

```python
import functools
import math

import jax
import jax.numpy as jnp
from jax import lax
from jax.experimental import pallas as pl
from jax.experimental.pallas import tpu as pltpu

f32 = jnp.float32
bf16 = jnp.bfloat16

HEAD_DIM = 128
GRID_W = 64
ROPE_THETA = 10000.0
WINDOW = 128
Q_TILE = 128
KV_GROUP = 4
LRU_C = 8.0
LRU_BLOCKS = 8
S5_GROUP = 16
S5_STATE = 64
S5_CHUNK = 8
S5_OCT = 8
PEER_TOPK = 16
PEER_KEYS = 128
LN_EPS = 1e-6
NEG = -1e30
ROW_TILE = 256
MM_TILE_M = 512
PROJ_TILE_M = 1536
V7X_VMEM_LIMIT = 56 * 1024 * 1024


def _params(*sem, vmem=V7X_VMEM_LIMIT):
    return pltpu.CompilerParams(dimension_semantics=sem, vmem_limit_bytes=vmem)


def _dot(a, b):
    return jnp.dot(a, b, preferred_element_type=f32)


def _dot_nt(a, b):
    return lax.dot_general(a, b, (((1,), (1,)), ((), ())), preferred_element_type=f32)


def _gelu(x):
    return x * (0.5 * (1.0 + jnp.tanh(math.sqrt(2.0 / math.pi) * (x + 0.044715 * (x * x * x)))))


def _sigmoid(x):
    return 1.0 / (1.0 + jnp.exp(-x))


def _ada_kernel(c_ref, w_ref, b_ref, o_ref):
    c = c_ref[...]
    a = (c * _sigmoid(c)).astype(bf16)
    o_ref[0] = _dot(a, w_ref[0].astype(bf16)) + b_ref[0]


def _ada_call(cin, w_ada, b_ada, tn=512):
    depth, d, n = w_ada.shape
    return pl.pallas_call(
        _ada_kernel,
        grid=(depth, n // tn),
        in_specs=[pl.BlockSpec((8, d), lambda l, j: (0, 0)),
                  pl.BlockSpec((1, d, tn), lambda l, j: (l, 0, j)),
                  pl.BlockSpec((1, 1, tn), lambda l, j: (l, 0, j))],
        out_specs=pl.BlockSpec((1, 8, tn), lambda l, j: (l, 0, j)),
        out_shape=jax.ShapeDtypeStruct((depth, 8, n), f32),
        compiler_params=_params("parallel", "parallel"),
        name="adaln",
    )(cin, w_ada, b_ada.reshape(depth, 1, n))


def _row_group(i, tiles_per_batch, ctx_tiles, n_batch):
    return jnp.where(i % tiles_per_batch < ctx_tiles, n_batch, i // tiles_per_batch)


def _stream_specs(stream, geom, d):
    tpb = geom[0]
    if len(stream) == 1:
        return [pl.BlockSpec((ROW_TILE, d), lambda i: (i, 0))]
    return [pl.BlockSpec((1, ROW_TILE, d), lambda i: (i // tpb, jnp.maximum(i % tpb - 1, 0), 0)),
            pl.BlockSpec((1, ROW_TILE, d), lambda i: (i // tpb, 0, 0))]


def _read_stream(refs, is_ctx):
    if len(refs) == 1:
        return refs[0][...]
    return jnp.where(is_ctx, refs[1][0], refs[0][0])


def _mod_kernel(*refs, n_stream, tiles_per_batch):
    sc_ref, sh_ref, o_ref = refs[n_stream:]
    x = _read_stream(refs[:n_stream], pl.program_id(0) % tiles_per_batch == 0)
    o_ref[...] = (x * (1.0 + sc_ref[0]) + sh_ref[0]).astype(o_ref.dtype)


def _mod_call(stream, n, sc, sh, geom):
    d = sc.shape[-1]
    tpb, ct, nb = geom
    sel = lambda i: (_row_group(i, tpb, ct, nb), 0, 0)
    return pl.pallas_call(
        functools.partial(_mod_kernel, n_stream=len(stream), tiles_per_batch=tpb),
        grid=(n // ROW_TILE,),
        in_specs=_stream_specs(stream, geom, d) + [pl.BlockSpec((1, 1, d), sel), pl.BlockSpec((1, 1, d), sel)],
        out_specs=pl.BlockSpec((ROW_TILE, d), lambda i: (i, 0)),
        out_shape=jax.ShapeDtypeStruct((n, d), bf16),
        compiler_params=_params("parallel"),
        name="modulate",
    )(*stream, sc, sh)


def _ln_kernel(*refs, n_stream, tiles_per_batch, alpha, last):
    y_ref, g_ref, lg_ref, lb_ref = refs[n_stream:n_stream + 4]
    is_ctx = pl.program_id(0) % tiles_per_batch == 0
    v = alpha * _read_stream(refs[:n_stream], is_ctx) + g_ref[0] * y_ref[...]
    mu = jnp.mean(v, axis=-1, keepdims=True)
    vc = v - mu
    var = jnp.mean(vc * vc, axis=-1, keepdims=True)
    o = vc * lax.rsqrt(var + LN_EPS) * lg_ref[...] + lb_ref[...]
    if last:
        xo_ref, = refs[n_stream + 4:]

        @pl.when(jnp.logical_not(is_ctx))
        def _():
            xo_ref[0] = o
    else:
        sc_ref, sh_ref, xo_ref, xm_ref = refs[n_stream + 4:]
        xo_ref[...] = o
        xm_ref[...] = (o * (1.0 + sc_ref[0]) + sh_ref[0]).astype(xm_ref.dtype)


def _ln_call(stream, y, gate, ln_g, ln_b, nxt_mod, geom, alpha):
    n, d = y.shape
    tpb, ct, nb = geom
    sel = lambda i: (_row_group(i, tpb, ct, nb), 0, 0)
    row = pl.BlockSpec((ROW_TILE, d), lambda i: (i, 0))
    vec = pl.BlockSpec((1, d), lambda i: (0, 0))
    mod = pl.BlockSpec((1, 1, d), sel)
    last = nxt_mod is None
    in_specs = _stream_specs(stream, geom, d) + [row, mod, vec, vec]
    args = list(stream) + [y, gate, ln_g.reshape(1, d), ln_b.reshape(1, d)]
    if last:
        out_specs = [pl.BlockSpec((1, ROW_TILE, d), lambda i: (i // tpb, jnp.maximum(i % tpb - 1, 0), 0))]
        out_shape = [jax.ShapeDtypeStruct((nb, n // nb - ct * ROW_TILE, d), f32)]
    else:
        in_specs += [mod, mod]
        args += list(nxt_mod)
        out_specs = [row, row]
        out_shape = [jax.ShapeDtypeStruct((n, d), f32), jax.ShapeDtypeStruct((n, d), bf16)]
    return pl.pallas_call(
        functools.partial(_ln_kernel, n_stream=len(stream), tiles_per_batch=tpb, alpha=alpha, last=last),
        grid=(n // ROW_TILE,),
        in_specs=in_specs, out_specs=out_specs, out_shape=out_shape,
        compiler_params=_params("arbitrary" if last else "parallel"),
        name="residual_ln",
    )(*args)


def _mm_kernel(a_ref, b_ref, o_ref):
    o_ref[...] = _dot(a_ref[...], b_ref[...]).astype(o_ref.dtype)


def _mm_call(a, b, layer, out_dtype, tn=512, name="matmul"):
    m, k = a.shape
    n = b.shape[2]
    tm = PROJ_TILE_M if m % PROJ_TILE_M == 0 else MM_TILE_M
    return pl.pallas_call(
        _mm_kernel,
        grid=(m // tm, n // tn),
        in_specs=[pl.BlockSpec((tm, k), lambda i, j: (i, 0)),
                  pl.BlockSpec((None, k, tn), lambda i, j: (layer, 0, j))],
        out_specs=pl.BlockSpec((tm, tn), lambda i, j: (i, j)),
        out_shape=jax.ShapeDtypeStruct((m, n), out_dtype),
        compiler_params=_params("parallel", "arbitrary"),
        name=name,
    )(a, b)


def _glu_kernel(a_ref, wv_ref, wg_ref, bv_ref, bg_ref, o_ref):
    a = a_ref[...].astype(bf16)
    val = _dot(a, wv_ref[...]) + bv_ref[...]
    gate = _dot(a, wg_ref[...]) + bg_ref[...]
    o_ref[...] = (val * _sigmoid(gate)).astype(o_ref.dtype)


def _glu_call(a, w, layer, b, tn=512, tm=MM_TILE_M):
    m, k = a.shape
    n = w.shape[2] // 2
    nj = n // tn
    b2 = b.reshape(1, 2 * n)
    return pl.pallas_call(
        _glu_kernel,
        grid=(m // tm, nj),
        in_specs=[pl.BlockSpec((tm, k), lambda i, j: (i, 0)),
                  pl.BlockSpec((None, k, tn), lambda i, j: (layer, 0, j)),
                  pl.BlockSpec((None, k, tn), lambda i, j: (layer, 0, j + nj)),
                  pl.BlockSpec((1, tn), lambda i, j: (0, j)),
                  pl.BlockSpec((1, tn), lambda i, j: (0, j + nj))],
        out_specs=pl.BlockSpec((tm, tn), lambda i, j: (i, j)),
        out_shape=jax.ShapeDtypeStruct((m, n), bf16),
        compiler_params=_params("parallel", "arbitrary"),
        name="s5_glu",
    )(a, w, w, b2, b2)


def _merge_kernel(u_ref, ya_ref, yb_ref, yr_ref, ys_ref, wg_ref, bg_ref, wb_ref, o_ref):
    u = u_ref[...]
    acc = None
    for i, y_ref in enumerate((ya_ref, yb_ref, yr_ref, ys_ref)):
        gate = _sigmoid(_dot(u, wg_ref[i]) + bg_ref[i])
        term = gate * _dot(y_ref[...], wb_ref[i])
        acc = term if acc is None else acc + term
    o_ref[...] = acc.astype(o_ref.dtype)


def _merge_call(u, branches, w_gate, b_gate, w_branch, layer, tn=256, tm=MM_TILE_M):
    m, d = u.shape
    _, nbr, bw, n = w_branch.shape
    ybr = pl.BlockSpec((tm, bw), lambda i, j: (i, 0))
    return pl.pallas_call(
        _merge_kernel,
        grid=(m // tm, n // tn),
        in_specs=[pl.BlockSpec((tm, d), lambda i, j: (i, 0)), ybr, ybr, ybr, ybr,
                  pl.BlockSpec((None, nbr, d, tn), lambda i, j: (layer, 0, 0, j)),
                  pl.BlockSpec((nbr, 1, tn), lambda i, j: (0, 0, j)),
                  pl.BlockSpec((None, nbr, bw, tn), lambda i, j: (layer, 0, 0, j))],
        out_specs=pl.BlockSpec((tm, tn), lambda i, j: (i, j)),
        out_shape=jax.ShapeDtypeStruct((m, n), bf16),
        compiler_params=_params("parallel", "arbitrary"),
        name="branch_merge",
    )(u, *branches, w_gate, b_gate.reshape(nbr, 1, n), w_branch)


def _rope_tables(n_lat, n_ctx):
    t = jnp.arange(n_lat)
    row = (t // GRID_W).astype(f32)
    col = (t % GRID_W).astype(f32)
    n_freq = HEAD_DIM // 4
    inv = ROPE_THETA ** (-jnp.arange(n_freq, dtype=f32) / n_freq)
    ang = jnp.concatenate([row[:, None] * inv, col[:, None] * inv], axis=-1)
    cos = jnp.repeat(jnp.cos(ang), 2, axis=-1)
    sin = jnp.repeat(jnp.sin(ang), 2, axis=-1) * jnp.tile(jnp.array([-1.0, 1.0], f32), HEAD_DIM // 2)
    cos = jnp.concatenate([jnp.ones((n_ctx, HEAD_DIM), f32), cos], axis=0)
    sin = jnp.concatenate([jnp.zeros((n_ctx, HEAD_DIM), f32), sin], axis=0)
    return cos, sin


N_QKV_SLICES = 24
A_Q, A_K, A_V, B_Q, B_K, B_V = 0, 8, 10, 12, 20, 22


def _prep_kernel(z_ref, cos_ref, sin_ref, gq_ref, gk_ref, o_ref):
    cos, sin = cos_ref[...], sin_ref[...]
    even = lax.broadcasted_iota(jnp.int32, cos.shape, 1) % 2 == 0
    for s in range(N_QKV_SLICES):
        y = z_ref[:, s * HEAD_DIM:(s + 1) * HEAD_DIM]
        if s < A_V:
            gain = gq_ref[...] if s < A_K else gk_ref[...]
            y = y * lax.rsqrt(jnp.mean(y * y, axis=-1, keepdims=True) + LN_EPS) * gain
        if s < A_V or B_Q <= s < B_V:
            partner = jnp.where(even, pltpu.roll(y, HEAD_DIM - 1, axis=1), pltpu.roll(y, 1, axis=1))
            y = y * cos + partner * sin
        if s < A_K or B_Q <= s < B_K:
            y = y * (HEAD_DIM ** -0.5)
        o_ref[0, s] = y.astype(o_ref.dtype)


def _prep_call(z, cos, sin, gain_q, gain_k, n_batch, seq_tot):
    n = z.shape[0]
    tr = ROW_TILE
    tpb = seq_tot // tr
    width = N_QKV_SLICES * HEAD_DIM
    vec = pl.BlockSpec((1, HEAD_DIM), lambda i: (0, 0))
    return pl.pallas_call(
        _prep_kernel,
        grid=(n // tr,),
        in_specs=[pl.BlockSpec((tr, width), lambda i: (i, 0)),
                  pl.BlockSpec((tr, HEAD_DIM), lambda i: (i % tpb, 0)),
                  pl.BlockSpec((tr, HEAD_DIM), lambda i: (i % tpb, 0)), vec, vec],
        out_specs=pl.BlockSpec((1, N_QKV_SLICES, tr, HEAD_DIM), lambda i: (i // tpb, 0, i % tpb, 0)),
        out_shape=jax.ShapeDtypeStruct((n_batch, N_QKV_SLICES, seq_tot, HEAD_DIM), bf16),
        compiler_params=_params("parallel"),
        name="qkv_prep",
    )(z, cos, sin, gain_q.reshape(1, HEAD_DIM), gain_k.reshape(1, HEAD_DIM))


def _attn_a_kernel(q_ref, k_ref, v_ref, o_ref, m_s, acc_s, *, n_ctx, tk, n_lat_chunks):
    q = q_ref[0].reshape(KV_GROUP * Q_TILE, HEAD_DIM)

    def values(lo, n):
        return jnp.concatenate([v_ref[0, 0, lo:lo + n, :], jnp.ones((n, HEAD_DIM), bf16)], axis=1)

    s = _dot_nt(q, k_ref[0, 0, 0:n_ctx, :])
    m0 = jnp.max(s, axis=-1, keepdims=True)
    m_s[...] = m0
    acc_s[...] = _dot(jnp.exp((s - m0).astype(bf16)), values(0, n_ctx))

    @pl.when(pl.program_id(2) >= n_ctx // Q_TILE)
    def _():
        for c in range(n_lat_chunks):
            lo = n_ctx + c * tk
            s = _dot_nt(q, k_ref[0, 0, lo:lo + tk, :])
            m_prev = m_s[...]
            m_new = jnp.maximum(m_prev, jnp.max(s, axis=-1, keepdims=True))
            p = jnp.exp((s - m_new).astype(bf16))
            acc_s[...] = jnp.exp(m_prev - m_new) * acc_s[...] + _dot(p, values(lo, tk))
            m_s[...] = m_new

    out = acc_s[:, 0:HEAD_DIM] / acc_s[:, HEAD_DIM:2 * HEAD_DIM]
    for g in range(KV_GROUP):
        o_ref[0, :, g * HEAD_DIM:(g + 1) * HEAD_DIM] = out[g * Q_TILE:(g + 1) * Q_TILE].astype(o_ref.dtype)


def _attn_a_call(qkv, n_ctx, tk=1024):
    nb, _, seq_tot, _ = qkv.shape
    n_kv = (A_V - A_K)
    rows = KV_GROUP * Q_TILE
    n_lat = seq_tot - n_ctx
    tk = min(tk, n_lat)
    assert n_lat % tk == 0
    kern = functools.partial(_attn_a_kernel, n_ctx=n_ctx, tk=tk, n_lat_chunks=n_lat // tk)
    return pl.pallas_call(
        kern,
        grid=(nb, n_kv, seq_tot // Q_TILE),
        in_specs=[pl.BlockSpec((1, KV_GROUP, Q_TILE, HEAD_DIM), lambda b, h, i: (b, A_Q // KV_GROUP + h, i, 0)),
                  pl.BlockSpec((1, 1, seq_tot, HEAD_DIM), lambda b, h, i: (b, A_K + h, 0, 0)),
                  pl.BlockSpec((1, 1, seq_tot, HEAD_DIM), lambda b, h, i: (b, A_V + h, 0, 0))],
        out_specs=pl.BlockSpec((1, Q_TILE, KV_GROUP * HEAD_DIM), lambda b, h, i: (b, i, h)),
        out_shape=jax.ShapeDtypeStruct((nb, seq_tot, n_kv * KV_GROUP * HEAD_DIM), bf16),
        scratch_shapes=[pltpu.VMEM((rows, 1), f32), pltpu.VMEM((rows, 2 * HEAD_DIM), f32)],
        compiler_params=_params("parallel", "parallel", "arbitrary"),
        name="dense_attention",
    )(qkv, qkv, qkv)


def _attn_b_kernel(q_ref, k_ref, v_ref, sink_ref, o_ref, *, n_ctx, n_blk):
    i = pl.program_id(2)
    blk = i - n_ctx // Q_TILE
    rows = KV_GROUP * Q_TILE
    q = q_ref[0].reshape(rows, HEAD_DIM)
    sink = sink_ref[0]
    kc = k_ref[0, 0, 0:n_ctx, :]
    vc = v_ref[0, 0, 0:n_ctx, :]
    s_ctx = _dot_nt(q, kc)
    m = jnp.maximum(jnp.max(s_ctx, axis=-1, keepdims=True), sink)
    q_pos = lax.broadcasted_iota(jnp.int32, (rows, Q_TILE), 0) % Q_TILE
    k_off = lax.broadcasted_iota(jnp.int32, (rows, Q_TILE), 1)
    s_loc, v_loc = [], []
    for d in (-1, 0, 1):
        j = blk + d
        valid = jnp.logical_and(blk >= 0, jnp.logical_and(j >= 0, j < n_blk))
        start = pl.multiple_of(n_ctx + jnp.clip(j, 0, n_blk - 1) * Q_TILE, Q_TILE)
        k = k_ref[0, 0, pl.ds(start, Q_TILE), :]
        v_loc.append(v_ref[0, 0, pl.ds(start, Q_TILE), :])
        keep = jnp.logical_and(jnp.abs(k_off + d * Q_TILE - q_pos) <= WINDOW, valid)
        s = jnp.where(keep, _dot_nt(q, k), NEG)
        m = jnp.maximum(m, jnp.max(s, axis=-1, keepdims=True))
        s_loc.append(s)
    ones = jnp.ones((Q_TILE, HEAD_DIM), bf16)
    acc = _dot(jnp.exp((s_ctx - m).astype(bf16)), jnp.concatenate([vc, jnp.ones((n_ctx, HEAD_DIM), bf16)], axis=1))
    for s, v in zip(s_loc, v_loc):
        acc = acc + _dot(jnp.exp((s - m).astype(bf16)), jnp.concatenate([v, ones], axis=1))
    out = acc[:, 0:HEAD_DIM] / (jnp.exp(sink - m) + acc[:, HEAD_DIM:2 * HEAD_DIM])
    for g in range(KV_GROUP):
        o_ref[0, :, g * HEAD_DIM:(g + 1) * HEAD_DIM] = out[g * Q_TILE:(g + 1) * Q_TILE].astype(o_ref.dtype)


def _attn_b_call(qkv, sink, n_ctx):
    nb, _, seq_tot, _ = qkv.shape
    n_kv = (B_V - B_K)
    rows = KV_GROUP * Q_TILE
    sink_rows = jnp.repeat(sink.astype(f32).reshape(n_kv, KV_GROUP), Q_TILE, axis=1).reshape(n_kv, rows, 1)
    kern = functools.partial(_attn_b_kernel, n_ctx=n_ctx, n_blk=(seq_tot - n_ctx) // Q_TILE)
    return pl.pallas_call(
        kern,
        grid=(nb, n_kv, seq_tot // Q_TILE),
        in_specs=[pl.BlockSpec((1, KV_GROUP, Q_TILE, HEAD_DIM), lambda b, h, i: (b, B_Q // KV_GROUP + h, i, 0)),
                  pl.BlockSpec((1, 1, seq_tot, HEAD_DIM), lambda b, h, i: (b, B_K + h, 0, 0)),
                  pl.BlockSpec((1, 1, seq_tot, HEAD_DIM), lambda b, h, i: (b, B_V + h, 0, 0)),
                  pl.BlockSpec((1, rows, 1), lambda b, h, i: (h, 0, 0))],
        out_specs=pl.BlockSpec((1, Q_TILE, KV_GROUP * HEAD_DIM), lambda b, h, i: (b, i, h)),
        out_shape=jax.ShapeDtypeStruct((nb, seq_tot, n_kv * KV_GROUP * HEAD_DIM), bf16),
        compiler_params=_params("parallel", "parallel", "arbitrary"),
        name="window_attention",
    )(qkv, qkv, qkv, sink_rows)


HALO = 8


def _lru_gate_kernel(x_ref, xp_ref, xn_ref, cw_ref, cb_ref, wr_ref, br_ref, wi_ref, bi_ref, sp_ref,
                     a_ref, b_ref, pad_s, *, tiles_per_batch):
    tr = x_ref.shape[0]
    t = pl.program_id(0) % tiles_per_batch
    has_prev = t >= 2
    has_next = jnp.logical_and(t >= 1, t < tiles_per_batch - 1)
    x = x_ref[...]
    pad_s[0:HALO, :] = jnp.where(has_prev, xp_ref[...], 0.0)
    pad_s[HALO:HALO + tr, :] = x
    pad_s[HALO + tr:2 * HALO + tr, :] = jnp.where(has_next, xn_ref[...], 0.0)
    cw = cw_ref[...]
    xl = cb_ref[...] + pad_s[HALO - 2:HALO - 2 + tr, :] * cw[0:1]
    xl = xl + pad_s[HALO - 1:HALO - 1 + tr, :] * cw[1:2]
    xl = xl + x * cw[2:3]
    xl = xl + pad_s[HALO + 1:HALO + 1 + tr, :] * cw[3:4]
    bw = xl.shape[1] // LRU_BLOCKS
    for n in range(LRU_BLOCKS):
        cols = slice(n * bw, (n + 1) * bw)
        xb = xl[:, cols]
        xb16 = xb.astype(bf16)
        for d in range(2):
            r = _sigmoid(_dot(xb16, wr_ref[d, n]) + br_ref[d:d + 1, cols])
            gi = _sigmoid(_dot(xb16, wi_ref[d, n]) + bi_ref[d:d + 1, cols])
            log_a = (-LRU_C) * r * sp_ref[d:d + 1, cols]
            a = jnp.exp(log_a)
            a_ref[d, :, cols] = a
            b_ref[d, :, cols] = jnp.sqrt(1.0 - a * a) * gi * xb


def _lru_gate_call(z, col_block, conv_w, conv_b, w_r, b_r, w_i, b_i, softplus_lam, tiles_per_batch):
    n = z.shape[0]
    w = conv_w.shape[1]
    tr = ROW_TILE
    hb = tr // HALO
    n_halo = n // HALO
    full = lambda shape: pl.BlockSpec(shape, lambda i: (0,) * len(shape))
    kern = functools.partial(_lru_gate_kernel, tiles_per_batch=tiles_per_batch)
    out = jax.ShapeDtypeStruct((2, n, w), f32)
    return pl.pallas_call(
        kern,
        grid=(n // tr,),
        in_specs=[pl.BlockSpec((tr, w), lambda i: (i, col_block)),
                  pl.BlockSpec((HALO, w), lambda i: (jnp.maximum(i * hb - 1, 0), col_block)),
                  pl.BlockSpec((HALO, w), lambda i: (jnp.minimum((i + 1) * hb, n_halo - 1), col_block)),
                  full(conv_w.shape), full((1, w)), full(w_r.shape), full(b_r.shape),
                  full(w_i.shape), full(b_i.shape), full(softplus_lam.shape)],
        out_specs=[pl.BlockSpec((2, tr, w), lambda i: (0, i, 0)), pl.BlockSpec((2, tr, w), lambda i: (0, i, 0))],
        out_shape=[out, out],
        scratch_shapes=[pltpu.VMEM((tr + 2 * HALO, w), f32)],
        compiler_params=_params("parallel"),
        name="lru_gates",
    )(z, z, z, conv_w, conv_b.reshape(1, w), w_r, b_r, w_i, b_i, softplus_lam)


def _lru_scan_kernel(af_ref, bf_ref, ar_ref, br_ref, hf_ref, hr_ref, sf, sr):
    tr = hf_ref.shape[0]

    @pl.when(pl.program_id(1) == 0)
    def _():
        sf[...] = jnp.zeros(sf.shape, f32)
        sr[...] = jnp.zeros(sr.shape, f32)

    def body(t, carry):
        hf, hr = carry
        hf = af_ref[0, pl.ds(t, 1), :] * hf + bf_ref[0, pl.ds(t, 1), :]
        hf_ref[pl.ds(t, 1), :] = hf
        u = tr - 1 - t
        hr = ar_ref[0, pl.ds(u, 1), :] * hr + br_ref[0, pl.ds(u, 1), :]
        hr_ref[pl.ds(u, 1), :] = hr
        return hf, hr

    hf, hr = lax.fori_loop(0, tr, body, (sf[...], sr[...]), unroll=8)
    sf[...] = hf
    sr[...] = hr


def _scan_tile_maps(tiles_per_batch):
    fwd = lambda b, i: b * tiles_per_batch + i
    rev = lambda b, i: b * tiles_per_batch + jnp.where(i == 0, 0, tiles_per_batch - i)
    return fwd, rev


def _lru_scan_call(a, b, n_batch, tiles_per_batch):
    _, n, w = a.shape
    tr = ROW_TILE
    fwd, rev = _scan_tile_maps(tiles_per_batch)
    out = jax.ShapeDtypeStruct((n, w), f32)
    return pl.pallas_call(
        _lru_scan_kernel,
        grid=(n_batch, tiles_per_batch),
        in_specs=[pl.BlockSpec((1, tr, w), lambda bb, i: (0, fwd(bb, i), 0)),
                  pl.BlockSpec((1, tr, w), lambda bb, i: (0, fwd(bb, i), 0)),
                  pl.BlockSpec((1, tr, w), lambda bb, i: (1, rev(bb, i), 0)),
                  pl.BlockSpec((1, tr, w), lambda bb, i: (1, rev(bb, i), 0))],
        out_specs=[pl.BlockSpec((tr, w), lambda bb, i: (fwd(bb, i), 0)),
                   pl.BlockSpec((tr, w), lambda bb, i: (rev(bb, i), 0))],
        out_shape=[out, out],
        scratch_shapes=[pltpu.VMEM((1, w), f32), pltpu.VMEM((1, w), f32)],
        compiler_params=_params("parallel", "arbitrary"),
        name="lru_scan",
    )(a, b, a, b)


def _lru_out_kernel(hf_ref, hr_ref, g_ref, o_ref):
    o_ref[...] = ((hf_ref[...] + hr_ref[...]) * _gelu(g_ref[...])).astype(o_ref.dtype)


def _lru_out_call(hf, hr, z, gate_col_block):
    n, w = hf.shape
    tr = ROW_TILE
    row = pl.BlockSpec((tr, w), lambda i: (i, 0))
    return pl.pallas_call(
        _lru_out_kernel,
        grid=(n // tr,),
        in_specs=[row, row, pl.BlockSpec((tr, w), lambda i: (i, gate_col_block))],
        out_specs=row,
        out_shape=jax.ShapeDtypeStruct((n, w), bf16),
        compiler_params=_params("parallel"),
        name="lru_out",
    )(hf, hr, z)


def _s5_weights(a_re, a_im, log_step, b_re, b_im, c_re, c_im):
    r = S5_CHUNK
    n_g = a_re.shape[1]
    n_oct = n_g // S5_OCT
    step = jnp.exp(log_step)[..., None]
    mag = jnp.exp(a_re * step)
    lb_re, lb_im = mag * jnp.cos(a_im * step), mag * jnp.sin(a_im * step)
    den = a_re * a_re + a_im * a_im
    num_re = lb_re - 1.0
    coef_re = (num_re * a_re + lb_im * a_im) / den
    coef_im = (lb_im * a_re - num_re * a_im) / den
    bb_re = coef_re[..., None] * b_re - coef_im[..., None] * b_im
    bb_im = coef_re[..., None] * b_im + coef_im[..., None] * b_re
    tau = jnp.arange(r + 1, dtype=f32)[:, None, None, None]
    pmag = jnp.exp(tau * (a_re * step))
    pw_re, pw_im = pmag * jnp.cos(tau * (a_im * step)), pmag * jnp.sin(tau * (a_im * step))

    lb_b_re = pw_re[..., None] * bb_re - pw_im[..., None] * bb_im
    lb_b_im = pw_re[..., None] * bb_im + pw_im[..., None] * bb_re
    lag = (jnp.einsum('tdgpc,dgop->tdgco', lb_b_re, c_re) - jnp.einsum('tdgpc,dgop->tdgco', lb_b_im, c_im))
    idx = jnp.arange(r)
    sh_f = (idx[None, None, :] - idx[None, :, None] == jnp.arange(r + 1)[:, None, None]).astype(f32)
    sh_r = (idx[None, :, None] - idx[None, None, :] == jnp.arange(r + 1)[:, None, None]).astype(f32)
    k_loc = jnp.einsum('tio,tgcd->gicod', sh_f, lag[:, 0]) + jnp.einsum('tio,tgcd->gicod', sh_r, lag[:, 1])
    k_cmp = k_loc.reshape(n_oct, S5_OCT, r, S5_GROUP, r * S5_GROUP)
    k_cmp = k_cmp.transpose(0, 2, 1, 3, 4).reshape(n_oct, r * 128, r * S5_GROUP)

    def inject(d, powers):
        w_re = pw_re[powers, d][..., None] * bb_re[d] - pw_im[powers, d][..., None] * bb_im[d]
        w_im = pw_re[powers, d][..., None] * bb_im[d] + pw_im[powers, d][..., None] * bb_re[d]
        out = []
        for w in (w_re, w_im):
            w = w.reshape(r, n_oct, S5_OCT, S5_STATE, S5_GROUP).transpose(1, 0, 2, 4, 3)
            out.append(w.reshape(n_oct, r * 128, S5_STATE))
        return out
    w_cmp = jnp.concatenate(inject(0, idx[::-1]) + inject(1, idx), axis=-1)

    def readout(d, powers):
        cl_re = c_re[d][None] * pw_re[powers, d][:, :, None, :] - c_im[d][None] * pw_im[powers, d][:, :, None, :]
        cl_im = c_re[d][None] * pw_im[powers, d][:, :, None, :] + c_im[d][None] * pw_re[powers, d][:, :, None, :]
        out = []
        for w in (cl_re, -cl_im):
            w = w.reshape(r, n_oct, S5_OCT, S5_GROUP, S5_STATE).transpose(1, 2, 4, 0, 3)
            out.append(w.reshape(n_oct, S5_OCT * S5_STATE, r * S5_GROUP))
        return out
    m_cmp = jnp.concatenate(readout(0, idx + 1) + readout(1, r - idx), axis=1)

    lam_r = jnp.stack([pw_re[r, 0].reshape(-1), pw_im[r, 0].reshape(-1),
                       pw_re[r, 1].reshape(-1), pw_im[r, 1].reshape(-1)])
    return k_cmp.astype(bf16), w_cmp.astype(bf16), m_cmp.astype(bf16), lam_r


def _expand_octet(compact, inner, row_inner):
    n_rows, n_cols = compact.shape
    wide = n_cols * S5_OCT
    log_inner, log_row = inner.bit_length() - 1, row_inner.bit_length() - 1
    oct_bits = S5_OCT.bit_length() - 1
    src = lax.broadcasted_iota(jnp.int32, (n_cols, wide), 0)
    dst = lax.broadcasted_iota(jnp.int32, (n_cols, wide), 1)
    dst_compact = ((dst >> (log_inner + oct_bits)) << log_inner) + (dst & (inner - 1))
    spread = _dot(compact, (src == dst_compact).astype(compact.dtype))
    row_g = (lax.broadcasted_iota(jnp.int32, (n_rows, wide), 0) >> log_row) & (S5_OCT - 1)
    col_g = (lax.broadcasted_iota(jnp.int32, (n_rows, wide), 1) >> log_inner) & (S5_OCT - 1)
    return jnp.where(row_g == col_g, spread, 0.0).astype(compact.dtype)


def _chunk_rows(u_ref):
    n_rows = u_ref.shape[0] // S5_CHUNK
    return [u_ref[pl.ds(i, n_rows, stride=S5_CHUNK), :] for i in range(S5_CHUNK)]


def _s5_inject_kernel(u_ref, w_ref, efr_ref, efi_ref, err_ref, eri_ref, w_s):
    @pl.when(pl.program_id(1) == 0)
    def _():
        w_s[...] = _expand_octet(w_ref[0], S5_STATE, S5_GROUP)

    x = jnp.concatenate(_chunk_rows(u_ref), axis=1).astype(bf16)
    e = _dot(x, w_s[...])
    w = efr_ref.shape[1]
    for k, o_ref in enumerate((efr_ref, efi_ref, err_ref, eri_ref)):
        o_ref[...] = e[:, k * w:(k + 1) * w]


def _s5_inject_call(z, col_block0, w_cmp, steps):
    n = z.shape[0]
    n_oct, n_in, n_cmp = w_cmp.shape
    sw = S5_OCT * S5_STATE
    rows = steps // S5_CHUNK
    out = jax.ShapeDtypeStruct((n // S5_CHUNK, n_oct * sw), f32)
    ospec = pl.BlockSpec((rows, sw), lambda k, i: (i, k))
    return pl.pallas_call(
        _s5_inject_kernel,
        grid=(n_oct, n // steps),
        in_specs=[pl.BlockSpec((steps, 128), lambda k, i: (i, col_block0 + k)),
                  pl.BlockSpec((1, n_in, n_cmp), lambda k, i: (k, 0, 0))],
        out_specs=[ospec] * 4,
        out_shape=[out] * 4,
        scratch_shapes=[pltpu.VMEM((n_in, n_cmp * S5_OCT), bf16)],
        compiler_params=_params("arbitrary", "arbitrary"),
        name="s5_inject",
    )(z, w_cmp)


def _s5_scan_kernel(efr_ref, efi_ref, err_ref, eri_ref, lam_ref, hfr_ref, hfi_ref, hrr_ref, hri_ref, *, n_ctx):
    n_rows = efr_ref.shape[0]
    lfr, lfi, lrr, lri = (lam_ref[k:k + 1, :] for k in range(4))
    zero = jnp.zeros((1, efr_ref.shape[1]), f32)

    def body(t, carry):
        fr, fi, rr, ri = carry
        hfr_ref[pl.ds(t, 1), :] = fr
        hfi_ref[pl.ds(t, 1), :] = fi
        er, ei = efr_ref[pl.ds(t, 1), :], efi_ref[pl.ds(t, 1), :]
        fr, fi = lfr * fr - lfi * fi + er, lfr * fi + lfi * fr + ei
        u = jnp.where(t < n_ctx, n_ctx - 1 - t, n_rows - 1 - (t - n_ctx))
        hrr_ref[pl.ds(u, 1), :] = rr
        hri_ref[pl.ds(u, 1), :] = ri
        er, ei = err_ref[pl.ds(u, 1), :], eri_ref[pl.ds(u, 1), :]
        rr, ri = lrr * rr - lri * ri + er, lrr * ri + lri * rr + ei
        return fr, fi, rr, ri

    lax.fori_loop(0, n_rows, body, (zero, zero, zero, zero), unroll=4)


def _s5_scan_call(e_parts, lam_r, n_batch, n_ctx_rows, lane_block=512):
    n_rows_tot, width = e_parts[0].shape
    rows = n_rows_tot // n_batch
    blk = pl.BlockSpec((rows, lane_block), lambda b, j: (b, j))
    out = jax.ShapeDtypeStruct((n_rows_tot, width), f32)
    return pl.pallas_call(
        functools.partial(_s5_scan_kernel, n_ctx=n_ctx_rows),
        grid=(n_batch, width // lane_block),
        in_specs=[blk] * 4 + [pl.BlockSpec((4, lane_block), lambda b, j: (0, j))],
        out_specs=[blk] * 4,
        out_shape=[out] * 4,
        compiler_params=_params("parallel", "parallel"),
        name="s5_scan",
    )(*e_parts, lam_r)


def _s5_read_kernel(u_ref, hfr_ref, hfi_ref, hrr_ref, hri_ref, k_ref, m_ref, d_ref, o_ref, k_s, m_s):
    @pl.when(pl.program_id(1) == 0)
    def _():
        k_s[...] = _expand_octet(k_ref[0], S5_GROUP, S5_GROUP)
        m_s[...] = _expand_octet(m_ref[0], S5_GROUP, S5_STATE)

    parts = _chunk_rows(u_ref)
    x = jnp.concatenate(parts, axis=1).astype(bf16)
    h = jnp.concatenate([hfr_ref[...], hfi_ref[...], hrr_ref[...], hri_ref[...]], axis=1).astype(bf16)
    y = _dot(x, k_s[...]) + _dot(h, m_s[...])
    n_rows = u_ref.shape[0] // S5_CHUNK
    for i in range(S5_CHUNK):
        o_ref[pl.ds(i, n_rows, stride=S5_CHUNK), :] = y[:, i * 128:(i + 1) * 128] + parts[i] * d_ref[...]


def _s5_read_call(z, col_block0, h_parts, k_cmp, m_cmp, d_skip, steps):
    n = z.shape[0]
    n_oct = k_cmp.shape[0]
    sw = S5_OCT * S5_STATE
    rows = steps // S5_CHUNK
    hspec = pl.BlockSpec((rows, sw), lambda k, i: (i, k))
    return pl.pallas_call(
        _s5_read_kernel,
        grid=(n_oct, n // steps),
        in_specs=[pl.BlockSpec((steps, 128), lambda k, i: (i, col_block0 + k))] + [hspec] * 4 +
                 [pl.BlockSpec((1,) + k_cmp.shape[1:], lambda k, i: (k, 0, 0)),
                  pl.BlockSpec((1,) + m_cmp.shape[1:], lambda k, i: (k, 0, 0)),
                  pl.BlockSpec((1, 128), lambda k, i: (0, k))],
        out_specs=pl.BlockSpec((steps, 128), lambda k, i: (i, k)),
        out_shape=jax.ShapeDtypeStruct((n, n_oct * 128), f32),
        scratch_shapes=[pltpu.VMEM((k_cmp.shape[1], k_cmp.shape[2] * S5_OCT), bf16),
                        pltpu.VMEM((m_cmp.shape[1], m_cmp.shape[2] * S5_OCT), bf16)],
        compiler_params=_params("arbitrary", "arbitrary"),
        name="s5_readout",
    )(z, *h_parts, k_cmp, m_cmp, d_skip.reshape(1, -1))


def _top_values(s, k):
    vals = []
    for _ in range(k):
        m = jnp.max(s, axis=0, keepdims=True)
        vals.append(m)
        s = jnp.where(s == m, -jnp.inf, s)
    return vals


def _merge_sort_pairs(n):
    pairs = []

    def merge(lo, hi, r):
        step = r * 2
        if step < hi - lo:
            merge(lo, hi, step)
            merge(lo + r, hi, step)
            pairs.extend((i, i + r) for i in range(lo + r, hi - r, step))
        else:
            pairs.append((lo, lo + r))

    def sort(lo, hi):
        if hi - lo >= 1:
            mid = lo + (hi - lo) // 2
            sort(lo, mid)
            sort(mid + 1, hi)
            merge(lo, hi, 1)

    sort(0, n - 1)
    return pairs


def _top_values_sorted(s, k):
    n_grp = s.shape[0] // 8
    lists = [s[8 * v:8 * v + 8, :] for v in range(n_grp)]
    for i, j in _merge_sort_pairs(n_grp):
        lists[i], lists[j] = jnp.maximum(lists[i], lists[j]), jnp.minimum(lists[i], lists[j])
    lists.append(jnp.full_like(lists[0], -jnp.inf))
    vals = []
    for t in range(k):
        m = jnp.max(lists[0], axis=0, keepdims=True)
        vals.append(m)
        hit = lists[0] == m
        for v in range(min(k - t, n_grp)):
            lists[v] = jnp.where(hit, lists[v + 1], lists[v])
    return vals


def _peer_route_kernel(q_ref, keys_ref, s0_ref, s1_ref, e0_ref, e1_ref, th_ref):
    k = PEER_TOPK
    s0 = _dot_nt(keys_ref[0], q_ref[:, 0:PEER_KEYS])
    s1 = _dot_nt(keys_ref[1], q_ref[:, PEER_KEYS:2 * PEER_KEYS])
    top0, top1 = _top_values_sorted(s0, k + 1), _top_values_sorted(s1, k + 1)
    pad = [jnp.full_like(top0[0], -jnp.inf)] * 7
    t0, t1 = jnp.concatenate(top0 + pad, axis=0), jnp.concatenate(top1 + pad, axis=0)
    cand = jnp.concatenate([top0[0] + t1] + [top0[i] + t1[0:8] for i in range(1, 8)] + [t0[8:24] + top1[0]], axis=0)
    best = _top_values(cand, k + 1)
    z = None
    for v in best[:k]:
        e = jnp.exp(v - best[0])
        z = e if z is None else z + e
    s0_ref[0] = s0.reshape(s0_ref.shape[1:])
    s1_ref[0] = s1
    e0_ref[0] = (jnp.exp(s0 - top0[0]) / z).reshape(e0_ref.shape[1:])
    e1_ref[0] = jnp.exp(s1 - top1[0])
    th_ref[0] = 0.5 * (best[k - 1] + best[k])


def _peer_route_call(q, keys, tt=512):
    n = q.shape[0]
    n_heads = keys.shape[0] // 2
    big = jax.ShapeDtypeStruct((n_heads, PEER_KEYS, n), f32)
    bspec = pl.BlockSpec((1, PEER_KEYS, tt), lambda t, h: (h, 0, t))
    grp = jax.ShapeDtypeStruct((n_heads, PEER_KEYS // 8, 8, n), f32)
    gspec = pl.BlockSpec((1, PEER_KEYS // 8, 8, tt), lambda t, h: (h, 0, 0, t))
    return pl.pallas_call(
        _peer_route_kernel,
        grid=(n // tt, n_heads),
        in_specs=[pl.BlockSpec((tt, 2 * PEER_KEYS), lambda t, h: (t, h)),
                  pl.BlockSpec((2, PEER_KEYS, PEER_KEYS), lambda t, h: (h, 0, 0))],
        out_specs=[gspec, bspec, gspec, bspec, pl.BlockSpec((1, 1, tt), lambda t, h: (h, 0, t))],
        out_shape=[grp, big, grp, big, jax.ShapeDtypeStruct((n_heads, 1, n), f32)],
        compiler_params=_params("parallel", "arbitrary"),
        name="peer_route",
    )(q, keys)


PEER_I_PER_TILE = 4


def _peer_dense_kernel(xm_ref, u_ref, v_ref, s0_ref, e0_ref, s1_ref, e1_ref, th_ref, o_ref, s_s, w_s):
    e = pl.program_id(1)
    n_steps = pl.num_programs(1)

    @pl.when(e == 0)
    def _():
        o_ref[...] = jnp.zeros(o_ref.shape, f32)

    @pl.when(e > 0)
    def _():
        n_heads = s1_ref.shape[0]
        sub = ((e - 1) % (8 // PEER_I_PER_TILE)) * PEER_I_PER_TILE
        for il in range(PEER_I_PER_TILE):
            w = None
            for h in range(n_heads):
                s0 = s0_ref[h, 0, pl.ds(sub + il, 1), :]
                e0 = e0_ref[h, 0, pl.ds(sub + il, 1), :]
                keep = s1_ref[h] >= th_ref[h] - s0
                term = jnp.where(keep, e0 * e1_ref[h], 0.0)
                w = term if w is None else w + term
            w_s[il * PEER_KEYS:(il + 1) * PEER_KEYS, :] = w
        act = (_gelu(s_s[...]) * w_s[...]).T.astype(bf16)
        o_ref[...] += _dot(act, v_ref[...])

    @pl.when(e < n_steps - 1)
    def _():
        s_s[...] = _dot_nt(u_ref[...], xm_ref[...])


def _peer_dense_call(xm, u_tab, v_tab, layer, s0, e0, s1, e1, th, tm=512):
    n, d = xm.shape
    n_exp = u_tab.shape[1]
    te = PEER_I_PER_TILE * PEER_KEYS
    n_tiles = n_exp // te
    n_heads = s1.shape[0]
    per8 = 8 // PEER_I_PER_TILE
    prev = lambda e: jnp.maximum(e - 1, 0)
    row8 = pl.BlockSpec((n_heads, 1, 8, tm), lambda t, e: (0, prev(e) // per8, 0, t))
    full = pl.BlockSpec((n_heads, PEER_KEYS, tm), lambda t, e: (0, 0, t))
    return pl.pallas_call(
        _peer_dense_kernel,
        grid=(n // tm, n_tiles + 1),
        in_specs=[pl.BlockSpec((tm, d), lambda t, e: (t, 0)),
                  pl.BlockSpec((None, te, d), lambda t, e: (layer, jnp.minimum(e, n_tiles - 1), 0)),
                  pl.BlockSpec((None, te, d), lambda t, e: (layer, prev(e), 0)),
                  row8, row8, full, full,
                  pl.BlockSpec((n_heads, 1, tm), lambda t, e: (0, 0, t))],
        out_specs=pl.BlockSpec((tm, d), lambda t, e: (t, 0)),
        out_shape=jax.ShapeDtypeStruct((n, d), f32),
        scratch_shapes=[pltpu.VMEM((te, tm), f32), pltpu.VMEM((te, tm), f32)],
        compiler_params=_params("parallel", "arbitrary", vmem=60 * 1024 * 1024),
        name="peer_dense",
    )(xm, u_tab, v_tab, s0, e0, s1, e1, th)


def kernel(x, c, ctx, c_ctx, w_ada, b_ada, w_in, a_q_norm, a_k_norm, b_sink, lru_conv_w, lru_conv_b, lru_w_r, lru_b_r, lru_w_i, lru_b_i, lru_lambda, s5_a_re, s5_a_im, s5_log_step, s5_b_re, s5_b_im, s5_c_re, s5_c_im, s5_d, s5_w_glu, s5_b_glu, w_gate, b_gate, w_branch, w_out, ln_g, ln_b, peer_w_q, peer_sub_keys, peer_u, peer_v):
    n_batch, n_lat, d = x.shape
    n_ctx = ctx.shape[1]
    depth = w_ada.shape[0]
    seq_tot = n_ctx + n_lat
    n_tok = n_batch * seq_tot
    mix_w = d // 4
    assert n_ctx % ROW_TILE == 0 and n_lat % ROW_TILE == 0 and n_tok % MM_TILE_M == 0
    assert n_ctx == ROW_TILE, "scan kernels treat row tile 0 of every batch element as the context"
    tiles_per_batch = seq_tot // ROW_TILE
    geom = (tiles_per_batch, n_ctx // ROW_TILE, n_batch)
    alpha = (2.0 * depth) ** 0.25
    s5_steps = seq_tot // 4
    assert seq_tot % 4 == 0 and s5_steps % (8 * S5_CHUNK) == 0 and n_ctx % S5_CHUNK == 0

    cin = jnp.zeros((8, d), f32).at[:n_batch].set(c).at[n_batch].set(c_ctx)
    mod = _ada_call(cin, w_ada, b_ada)

    def mod_vec(l, k):
        return mod[l, :, k * d:(k + 1) * d].reshape(8, 1, d)

    cos, sin = _rope_tables(n_lat, n_ctx)
    w_in_b, w_gate_b, w_branch_b, w_out_b = (w.astype(bf16) for w in (w_in, w_gate, w_branch, w_out))
    w_glu_b, w_q_b, u_tab_b, v_tab_b = (w.astype(bf16) for w in (s5_w_glu, peer_w_q, peer_u, peer_v))
    stream = (x, ctx)
    um = _mod_call(stream, n_tok, mod_vec(0, 1), mod_vec(0, 0), geom)

    qkv_cols = N_QKV_SLICES * HEAD_DIM
    for l in range(depth):
        z = _mm_call(um, w_in_b, l, f32, name="in_proj")
        qkv = _prep_call(z, cos, sin, a_q_norm[l], a_k_norm[l], n_batch, seq_tot)
        ya = _attn_a_call(qkv, n_ctx).reshape(n_tok, mix_w)
        yb = _attn_b_call(qkv, b_sink[l], n_ctx).reshape(n_tok, mix_w)

        lru_x_block = qkv_cols // mix_w
        a_coef, b_coef = _lru_gate_call(
            z, lru_x_block, lru_conv_w[l], lru_conv_b[l], lru_w_r[l].astype(bf16), lru_b_r[l],
            lru_w_i[l].astype(bf16), lru_b_i[l], jax.nn.softplus(-lru_lambda[l]), tiles_per_batch)
        hf, hr = _lru_scan_call(a_coef, b_coef, n_batch, tiles_per_batch)
        yr = _lru_out_call(hf, hr, z, lru_x_block + 1)

        s5_block0 = (qkv_cols + 2 * mix_w) // 128
        k_cmp, w_cmp, m_cmp, lam_r = _s5_weights(s5_a_re[l], s5_a_im[l], s5_log_step[l], s5_b_re[l],
                                                  s5_b_im[l], s5_c_re[l], s5_c_im[l])
        e_parts = _s5_inject_call(z, s5_block0, w_cmp, s5_steps)
        h_parts = _s5_scan_call(e_parts, lam_r, n_batch, n_ctx // S5_CHUNK)
        y_s5 = _s5_read_call(z, s5_block0, h_parts, k_cmp, m_cmp, s5_d[l], s5_steps)
        ys = _glu_call(y_s5, w_glu_b, l, s5_b_glu[l])

        merged = _merge_call(um, (ya, yb, yr, ys), w_gate_b, b_gate[l], w_branch_b, l)
        y = _mm_call(merged, w_out_b, l, f32, name="out_proj")
        xs, um = _ln_call(stream, y, mod_vec(l, 2), ln_g[l, 0], ln_b[l, 0], (mod_vec(l, 4), mod_vec(l, 3)), geom, alpha)

        q = _mm_call(um, w_q_b, l, bf16, name="peer_query")
        n_heads = peer_sub_keys.shape[1]
        keys = peer_sub_keys[l].astype(bf16).reshape(2 * n_heads, PEER_KEYS, -1)
        s0, s1, e0, e1, th = _peer_route_call(q, keys)
        f = _peer_dense_call(um, u_tab_b, v_tab_b, l, s0, e0, s1, e1, th)
        if l == depth - 1:
            return _ln_call((xs,), f, mod_vec(l, 5), ln_g[l, 1], ln_b[l, 1], None, geom, alpha)[0]
        xs, um = _ln_call((xs,), f, mod_vec(l, 5), ln_g[l, 1], ln_b[l, 1],
                          (mod_vec(l + 1, 1), mod_vec(l + 1, 0)), geom, alpha)
        stream = (xs,)
```

```python
import functools
import math

import jax
import jax.numpy as jnp
from jax import lax
from jax.experimental import pallas as pl
from jax.experimental.pallas import tpu as pltpu

f32 = jnp.float32
bf16 = jnp.bfloat16

HEAD_DIM = 128
GRID_W = 64
ROPE_THETA = 10000.0
WINDOW = 128
Q_TILE = 128
KV_GROUP = 4
LRU_C = 8.0
LRU_BLOCKS = 8
S5_GROUP = 16
S5_STATE = 64
S5_CHUNK = 8
S5_OCT = 8
PEER_TOPK = 16
PEER_KEYS = 128
LN_EPS = 1e-6
NEG = -1e30
ROW_TILE = 256
MM_TILE_M = 512
PROJ_TILE_M = 1536
V7X_VMEM_LIMIT = 56 * 1024 * 1024


def _params(*sem, vmem=V7X_VMEM_LIMIT):
    return pltpu.CompilerParams(dimension_semantics=sem, vmem_limit_bytes=vmem)


def _dot(a, b):
    return jnp.dot(a, b, preferred_element_type=f32)


def _dot_nt(a, b):
    return lax.dot_general(a, b, (((1,), (1,)), ((), ())), preferred_element_type=f32)


def _gelu(x):
    return x * (0.5 * (1.0 + jnp.tanh(math.sqrt(2.0 / math.pi) * (x + 0.044715 * (x * x * x)))))


def _sigmoid(x):
    return 0.5 * (1.0 + jnp.tanh(0.5 * x))


def _ada_kernel(c_ref, w_ref, b_ref, o_ref):
    c = c_ref[...]
    a = (c * _sigmoid(c)).astype(bf16)
    o_ref[0] = _dot(a, w_ref[0].astype(bf16)) + b_ref[0]


def _ada_call(cin, w_ada, b_ada, tn=512):
    depth, d, n = w_ada.shape
    return pl.pallas_call(
        _ada_kernel,
        grid=(depth, n // tn),
        in_specs=[pl.BlockSpec((8, d), lambda l, j: (0, 0)),
                  pl.BlockSpec((1, d, tn), lambda l, j: (l, 0, j)),
                  pl.BlockSpec((1, 1, tn), lambda l, j: (l, 0, j))],
        out_specs=pl.BlockSpec((1, 8, tn), lambda l, j: (l, 0, j)),
        out_shape=jax.ShapeDtypeStruct((depth, 8, n), f32),
        compiler_params=_params("parallel", "parallel"),
        name="adaln",
    )(cin, w_ada, b_ada.reshape(depth, 1, n))


def _row_group(i, tiles_per_batch, ctx_tiles, n_batch):
    return jnp.where(i % tiles_per_batch < ctx_tiles, n_batch, i // tiles_per_batch)


def _stream_specs(stream, geom, d):
    tpb = geom[0]
    if len(stream) == 1:
        return [pl.BlockSpec((ROW_TILE, d), lambda i: (i, 0))]
    return [pl.BlockSpec((1, ROW_TILE, d), lambda i: (i // tpb, jnp.maximum(i % tpb - 1, 0), 0)),
            pl.BlockSpec((1, ROW_TILE, d), lambda i: (i // tpb, 0, 0))]


def _read_stream(refs, is_ctx):
    if len(refs) == 1:
        return refs[0][...]
    return jnp.where(is_ctx, refs[1][0], refs[0][0])


def _mod_kernel(*refs, n_stream, tiles_per_batch):
    sc_ref, sh_ref, o_ref = refs[n_stream:]
    x = _read_stream(refs[:n_stream], pl.program_id(0) % tiles_per_batch == 0)
    o_ref[...] = (x * (1.0 + sc_ref[0]) + sh_ref[0]).astype(o_ref.dtype)


def _mod_call(stream, n, sc, sh, geom):
    d = sc.shape[-1]
    tpb, ct, nb = geom
    sel = lambda i: (_row_group(i, tpb, ct, nb), 0, 0)
    return pl.pallas_call(
        functools.partial(_mod_kernel, n_stream=len(stream), tiles_per_batch=tpb),
        grid=(n // ROW_TILE,),
        in_specs=_stream_specs(stream, geom, d) + [pl.BlockSpec((1, 1, d), sel), pl.BlockSpec((1, 1, d), sel)],
        out_specs=pl.BlockSpec((ROW_TILE, d), lambda i: (i, 0)),
        out_shape=jax.ShapeDtypeStruct((n, d), bf16),
        compiler_params=_params("parallel"),
        name="modulate",
    )(*stream, sc, sh)


def _ln_kernel(*refs, n_stream, tiles_per_batch, alpha, last):
    y_ref, g_ref, lg_ref, lb_ref = refs[n_stream:n_stream + 4]
    is_ctx = pl.program_id(0) % tiles_per_batch == 0
    v = alpha * _read_stream(refs[:n_stream], is_ctx) + g_ref[0] * y_ref[...]
    mu = jnp.mean(v, axis=-1, keepdims=True)
    vc = v - mu
    var = jnp.mean(vc * vc, axis=-1, keepdims=True)
    o = vc * lax.rsqrt(var + LN_EPS) * lg_ref[...] + lb_ref[...]
    if last:
        xo_ref, = refs[n_stream + 4:]

        @pl.when(jnp.logical_not(is_ctx))
        def _():
            xo_ref[0] = o
    else:
        sc_ref, sh_ref, xo_ref, xm_ref = refs[n_stream + 4:]
        xo_ref[...] = o
        xm_ref[...] = (o * (1.0 + sc_ref[0]) + sh_ref[0]).astype(xm_ref.dtype)


def _ln_call(stream, y, gate, ln_g, ln_b, nxt_mod, geom, alpha):
    n, d = y.shape
    tpb, ct, nb = geom
    sel = lambda i: (_row_group(i, tpb, ct, nb), 0, 0)
    row = pl.BlockSpec((ROW_TILE, d), lambda i: (i, 0))
    vec = pl.BlockSpec((1, d), lambda i: (0, 0))
    mod = pl.BlockSpec((1, 1, d), sel)
    last = nxt_mod is None
    in_specs = _stream_specs(stream, geom, d) + [row, mod, vec, vec]
    args = list(stream) + [y, gate, ln_g.reshape(1, d), ln_b.reshape(1, d)]
    if last:
        out_specs = [pl.BlockSpec((1, ROW_TILE, d), lambda i: (i // tpb, jnp.maximum(i % tpb - 1, 0), 0))]
        out_shape = [jax.ShapeDtypeStruct((nb, n // nb - ct * ROW_TILE, d), f32)]
    else:
        in_specs += [mod, mod]
        args += list(nxt_mod)
        out_specs = [row, row]
        out_shape = [jax.ShapeDtypeStruct((n, d), f32), jax.ShapeDtypeStruct((n, d), bf16)]
    return pl.pallas_call(
        functools.partial(_ln_kernel, n_stream=len(stream), tiles_per_batch=tpb, alpha=alpha, last=last),
        grid=(n // ROW_TILE,),
        in_specs=in_specs, out_specs=out_specs, out_shape=out_shape,
        compiler_params=_params("arbitrary" if last else "parallel"),
        name="residual_ln",
    )(*args)


def _mm_kernel(a_ref, b_ref, o_ref):
    o_ref[...] = _dot(a_ref[...], b_ref[...]).astype(o_ref.dtype)


def _mm_call(a, b, layer, out_dtype, tn=512, name="matmul"):
    m, k = a.shape
    n = b.shape[2]
    tm = PROJ_TILE_M if m % PROJ_TILE_M == 0 else MM_TILE_M
    return pl.pallas_call(
        _mm_kernel,
        grid=(m // tm, n // tn),
        in_specs=[pl.BlockSpec((tm, k), lambda i, j: (i, 0)),
                  pl.BlockSpec((None, k, tn), lambda i, j: (layer, 0, j))],
        out_specs=pl.BlockSpec((tm, tn), lambda i, j: (i, j)),
        out_shape=jax.ShapeDtypeStruct((m, n), out_dtype),
        compiler_params=_params("parallel", "arbitrary"),
        name=name,
    )(a, b)


def _glu_kernel(a_ref, wv_ref, wg_ref, bv_ref, bg_ref, o_ref):
    a = a_ref[...].astype(bf16)
    val = _dot(a, wv_ref[...]) + bv_ref[...]
    gate = _dot(a, wg_ref[...]) + bg_ref[...]
    o_ref[...] = (val * _sigmoid(gate)).astype(o_ref.dtype)


def _glu_call(a, w, layer, b, tn=512, tm=MM_TILE_M):
    m, k = a.shape
    n = w.shape[2] // 2
    nj = n // tn
    b2 = b.reshape(1, 2 * n)
    return pl.pallas_call(
        _glu_kernel,
        grid=(m // tm, nj),
        in_specs=[pl.BlockSpec((tm, k), lambda i, j: (i, 0)),
                  pl.BlockSpec((None, k, tn), lambda i, j: (layer, 0, j)),
                  pl.BlockSpec((None, k, tn), lambda i, j: (layer, 0, j + nj)),
                  pl.BlockSpec((1, tn), lambda i, j: (0, j)),
                  pl.BlockSpec((1, tn), lambda i, j: (0, j + nj))],
        out_specs=pl.BlockSpec((tm, tn), lambda i, j: (i, j)),
        out_shape=jax.ShapeDtypeStruct((m, n), bf16),
        compiler_params=_params("parallel", "arbitrary"),
        name="s5_glu",
    )(a, w, w, b2, b2)


def _merge_kernel(u_ref, ya_ref, yb_ref, yr_ref, ys_ref, wg_ref, bg_ref, wb_ref, o_ref):
    u = u_ref[...]
    acc = None
    for i, y_ref in enumerate((ya_ref, yb_ref, yr_ref, ys_ref)):
        gate = _sigmoid(_dot(u, wg_ref[i]) + bg_ref[i])
        term = gate * _dot(y_ref[...], wb_ref[i])
        acc = term if acc is None else acc + term
    o_ref[...] = acc.astype(o_ref.dtype)


def _merge_call(u, branches, w_gate, b_gate, w_branch, layer, tn=256, tm=MM_TILE_M):
    m, d = u.shape
    _, nbr, bw, n = w_branch.shape
    ybr = pl.BlockSpec((tm, bw), lambda i, j: (i, 0))
    return pl.pallas_call(
        _merge_kernel,
        grid=(m // tm, n // tn),
        in_specs=[pl.BlockSpec((tm, d), lambda i, j: (i, 0)), ybr, ybr, ybr, ybr,
                  pl.BlockSpec((None, nbr, d, tn), lambda i, j: (layer, 0, 0, j)),
                  pl.BlockSpec((nbr, 1, tn), lambda i, j: (0, 0, j)),
                  pl.BlockSpec((None, nbr, bw, tn), lambda i, j: (layer, 0, 0, j))],
        out_specs=pl.BlockSpec((tm, tn), lambda i, j: (i, j)),
        out_shape=jax.ShapeDtypeStruct((m, n), bf16),
        compiler_params=_params("parallel", "arbitrary"),
        name="branch_merge",
    )(u, *branches, w_gate, b_gate.reshape(nbr, 1, n), w_branch)


def _rope_tables(n_lat, n_ctx):
    t = jnp.arange(n_lat)
    row = (t // GRID_W).astype(f32)
    col = (t % GRID_W).astype(f32)
    n_freq = HEAD_DIM // 4
    inv = ROPE_THETA ** (-jnp.arange(n_freq, dtype=f32) / n_freq)
    ang = jnp.concatenate([row[:, None] * inv, col[:, None] * inv], axis=-1)
    cos = jnp.repeat(jnp.cos(ang), 2, axis=-1)
    sin = jnp.repeat(jnp.sin(ang), 2, axis=-1) * jnp.tile(jnp.array([-1.0, 1.0], f32), HEAD_DIM // 2)
    cos = jnp.concatenate([jnp.ones((n_ctx, HEAD_DIM), f32), cos], axis=0)
    sin = jnp.concatenate([jnp.zeros((n_ctx, HEAD_DIM), f32), sin], axis=0)
    return cos, sin


N_QKV_SLICES = 24
A_Q, A_K, A_V, B_Q, B_K, B_V = 0, 8, 10, 12, 20, 22


def _prep_kernel(z_ref, cos_ref, sin_ref, gq_ref, gk_ref, o_ref):
    cos, sin = cos_ref[...], sin_ref[...]
    even = lax.broadcasted_iota(jnp.int32, cos.shape, 1) % 2 == 0
    for s in range(N_QKV_SLICES):
        y = z_ref[:, s * HEAD_DIM:(s + 1) * HEAD_DIM]
        if s < A_V:
            gain = gq_ref[...] if s < A_K else gk_ref[...]
            y = y * lax.rsqrt(jnp.mean(y * y, axis=-1, keepdims=True) + LN_EPS) * gain
        if s < A_V or B_Q <= s < B_V:
            partner = jnp.where(even, pltpu.roll(y, HEAD_DIM - 1, axis=1), pltpu.roll(y, 1, axis=1))
            y = y * cos + partner * sin
        if s < A_K or B_Q <= s < B_K:
            y = y * (HEAD_DIM ** -0.5)
        o_ref[0, s] = y.astype(o_ref.dtype)


def _prep_call(z, cos, sin, gain_q, gain_k, n_batch, seq_tot):
    n = z.shape[0]
    tr = ROW_TILE
    tpb = seq_tot // tr
    width = N_QKV_SLICES * HEAD_DIM
    vec = pl.BlockSpec((1, HEAD_DIM), lambda i: (0, 0))
    return pl.pallas_call(
        _prep_kernel,
        grid=(n // tr,),
        in_specs=[pl.BlockSpec((tr, width), lambda i: (i, 0)),
                  pl.BlockSpec((tr, HEAD_DIM), lambda i: (i % tpb, 0)),
                  pl.BlockSpec((tr, HEAD_DIM), lambda i: (i % tpb, 0)), vec, vec],
        out_specs=pl.BlockSpec((1, N_QKV_SLICES, tr, HEAD_DIM), lambda i: (i // tpb, 0, i % tpb, 0)),
        out_shape=jax.ShapeDtypeStruct((n_batch, N_QKV_SLICES, seq_tot, HEAD_DIM), bf16),
        compiler_params=_params("parallel"),
        name="qkv_prep",
    )(z, cos, sin, gain_q.reshape(1, HEAD_DIM), gain_k.reshape(1, HEAD_DIM))


def _attn_a_kernel(q_ref, k_ref, v_ref, o_ref, m_s, acc_s, *, n_ctx, tk, n_lat_chunks):
    q = q_ref[0].reshape(KV_GROUP * Q_TILE, HEAD_DIM)

    def values(lo, n):
        return jnp.concatenate([v_ref[0, 0, lo:lo + n, :], jnp.ones((n, HEAD_DIM), bf16)], axis=1)

    s = _dot_nt(q, k_ref[0, 0, 0:n_ctx, :])
    m0 = jnp.max(s, axis=-1, keepdims=True)
    m_s[...] = m0
    acc_s[...] = _dot(jnp.exp((s - m0).astype(bf16)), values(0, n_ctx))

    @pl.when(pl.program_id(2) >= n_ctx // Q_TILE)
    def _():
        for c in range(n_lat_chunks):
            lo = n_ctx + c * tk
            s = _dot_nt(q, k_ref[0, 0, lo:lo + tk, :])
            m_prev = m_s[...]
            m_new = jnp.maximum(m_prev, jnp.max(s, axis=-1, keepdims=True))
            p = jnp.exp((s - m_new).astype(bf16))
            acc_s[...] = jnp.exp(m_prev - m_new) * acc_s[...] + _dot(p, values(lo, tk))
            m_s[...] = m_new

    out = acc_s[:, 0:HEAD_DIM] / acc_s[:, HEAD_DIM:2 * HEAD_DIM]
    for g in range(KV_GROUP):
        o_ref[0, :, g * HEAD_DIM:(g + 1) * HEAD_DIM] = out[g * Q_TILE:(g + 1) * Q_TILE].astype(o_ref.dtype)


def _attn_a_call(qkv, n_ctx, tk=1024):
    nb, _, seq_tot, _ = qkv.shape
    n_kv = (A_V - A_K)
    rows = KV_GROUP * Q_TILE
    n_lat = seq_tot - n_ctx
    tk = min(tk, n_lat)
    assert n_lat % tk == 0
    kern = functools.partial(_attn_a_kernel, n_ctx=n_ctx, tk=tk, n_lat_chunks=n_lat // tk)
    return pl.pallas_call(
        kern,
        grid=(nb, n_kv, seq_tot // Q_TILE),
        in_specs=[pl.BlockSpec((1, KV_GROUP, Q_TILE, HEAD_DIM), lambda b, h, i: (b, A_Q // KV_GROUP + h, i, 0)),
                  pl.BlockSpec((1, 1, seq_tot, HEAD_DIM), lambda b, h, i: (b, A_K + h, 0, 0)),
                  pl.BlockSpec((1, 1, seq_tot, HEAD_DIM), lambda b, h, i: (b, A_V + h, 0, 0))],
        out_specs=pl.BlockSpec((1, Q_TILE, KV_GROUP * HEAD_DIM), lambda b, h, i: (b, i, h)),
        out_shape=jax.ShapeDtypeStruct((nb, seq_tot, n_kv * KV_GROUP * HEAD_DIM), bf16),
        scratch_shapes=[pltpu.VMEM((rows, 1), f32), pltpu.VMEM((rows, 2 * HEAD_DIM), f32)],
        compiler_params=_params("parallel", "parallel", "arbitrary"),
        name="dense_attention",
    )(qkv, qkv, qkv)


def _attn_b_kernel(q_ref, k_ref, v_ref, sink_ref, o_ref, *, n_ctx, n_blk):
    i = pl.program_id(2)
    blk = i - n_ctx // Q_TILE
    rows = KV_GROUP * Q_TILE
    q = q_ref[0].reshape(rows, HEAD_DIM)
    sink = sink_ref[0]
    kc = k_ref[0, 0, 0:n_ctx, :]
    vc = v_ref[0, 0, 0:n_ctx, :]
    s_ctx = _dot_nt(q, kc)
    m = jnp.maximum(jnp.max(s_ctx, axis=-1, keepdims=True), sink)
    span = 3 * Q_TILE
    first = jnp.clip(blk - 1, 0, n_blk - 3) * Q_TILE
    start = pl.multiple_of(n_ctx + first, Q_TILE)
    q_pos = blk * Q_TILE + lax.broadcasted_iota(jnp.int32, (rows, span), 0) % Q_TILE
    k_pos = first + lax.broadcasted_iota(jnp.int32, (rows, span), 1)
    keep = jnp.logical_and(jnp.abs(k_pos - q_pos) <= WINDOW, blk >= 0)
    s_loc = jnp.where(keep, _dot_nt(q, k_ref[0, 0, pl.ds(start, span), :]), NEG)
    m = jnp.maximum(m, jnp.max(s_loc, axis=-1, keepdims=True))

    def values(v):
        return jnp.concatenate([v, jnp.ones(v.shape, bf16)], axis=1)

    acc = _dot(jnp.exp((s_ctx - m).astype(bf16)), values(vc))
    acc = acc + _dot(jnp.exp((s_loc - m).astype(bf16)), values(v_ref[0, 0, pl.ds(start, span), :]))
    out = acc[:, 0:HEAD_DIM] / (jnp.exp(sink - m) + acc[:, HEAD_DIM:2 * HEAD_DIM])
    for g in range(KV_GROUP):
        o_ref[0, :, g * HEAD_DIM:(g + 1) * HEAD_DIM] = out[g * Q_TILE:(g + 1) * Q_TILE].astype(o_ref.dtype)


def _attn_b_call(qkv, sink, n_ctx):
    nb, _, seq_tot, _ = qkv.shape
    n_kv = (B_V - B_K)
    rows = KV_GROUP * Q_TILE
    sink_rows = jnp.repeat(sink.astype(f32).reshape(n_kv, KV_GROUP), Q_TILE, axis=1).reshape(n_kv, rows, 1)
    n_blk = (seq_tot - n_ctx) // Q_TILE
    assert n_blk >= 3, "the windowed mixer reads a span of three key blocks"
    kern = functools.partial(_attn_b_kernel, n_ctx=n_ctx, n_blk=n_blk)
    return pl.pallas_call(
        kern,
        grid=(nb, n_kv, seq_tot // Q_TILE),
        in_specs=[pl.BlockSpec((1, KV_GROUP, Q_TILE, HEAD_DIM), lambda b, h, i: (b, B_Q // KV_GROUP + h, i, 0)),
                  pl.BlockSpec((1, 1, seq_tot, HEAD_DIM), lambda b, h, i: (b, B_K + h, 0, 0)),
                  pl.BlockSpec((1, 1, seq_tot, HEAD_DIM), lambda b, h, i: (b, B_V + h, 0, 0)),
                  pl.BlockSpec((1, rows, 1), lambda b, h, i: (h, 0, 0))],
        out_specs=pl.BlockSpec((1, Q_TILE, KV_GROUP * HEAD_DIM), lambda b, h, i: (b, i, h)),
        out_shape=jax.ShapeDtypeStruct((nb, seq_tot, n_kv * KV_GROUP * HEAD_DIM), bf16),
        compiler_params=_params("parallel", "parallel", "arbitrary"),
        name="window_attention",
    )(qkv, qkv, qkv, sink_rows)


HALO = 8


def _lru_gate_kernel(x_ref, xp_ref, xn_ref, cw_ref, cb_ref, wr_ref, br_ref, wi_ref, bi_ref, sp_ref,
                     a_ref, b_ref, pad_s, *, tiles_per_batch):
    tr = x_ref.shape[0]
    t = pl.program_id(0) % tiles_per_batch
    has_prev = t >= 2
    has_next = jnp.logical_and(t >= 1, t < tiles_per_batch - 1)
    x = x_ref[...]
    pad_s[0:HALO, :] = jnp.where(has_prev, xp_ref[...], 0.0)
    pad_s[HALO:HALO + tr, :] = x
    pad_s[HALO + tr:2 * HALO + tr, :] = jnp.where(has_next, xn_ref[...], 0.0)
    cw = cw_ref[...]
    xl = cb_ref[...] + pad_s[HALO - 2:HALO - 2 + tr, :] * cw[0:1]
    xl = xl + pad_s[HALO - 1:HALO - 1 + tr, :] * cw[1:2]
    xl = xl + x * cw[2:3]
    xl = xl + pad_s[HALO + 1:HALO + 1 + tr, :] * cw[3:4]
    bw = xl.shape[1] // LRU_BLOCKS
    for n in range(LRU_BLOCKS):
        cols = slice(n * bw, (n + 1) * bw)
        xb = xl[:, cols]
        xb16 = xb.astype(bf16)
        for d in range(2):
            r = _sigmoid(_dot(xb16, wr_ref[d, n]) + br_ref[d:d + 1, cols])
            gi = _sigmoid(_dot(xb16, wi_ref[d, n]) + bi_ref[d:d + 1, cols])
            log_a = (-LRU_C) * r * sp_ref[d:d + 1, cols]
            a = jnp.exp(log_a)
            a_ref[d, :, cols] = a
            b_ref[d, :, cols] = jnp.sqrt(1.0 - a * a) * gi * xb


def _lru_gate_call(z, col_block, conv_w, conv_b, w_r, b_r, w_i, b_i, softplus_lam, tiles_per_batch):
    n = z.shape[0]
    w = conv_w.shape[1]
    tr = ROW_TILE
    hb = tr // HALO
    n_halo = n // HALO
    full = lambda shape: pl.BlockSpec(shape, lambda i: (0,) * len(shape))
    kern = functools.partial(_lru_gate_kernel, tiles_per_batch=tiles_per_batch)
    out = jax.ShapeDtypeStruct((2, n, w), f32)
    return pl.pallas_call(
        kern,
        grid=(n // tr,),
        in_specs=[pl.BlockSpec((tr, w), lambda i: (i, col_block)),
                  pl.BlockSpec((HALO, w), lambda i: (jnp.maximum(i * hb - 1, 0), col_block)),
                  pl.BlockSpec((HALO, w), lambda i: (jnp.minimum((i + 1) * hb, n_halo - 1), col_block)),
                  full(conv_w.shape), full((1, w)), full(w_r.shape), full(b_r.shape),
                  full(w_i.shape), full(b_i.shape), full(softplus_lam.shape)],
        out_specs=[pl.BlockSpec((2, tr, w), lambda i: (0, i, 0)), pl.BlockSpec((2, tr, w), lambda i: (0, i, 0))],
        out_shape=[out, out],
        scratch_shapes=[pltpu.VMEM((tr + 2 * HALO, w), f32)],
        compiler_params=_params("parallel"),
        name="lru_gates",
    )(z, z, z, conv_w, conv_b.reshape(1, w), w_r, b_r, w_i, b_i, softplus_lam)


def _lru_scan_kernel(af_ref, bf_ref, ar_ref, br_ref, hf_ref, hr_ref, sf, sr):
    tr = hf_ref.shape[0]

    @pl.when(pl.program_id(1) == 0)
    def _():
        sf[...] = jnp.zeros(sf.shape, f32)
        sr[...] = jnp.zeros(sr.shape, f32)

    def body(t, carry):
        hf, hr = carry
        hf = af_ref[0, pl.ds(t, 1), :] * hf + bf_ref[0, pl.ds(t, 1), :]
        hf_ref[pl.ds(t, 1), :] = hf
        u = tr - 1 - t
        hr = ar_ref[0, pl.ds(u, 1), :] * hr + br_ref[0, pl.ds(u, 1), :]
        hr_ref[pl.ds(u, 1), :] = hr
        return hf, hr

    hf, hr = lax.fori_loop(0, tr, body, (sf[...], sr[...]), unroll=8)
    sf[...] = hf
    sr[...] = hr


def _scan_tile_maps(tiles_per_batch):
    fwd = lambda b, i: b * tiles_per_batch + i
    rev = lambda b, i: b * tiles_per_batch + jnp.where(i == 0, 0, tiles_per_batch - i)
    return fwd, rev


def _lru_scan_call(a, b, n_batch, tiles_per_batch):
    _, n, w = a.shape
    tr = ROW_TILE
    fwd, rev = _scan_tile_maps(tiles_per_batch)
    out = jax.ShapeDtypeStruct((n, w), f32)
    return pl.pallas_call(
        _lru_scan_kernel,
        grid=(n_batch, tiles_per_batch),
        in_specs=[pl.BlockSpec((1, tr, w), lambda bb, i: (0, fwd(bb, i), 0)),
                  pl.BlockSpec((1, tr, w), lambda bb, i: (0, fwd(bb, i), 0)),
                  pl.BlockSpec((1, tr, w), lambda bb, i: (1, rev(bb, i), 0)),
                  pl.BlockSpec((1, tr, w), lambda bb, i: (1, rev(bb, i), 0))],
        out_specs=[pl.BlockSpec((tr, w), lambda bb, i: (fwd(bb, i), 0)),
                   pl.BlockSpec((tr, w), lambda bb, i: (rev(bb, i), 0))],
        out_shape=[out, out],
        scratch_shapes=[pltpu.VMEM((1, w), f32), pltpu.VMEM((1, w), f32)],
        compiler_params=_params("parallel", "arbitrary"),
        name="lru_scan",
    )(a, b, a, b)


def _lru_out_kernel(hf_ref, hr_ref, g_ref, o_ref):
    o_ref[...] = ((hf_ref[...] + hr_ref[...]) * _gelu(g_ref[...])).astype(o_ref.dtype)


def _lru_out_call(hf, hr, z, gate_col_block):
    n, w = hf.shape
    tr = ROW_TILE
    row = pl.BlockSpec((tr, w), lambda i: (i, 0))
    return pl.pallas_call(
        _lru_out_kernel,
        grid=(n // tr,),
        in_specs=[row, row, pl.BlockSpec((tr, w), lambda i: (i, gate_col_block))],
        out_specs=row,
        out_shape=jax.ShapeDtypeStruct((n, w), bf16),
        compiler_params=_params("parallel"),
        name="lru_out",
    )(hf, hr, z)


def _s5_weights(a_re, a_im, log_step, b_re, b_im, c_re, c_im):
    r = S5_CHUNK
    n_g = a_re.shape[1]
    n_oct = n_g // S5_OCT
    step = jnp.exp(log_step)[..., None]
    mag = jnp.exp(a_re * step)
    lb_re, lb_im = mag * jnp.cos(a_im * step), mag * jnp.sin(a_im * step)
    den = a_re * a_re + a_im * a_im
    num_re = lb_re - 1.0
    coef_re = (num_re * a_re + lb_im * a_im) / den
    coef_im = (lb_im * a_re - num_re * a_im) / den
    bb_re = coef_re[..., None] * b_re - coef_im[..., None] * b_im
    bb_im = coef_re[..., None] * b_im + coef_im[..., None] * b_re
    tau = jnp.arange(r + 1, dtype=f32)[:, None, None, None]
    pmag = jnp.exp(tau * (a_re * step))
    pw_re, pw_im = pmag * jnp.cos(tau * (a_im * step)), pmag * jnp.sin(tau * (a_im * step))

    lb_b_re = pw_re[..., None] * bb_re - pw_im[..., None] * bb_im
    lb_b_im = pw_re[..., None] * bb_im + pw_im[..., None] * bb_re
    lag = (jnp.einsum('tdgpc,dgop->tdgco', lb_b_re, c_re) - jnp.einsum('tdgpc,dgop->tdgco', lb_b_im, c_im))
    idx = jnp.arange(r)
    sh_f = (idx[None, None, :] - idx[None, :, None] == jnp.arange(r + 1)[:, None, None]).astype(f32)
    sh_r = (idx[None, :, None] - idx[None, None, :] == jnp.arange(r + 1)[:, None, None]).astype(f32)
    k_loc = jnp.einsum('tio,tgcd->gicod', sh_f, lag[:, 0]) + jnp.einsum('tio,tgcd->gicod', sh_r, lag[:, 1])
    k_cmp = k_loc.reshape(n_oct, S5_OCT, r, S5_GROUP, r * S5_GROUP)
    k_cmp = k_cmp.transpose(0, 2, 1, 3, 4).reshape(n_oct, r * 128, r * S5_GROUP)

    def inject(d, powers):
        w_re = pw_re[powers, d][..., None] * bb_re[d] - pw_im[powers, d][..., None] * bb_im[d]
        w_im = pw_re[powers, d][..., None] * bb_im[d] + pw_im[powers, d][..., None] * bb_re[d]
        out = []
        for w in (w_re, w_im):
            w = w.reshape(r, n_oct, S5_OCT, S5_STATE, S5_GROUP).transpose(1, 0, 2, 4, 3)
            out.append(w.reshape(n_oct, r * 128, S5_STATE))
        return out
    w_cmp = jnp.concatenate(inject(0, idx[::-1]) + inject(1, idx), axis=-1)

    def readout(d, powers):
        cl_re = c_re[d][None] * pw_re[powers, d][:, :, None, :] - c_im[d][None] * pw_im[powers, d][:, :, None, :]
        cl_im = c_re[d][None] * pw_im[powers, d][:, :, None, :] + c_im[d][None] * pw_re[powers, d][:, :, None, :]
        out = []
        for w in (cl_re, -cl_im):
            w = w.reshape(r, n_oct, S5_OCT, S5_GROUP, S5_STATE).transpose(1, 2, 4, 0, 3)
            out.append(w.reshape(n_oct, S5_OCT * S5_STATE, r * S5_GROUP))
        return out
    m_cmp = jnp.concatenate(readout(0, idx + 1) + readout(1, r - idx), axis=1)

    lam_r = jnp.stack([pw_re[r, 0].reshape(-1), pw_im[r, 0].reshape(-1),
                       pw_re[r, 1].reshape(-1), pw_im[r, 1].reshape(-1)])
    return k_cmp.astype(bf16), w_cmp.astype(bf16), m_cmp.astype(bf16), lam_r


def _expand_octet(compact, inner, row_inner):
    n_rows, n_cols = compact.shape
    wide = n_cols * S5_OCT
    log_inner, log_row = inner.bit_length() - 1, row_inner.bit_length() - 1
    oct_bits = S5_OCT.bit_length() - 1
    src = lax.broadcasted_iota(jnp.int32, (n_cols, wide), 0)
    dst = lax.broadcasted_iota(jnp.int32, (n_cols, wide), 1)
    dst_compact = ((dst >> (log_inner + oct_bits)) << log_inner) + (dst & (inner - 1))
    spread = _dot(compact, (src == dst_compact).astype(compact.dtype))
    row_g = (lax.broadcasted_iota(jnp.int32, (n_rows, wide), 0) >> log_row) & (S5_OCT - 1)
    col_g = (lax.broadcasted_iota(jnp.int32, (n_rows, wide), 1) >> log_inner) & (S5_OCT - 1)
    return jnp.where(row_g == col_g, spread, 0.0).astype(compact.dtype)


def _chunk_rows(u_ref):
    n_rows = u_ref.shape[0] // S5_CHUNK
    return [u_ref[pl.ds(i, n_rows, stride=S5_CHUNK), :] for i in range(S5_CHUNK)]


def _s5_inject_kernel(u_ref, w_ref, efr_ref, efi_ref, err_ref, eri_ref, w_s):
    @pl.when(pl.program_id(1) == 0)
    def _():
        w_s[...] = _expand_octet(w_ref[0], S5_STATE, S5_GROUP)

    x = jnp.concatenate(_chunk_rows(u_ref), axis=1).astype(bf16)
    e = _dot(x, w_s[...])
    w = efr_ref.shape[1]
    for k, o_ref in enumerate((efr_ref, efi_ref, err_ref, eri_ref)):
        o_ref[...] = e[:, k * w:(k + 1) * w]


def _s5_inject_call(z, col_block0, w_cmp, steps):
    n = z.shape[0]
    n_oct, n_in, n_cmp = w_cmp.shape
    sw = S5_OCT * S5_STATE
    rows = steps // S5_CHUNK
    out = jax.ShapeDtypeStruct((n // S5_CHUNK, n_oct * sw), f32)
    ospec = pl.BlockSpec((rows, sw), lambda k, i: (i, k))
    return pl.pallas_call(
        _s5_inject_kernel,
        grid=(n_oct, n // steps),
        in_specs=[pl.BlockSpec((steps, 128), lambda k, i: (i, col_block0 + k)),
                  pl.BlockSpec((1, n_in, n_cmp), lambda k, i: (k, 0, 0))],
        out_specs=[ospec] * 4,
        out_shape=[out] * 4,
        scratch_shapes=[pltpu.VMEM((n_in, n_cmp * S5_OCT), bf16)],
        compiler_params=_params("arbitrary", "arbitrary"),
        name="s5_inject",
    )(z, w_cmp)


def _s5_scan_kernel(efr_ref, efi_ref, err_ref, eri_ref, lam_ref, hfr_ref, hfi_ref, hrr_ref, hri_ref, *, n_ctx):
    n_rows = efr_ref.shape[0]
    lfr, lfi, lrr, lri = (lam_ref[k:k + 1, :] for k in range(4))
    zero = jnp.zeros((1, efr_ref.shape[1]), f32)

    def body(t, carry):
        fr, fi, rr, ri = carry
        hfr_ref[pl.ds(t, 1), :] = fr
        hfi_ref[pl.ds(t, 1), :] = fi
        er, ei = efr_ref[pl.ds(t, 1), :], efi_ref[pl.ds(t, 1), :]
        fr, fi = lfr * fr - lfi * fi + er, lfr * fi + lfi * fr + ei
        u = jnp.where(t < n_ctx, n_ctx - 1 - t, n_rows - 1 - (t - n_ctx))
        hrr_ref[pl.ds(u, 1), :] = rr
        hri_ref[pl.ds(u, 1), :] = ri
        er, ei = err_ref[pl.ds(u, 1), :], eri_ref[pl.ds(u, 1), :]
        rr, ri = lrr * rr - lri * ri + er, lrr * ri + lri * rr + ei
        return fr, fi, rr, ri

    lax.fori_loop(0, n_rows, body, (zero, zero, zero, zero), unroll=4)


def _s5_scan_call(e_parts, lam_r, n_batch, n_ctx_rows, lane_block=512):
    n_rows_tot, width = e_parts[0].shape
    rows = n_rows_tot // n_batch
    blk = pl.BlockSpec((rows, lane_block), lambda b, j: (b, j))
    out = jax.ShapeDtypeStruct((n_rows_tot, width), f32)
    return pl.pallas_call(
        functools.partial(_s5_scan_kernel, n_ctx=n_ctx_rows),
        grid=(n_batch, width // lane_block),
        in_specs=[blk] * 4 + [pl.BlockSpec((4, lane_block), lambda b, j: (0, j))],
        out_specs=[blk] * 4,
        out_shape=[out] * 4,
        compiler_params=_params("parallel", "parallel"),
        name="s5_scan",
    )(*e_parts, lam_r)


def _s5_read_kernel(u_ref, hfr_ref, hfi_ref, hrr_ref, hri_ref, k_ref, m_ref, d_ref, o_ref, k_s, m_s):
    @pl.when(pl.program_id(1) == 0)
    def _():
        k_s[...] = _expand_octet(k_ref[0], S5_GROUP, S5_GROUP)
        m_s[...] = _expand_octet(m_ref[0], S5_GROUP, S5_STATE)

    parts = _chunk_rows(u_ref)
    x = jnp.concatenate(parts, axis=1).astype(bf16)
    h = jnp.concatenate([hfr_ref[...], hfi_ref[...], hrr_ref[...], hri_ref[...]], axis=1).astype(bf16)
    y = _dot(x, k_s[...]) + _dot(h, m_s[...])
    n_rows = u_ref.shape[0] // S5_CHUNK
    for i in range(S5_CHUNK):
        o_ref[pl.ds(i, n_rows, stride=S5_CHUNK), :] = y[:, i * 128:(i + 1) * 128] + parts[i] * d_ref[...]


def _s5_read_call(z, col_block0, h_parts, k_cmp, m_cmp, d_skip, steps):
    n = z.shape[0]
    n_oct = k_cmp.shape[0]
    sw = S5_OCT * S5_STATE
    rows = steps // S5_CHUNK
    hspec = pl.BlockSpec((rows, sw), lambda k, i: (i, k))
    return pl.pallas_call(
        _s5_read_kernel,
        grid=(n_oct, n // steps),
        in_specs=[pl.BlockSpec((steps, 128), lambda k, i: (i, col_block0 + k))] + [hspec] * 4 +
                 [pl.BlockSpec((1,) + k_cmp.shape[1:], lambda k, i: (k, 0, 0)),
                  pl.BlockSpec((1,) + m_cmp.shape[1:], lambda k, i: (k, 0, 0)),
                  pl.BlockSpec((1, 128), lambda k, i: (0, k))],
        out_specs=pl.BlockSpec((steps, 128), lambda k, i: (i, k)),
        out_shape=jax.ShapeDtypeStruct((n, n_oct * 128), f32),
        scratch_shapes=[pltpu.VMEM((k_cmp.shape[1], k_cmp.shape[2] * S5_OCT), bf16),
                        pltpu.VMEM((m_cmp.shape[1], m_cmp.shape[2] * S5_OCT), bf16)],
        compiler_params=_params("arbitrary", "arbitrary"),
        name="s5_readout",
    )(z, *h_parts, k_cmp, m_cmp, d_skip.reshape(1, -1))


def _top_values(s, k):
    vals = []
    for _ in range(k):
        m = jnp.max(s, axis=0, keepdims=True)
        vals.append(m)
        s = jnp.where(s == m, -jnp.inf, s)
    return vals


def _merge_sort_pairs(n):
    pairs = []

    def merge(lo, hi, r):
        step = r * 2
        if step < hi - lo:
            merge(lo, hi, step)
            merge(lo + r, hi, step)
            pairs.extend((i, i + r) for i in range(lo + r, hi - r, step))
        else:
            pairs.append((lo, lo + r))

    def sort(lo, hi):
        if hi - lo >= 1:
            mid = lo + (hi - lo) // 2
            sort(lo, mid)
            sort(mid + 1, hi)
            merge(lo, hi, 1)

    sort(0, n - 1)
    return pairs


def _top_values_sorted(s, k):
    n_grp = s.shape[0] // 8
    lists = [s[8 * v:8 * v + 8, :] for v in range(n_grp)]
    for i, j in _merge_sort_pairs(n_grp):
        lists[i], lists[j] = jnp.maximum(lists[i], lists[j]), jnp.minimum(lists[i], lists[j])
    lists.append(jnp.full_like(lists[0], -jnp.inf))
    vals = []
    for t in range(k):
        m = jnp.max(lists[0], axis=0, keepdims=True)
        vals.append(m)
        hit = lists[0] == m
        for v in range(min(k - t, n_grp)):
            lists[v] = jnp.where(hit, lists[v + 1], lists[v])
    return vals


def _peer_route_kernel(q_ref, keys_ref, s0_ref, s1_ref, e0_ref, e1_ref, th_ref):
    k = PEER_TOPK
    s0 = _dot_nt(keys_ref[0], q_ref[:, 0:PEER_KEYS])
    s1 = _dot_nt(keys_ref[1], q_ref[:, PEER_KEYS:2 * PEER_KEYS])
    top0, top1 = _top_values_sorted(s0, k + 1), _top_values_sorted(s1, k + 1)
    pad = [jnp.full_like(top0[0], -jnp.inf)] * 7
    t0, t1 = jnp.concatenate(top0 + pad, axis=0), jnp.concatenate(top1 + pad, axis=0)
    cand = jnp.concatenate([top0[0] + t1] + [top0[i] + t1[0:8] for i in range(1, 8)] + [t0[8:24] + top1[0]], axis=0)
    best = _top_values(cand, k + 1)
    z = None
    for v in best[:k]:
        e = jnp.exp(v - best[0])
        z = e if z is None else z + e
    s0_ref[0] = s0.reshape(s0_ref.shape[1:])
    s1_ref[0] = s1
    e0_ref[0] = (jnp.exp(s0 - top0[0]) / z).reshape(e0_ref.shape[1:])
    e1_ref[0] = jnp.exp(s1 - top1[0])
    th_ref[0] = 0.5 * (best[k - 1] + best[k])


def _peer_route_call(q, keys, tt=512):
    n = q.shape[0]
    n_heads = keys.shape[0] // 2
    big = jax.ShapeDtypeStruct((n_heads, PEER_KEYS, n), f32)
    bspec = pl.BlockSpec((1, PEER_KEYS, tt), lambda t, h: (h, 0, t))
    grp = jax.ShapeDtypeStruct((n_heads, PEER_KEYS // 8, 8, n), f32)
    gspec = pl.BlockSpec((1, PEER_KEYS // 8, 8, tt), lambda t, h: (h, 0, 0, t))
    return pl.pallas_call(
        _peer_route_kernel,
        grid=(n // tt, n_heads),
        in_specs=[pl.BlockSpec((tt, 2 * PEER_KEYS), lambda t, h: (t, h)),
                  pl.BlockSpec((2, PEER_KEYS, PEER_KEYS), lambda t, h: (h, 0, 0))],
        out_specs=[gspec, bspec, gspec, bspec, pl.BlockSpec((1, 1, tt), lambda t, h: (h, 0, t))],
        out_shape=[grp, big, grp, big, jax.ShapeDtypeStruct((n_heads, 1, n), f32)],
        compiler_params=_params("parallel", "arbitrary"),
        name="peer_route",
    )(q, keys)


PEER_I_PER_TILE = 4


def _peer_dense_kernel(xm_ref, u_ref, v_ref, s0_ref, e0_ref, s1_ref, e1_ref, th_ref, o_ref, s_s, w_s):
    e = pl.program_id(1)
    n_steps = pl.num_programs(1)

    @pl.when(e == 0)
    def _():
        o_ref[...] = jnp.zeros(o_ref.shape, f32)

    @pl.when(e > 0)
    def _():
        n_heads = s1_ref.shape[0]
        sub = ((e - 1) % (8 // PEER_I_PER_TILE)) * PEER_I_PER_TILE
        for il in range(PEER_I_PER_TILE):
            w = None
            for h in range(n_heads):
                s0 = s0_ref[h, 0, pl.ds(sub + il, 1), :]
                e0 = e0_ref[h, 0, pl.ds(sub + il, 1), :]
                keep = s1_ref[h] >= th_ref[h] - s0
                term = jnp.where(keep, e0 * e1_ref[h], 0.0)
                w = term if w is None else w + term
            w_s[il * PEER_KEYS:(il + 1) * PEER_KEYS, :] = w
        act = (_gelu(s_s[...]) * w_s[...]).T.astype(bf16)
        o_ref[...] += _dot(act, v_ref[...])

    @pl.when(e < n_steps - 1)
    def _():
        s_s[...] = _dot_nt(u_ref[...], xm_ref[...])


def _peer_dense_call(xm, u_tab, v_tab, layer, s0, e0, s1, e1, th, tm=512):
    n, d = xm.shape
    n_exp = u_tab.shape[1]
    te = PEER_I_PER_TILE * PEER_KEYS
    n_tiles = n_exp // te
    n_heads = s1.shape[0]
    per8 = 8 // PEER_I_PER_TILE
    prev = lambda e: jnp.maximum(e - 1, 0)
    row8 = pl.BlockSpec((n_heads, 1, 8, tm), lambda t, e: (0, prev(e) // per8, 0, t))
    full = pl.BlockSpec((n_heads, PEER_KEYS, tm), lambda t, e: (0, 0, t))
    return pl.pallas_call(
        _peer_dense_kernel,
        grid=(n // tm, n_tiles + 1),
        in_specs=[pl.BlockSpec((tm, d), lambda t, e: (t, 0)),
                  pl.BlockSpec((None, te, d), lambda t, e: (layer, jnp.minimum(e, n_tiles - 1), 0)),
                  pl.BlockSpec((None, te, d), lambda t, e: (layer, prev(e), 0)),
                  row8, row8, full, full,
                  pl.BlockSpec((n_heads, 1, tm), lambda t, e: (0, 0, t))],
        out_specs=pl.BlockSpec((tm, d), lambda t, e: (t, 0)),
        out_shape=jax.ShapeDtypeStruct((n, d), f32),
        scratch_shapes=[pltpu.VMEM((te, tm), f32), pltpu.VMEM((te, tm), f32)],
        compiler_params=_params("parallel", "arbitrary", vmem=60 * 1024 * 1024),
        name="peer_dense",
    )(xm, u_tab, v_tab, s0, e0, s1, e1, th)


def kernel(x, c, ctx, c_ctx, w_ada, b_ada, w_in, a_q_norm, a_k_norm, b_sink, lru_conv_w, lru_conv_b, lru_w_r, lru_b_r, lru_w_i, lru_b_i, lru_lambda, s5_a_re, s5_a_im, s5_log_step, s5_b_re, s5_b_im, s5_c_re, s5_c_im, s5_d, s5_w_glu, s5_b_glu, w_gate, b_gate, w_branch, w_out, ln_g, ln_b, peer_w_q, peer_sub_keys, peer_u, peer_v):
    n_batch, n_lat, d = x.shape
    n_ctx = ctx.shape[1]
    depth = w_ada.shape[0]
    seq_tot = n_ctx + n_lat
    n_tok = n_batch * seq_tot
    mix_w = d // 4
    assert n_ctx % ROW_TILE == 0 and n_lat % ROW_TILE == 0 and n_tok % MM_TILE_M == 0
    assert n_ctx == ROW_TILE, "scan kernels treat row tile 0 of every batch element as the context"
    tiles_per_batch = seq_tot // ROW_TILE
    geom = (tiles_per_batch, n_ctx // ROW_TILE, n_batch)
    alpha = (2.0 * depth) ** 0.25
    s5_steps = seq_tot // 4
    assert seq_tot % 4 == 0 and s5_steps % (8 * S5_CHUNK) == 0 and n_ctx % S5_CHUNK == 0

    cin = jnp.zeros((8, d), f32).at[:n_batch].set(c).at[n_batch].set(c_ctx)
    mod = _ada_call(cin, w_ada, b_ada)

    def mod_vec(l, k):
        return mod[l, :, k * d:(k + 1) * d].reshape(8, 1, d)

    cos, sin = _rope_tables(n_lat, n_ctx)
    w_in_b, w_gate_b, w_branch_b, w_out_b = (w.astype(bf16) for w in (w_in, w_gate, w_branch, w_out))
    w_glu_b, w_q_b, u_tab_b, v_tab_b = (w.astype(bf16) for w in (s5_w_glu, peer_w_q, peer_u, peer_v))
    stream = (x, ctx)
    um = _mod_call(stream, n_tok, mod_vec(0, 1), mod_vec(0, 0), geom)

    qkv_cols = N_QKV_SLICES * HEAD_DIM
    for l in range(depth):
        z = _mm_call(um, w_in_b, l, f32, name="in_proj")
        qkv = _prep_call(z, cos, sin, a_q_norm[l], a_k_norm[l], n_batch, seq_tot)
        ya = _attn_a_call(qkv, n_ctx).reshape(n_tok, mix_w)
        yb = _attn_b_call(qkv, b_sink[l], n_ctx).reshape(n_tok, mix_w)

        lru_x_block = qkv_cols // mix_w
        a_coef, b_coef = _lru_gate_call(
            z, lru_x_block, lru_conv_w[l], lru_conv_b[l], lru_w_r[l].astype(bf16), lru_b_r[l],
            lru_w_i[l].astype(bf16), lru_b_i[l], jax.nn.softplus(-lru_lambda[l]), tiles_per_batch)
        hf, hr = _lru_scan_call(a_coef, b_coef, n_batch, tiles_per_batch)
        yr = _lru_out_call(hf, hr, z, lru_x_block + 1)

        s5_block0 = (qkv_cols + 2 * mix_w) // 128
        k_cmp, w_cmp, m_cmp, lam_r = _s5_weights(s5_a_re[l], s5_a_im[l], s5_log_step[l], s5_b_re[l],
                                                  s5_b_im[l], s5_c_re[l], s5_c_im[l])
        e_parts = _s5_inject_call(z, s5_block0, w_cmp, s5_steps)
        h_parts = _s5_scan_call(e_parts, lam_r, n_batch, n_ctx // S5_CHUNK)
        y_s5 = _s5_read_call(z, s5_block0, h_parts, k_cmp, m_cmp, s5_d[l], s5_steps)
        ys = _glu_call(y_s5, w_glu_b, l, s5_b_glu[l])

        merged = _merge_call(um, (ya, yb, yr, ys), w_gate_b, b_gate[l], w_branch_b, l)
        y = _mm_call(merged, w_out_b, l, f32, name="out_proj")
        xs, um = _ln_call(stream, y, mod_vec(l, 2), ln_g[l, 0], ln_b[l, 0], (mod_vec(l, 4), mod_vec(l, 3)), geom, alpha)

        q = _mm_call(um, w_q_b, l, bf16, name="peer_query")
        n_heads = peer_sub_keys.shape[1]
        keys = peer_sub_keys[l].astype(bf16).reshape(2 * n_heads, PEER_KEYS, -1)
        s0, s1, e0, e1, th = _peer_route_call(q, keys)
        f = _peer_dense_call(um, u_tab_b, v_tab_b, l, s0, e0, s1, e1, th)
        if l == depth - 1:
            return _ln_call((xs,), f, mod_vec(l, 5), ln_g[l, 1], ln_b[l, 1], None, geom, alpha)[0]
        xs, um = _ln_call((xs,), f, mod_vec(l, 5), ln_g[l, 1], ln_b[l, 1],
                          (mod_vec(l + 1, 1), mod_vec(l + 1, 0)), geom, alpha)
        stream = (xs,)
```

```python
import functools
import math

import jax
import jax.numpy as jnp
from jax import lax
from jax.experimental import pallas as pl
from jax.experimental.pallas import tpu as pltpu

f32 = jnp.float32
bf16 = jnp.bfloat16

HEAD_DIM = 128
GRID_W = 64
ROPE_THETA = 10000.0
WINDOW = 128
Q_TILE = 128
KV_GROUP = 4
LRU_C = 8.0
LRU_BLOCKS = 8
S5_GROUP = 16
S5_STATE = 64
S5_CHUNK = 8
S5_OCT = 8
PEER_TOPK = 16
PEER_KEYS = 128
LN_EPS = 1e-6
NEG = -1e30
ROW_TILE = 256
MM_TILE_M = 512
PROJ_TILE_M = 1536
V7X_VMEM_LIMIT = 56 * 1024 * 1024


def _params(*sem, vmem=V7X_VMEM_LIMIT):
    return pltpu.CompilerParams(dimension_semantics=sem, vmem_limit_bytes=vmem)


def _dot(a, b):
    return jnp.dot(a, b, preferred_element_type=f32)


def _dot_nt(a, b):
    return lax.dot_general(a, b, (((1,), (1,)), ((), ())), preferred_element_type=f32)


def _gelu(x):
    return x * (0.5 * (1.0 + jnp.tanh(math.sqrt(2.0 / math.pi) * (x + 0.044715 * (x * x * x)))))


def _sigmoid(x):
    return 0.5 * (1.0 + jnp.tanh(0.5 * x))


def _ada_kernel(c_ref, w_ref, b_ref, o_ref):
    c = c_ref[...]
    a = (c * _sigmoid(c)).astype(bf16)
    o_ref[0] = _dot(a, w_ref[0].astype(bf16)) + b_ref[0]


def _ada_call(cin, w_ada, b_ada, tn=512):
    depth, d, n = w_ada.shape
    return pl.pallas_call(
        _ada_kernel,
        grid=(depth, n // tn),
        in_specs=[pl.BlockSpec((8, d), lambda l, j: (0, 0)),
                  pl.BlockSpec((1, d, tn), lambda l, j: (l, 0, j)),
                  pl.BlockSpec((1, 1, tn), lambda l, j: (l, 0, j))],
        out_specs=pl.BlockSpec((1, 8, tn), lambda l, j: (l, 0, j)),
        out_shape=jax.ShapeDtypeStruct((depth, 8, n), f32),
        compiler_params=_params("parallel", "parallel"),
        name="adaln",
    )(cin, w_ada, b_ada.reshape(depth, 1, n))


def _row_group(i, tiles_per_batch, ctx_tiles, n_batch):
    return jnp.where(i % tiles_per_batch < ctx_tiles, n_batch, i // tiles_per_batch)


def _stream_specs(stream, geom, d):
    tpb = geom[0]
    if len(stream) == 1:
        return [pl.BlockSpec((ROW_TILE, d), lambda i: (i, 0))]
    return [pl.BlockSpec((1, ROW_TILE, d), lambda i: (i // tpb, jnp.maximum(i % tpb - 1, 0), 0)),
            pl.BlockSpec((1, ROW_TILE, d), lambda i: (i // tpb, 0, 0))]


def _read_stream(refs, is_ctx):
    if len(refs) == 1:
        return refs[0][...]
    return jnp.where(is_ctx, refs[1][0], refs[0][0])


def _mod_kernel(*refs, n_stream, tiles_per_batch):
    sc_ref, sh_ref, o_ref = refs[n_stream:]
    x = _read_stream(refs[:n_stream], pl.program_id(0) % tiles_per_batch == 0)
    o_ref[...] = (x * (1.0 + sc_ref[0]) + sh_ref[0]).astype(o_ref.dtype)


def _mod_call(stream, n, sc, sh, geom):
    d = sc.shape[-1]
    tpb, ct, nb = geom
    sel = lambda i: (_row_group(i, tpb, ct, nb), 0, 0)
    return pl.pallas_call(
        functools.partial(_mod_kernel, n_stream=len(stream), tiles_per_batch=tpb),
        grid=(n // ROW_TILE,),
        in_specs=_stream_specs(stream, geom, d) + [pl.BlockSpec((1, 1, d), sel), pl.BlockSpec((1, 1, d), sel)],
        out_specs=pl.BlockSpec((ROW_TILE, d), lambda i: (i, 0)),
        out_shape=jax.ShapeDtypeStruct((n, d), bf16),
        compiler_params=_params("parallel"),
        name="modulate",
    )(*stream, sc, sh)


def _ln_kernel(*refs, n_stream, tiles_per_batch, alpha, last):
    y_ref, g_ref, lg_ref, lb_ref = refs[n_stream:n_stream + 4]
    is_ctx = pl.program_id(0) % tiles_per_batch == 0
    v = alpha * _read_stream(refs[:n_stream], is_ctx) + g_ref[0] * y_ref[...]
    mu = jnp.mean(v, axis=-1, keepdims=True)
    vc = v - mu
    var = jnp.mean(vc * vc, axis=-1, keepdims=True)
    o = vc * lax.rsqrt(var + LN_EPS) * lg_ref[...] + lb_ref[...]
    if last:
        xo_ref, = refs[n_stream + 4:]

        @pl.when(jnp.logical_not(is_ctx))
        def _():
            xo_ref[0] = o
    else:
        sc_ref, sh_ref, xo_ref, xm_ref = refs[n_stream + 4:]
        xo_ref[...] = o
        xm_ref[...] = (o * (1.0 + sc_ref[0]) + sh_ref[0]).astype(xm_ref.dtype)


def _ln_call(stream, y, gate, ln_g, ln_b, nxt_mod, geom, alpha):
    n, d = y.shape
    tpb, ct, nb = geom
    sel = lambda i: (_row_group(i, tpb, ct, nb), 0, 0)
    row = pl.BlockSpec((ROW_TILE, d), lambda i: (i, 0))
    vec = pl.BlockSpec((1, d), lambda i: (0, 0))
    mod = pl.BlockSpec((1, 1, d), sel)
    last = nxt_mod is None
    in_specs = _stream_specs(stream, geom, d) + [row, mod, vec, vec]
    args = list(stream) + [y, gate, ln_g.reshape(1, d), ln_b.reshape(1, d)]
    if last:
        out_specs = [pl.BlockSpec((1, ROW_TILE, d), lambda i: (i // tpb, jnp.maximum(i % tpb - 1, 0), 0))]
        out_shape = [jax.ShapeDtypeStruct((nb, n // nb - ct * ROW_TILE, d), f32)]
    else:
        in_specs += [mod, mod]
        args += list(nxt_mod)
        out_specs = [row, row]
        out_shape = [jax.ShapeDtypeStruct((n, d), f32), jax.ShapeDtypeStruct((n, d), bf16)]
    return pl.pallas_call(
        functools.partial(_ln_kernel, n_stream=len(stream), tiles_per_batch=tpb, alpha=alpha, last=last),
        grid=(n // ROW_TILE,),
        in_specs=in_specs, out_specs=out_specs, out_shape=out_shape,
        compiler_params=_params("arbitrary" if last else "parallel"),
        name="residual_ln",
    )(*args)


def _mm_kernel(a_ref, b_ref, o_ref):
    o_ref[...] = _dot(a_ref[...], b_ref[...]).astype(o_ref.dtype)


def _mm_call(a, b, layer, out_dtype, tn=512, name="matmul"):
    m, k = a.shape
    n = b.shape[2]
    tm = PROJ_TILE_M if m % PROJ_TILE_M == 0 else MM_TILE_M
    return pl.pallas_call(
        _mm_kernel,
        grid=(m // tm, n // tn),
        in_specs=[pl.BlockSpec((tm, k), lambda i, j: (i, 0)),
                  pl.BlockSpec((None, k, tn), lambda i, j: (layer, 0, j))],
        out_specs=pl.BlockSpec((tm, tn), lambda i, j: (i, j)),
        out_shape=jax.ShapeDtypeStruct((m, n), out_dtype),
        compiler_params=_params("parallel", "arbitrary"),
        name=name,
    )(a, b)


def _glu_kernel(a_ref, wv_ref, wg_ref, bv_ref, bg_ref, o_ref):
    a = a_ref[...].astype(bf16)
    val = _dot(a, wv_ref[...]) + bv_ref[...]
    gate = _dot(a, wg_ref[...]) + bg_ref[...]
    o_ref[...] = (val * _sigmoid(gate)).astype(o_ref.dtype)


def _glu_call(a, w, layer, b, tn=512, tm=MM_TILE_M):
    m, k = a.shape
    n = w.shape[2] // 2
    nj = n // tn
    b2 = b.reshape(1, 2 * n)
    return pl.pallas_call(
        _glu_kernel,
        grid=(m // tm, nj),
        in_specs=[pl.BlockSpec((tm, k), lambda i, j: (i, 0)),
                  pl.BlockSpec((None, k, tn), lambda i, j: (layer, 0, j)),
                  pl.BlockSpec((None, k, tn), lambda i, j: (layer, 0, j + nj)),
                  pl.BlockSpec((1, tn), lambda i, j: (0, j)),
                  pl.BlockSpec((1, tn), lambda i, j: (0, j + nj))],
        out_specs=pl.BlockSpec((tm, tn), lambda i, j: (i, j)),
        out_shape=jax.ShapeDtypeStruct((m, n), bf16),
        compiler_params=_params("parallel", "arbitrary"),
        name="s5_glu",
    )(a, w, w, b2, b2)


def _merge_kernel(u_ref, ya_ref, yb_ref, yr_ref, ys_ref, wg_ref, bg_ref, wb_ref, o_ref):
    u = u_ref[...]
    acc = None
    for i, y_ref in enumerate((ya_ref, yb_ref, yr_ref, ys_ref)):
        gate = _sigmoid(_dot(u, wg_ref[i]) + bg_ref[i])
        term = gate * _dot(y_ref[...], wb_ref[i])
        acc = term if acc is None else acc + term
    o_ref[...] = acc.astype(o_ref.dtype)


def _merge_call(u, branches, w_gate, b_gate, w_branch, layer, tn=256, tm=MM_TILE_M):
    m, d = u.shape
    _, nbr, bw, n = w_branch.shape
    ybr = pl.BlockSpec((tm, bw), lambda i, j: (i, 0))
    return pl.pallas_call(
        _merge_kernel,
        grid=(m // tm, n // tn),
        in_specs=[pl.BlockSpec((tm, d), lambda i, j: (i, 0)), ybr, ybr, ybr, ybr,
                  pl.BlockSpec((None, nbr, d, tn), lambda i, j: (layer, 0, 0, j)),
                  pl.BlockSpec((nbr, 1, tn), lambda i, j: (0, 0, j)),
                  pl.BlockSpec((None, nbr, bw, tn), lambda i, j: (layer, 0, 0, j))],
        out_specs=pl.BlockSpec((tm, tn), lambda i, j: (i, j)),
        out_shape=jax.ShapeDtypeStruct((m, n), bf16),
        compiler_params=_params("parallel", "arbitrary"),
        name="branch_merge",
    )(u, *branches, w_gate, b_gate.reshape(nbr, 1, n), w_branch)


def _rope_tables(n_lat, n_ctx):
    t = jnp.arange(n_lat)
    row = (t // GRID_W).astype(f32)
    col = (t % GRID_W).astype(f32)
    n_freq = HEAD_DIM // 4
    inv = ROPE_THETA ** (-jnp.arange(n_freq, dtype=f32) / n_freq)
    ang = jnp.concatenate([row[:, None] * inv, col[:, None] * inv], axis=-1)
    cos = jnp.repeat(jnp.cos(ang), 2, axis=-1)
    sin = jnp.repeat(jnp.sin(ang), 2, axis=-1) * jnp.tile(jnp.array([-1.0, 1.0], f32), HEAD_DIM // 2)
    cos = jnp.concatenate([jnp.ones((n_ctx, HEAD_DIM), f32), cos], axis=0)
    sin = jnp.concatenate([jnp.zeros((n_ctx, HEAD_DIM), f32), sin], axis=0)
    return cos, sin


N_QKV_SLICES = 24
A_Q, A_K, A_V, B_Q, B_K, B_V = 0, 8, 10, 12, 20, 22


def _prep_kernel(z_ref, cos_ref, sin_ref, gq_ref, gk_ref, o_ref):
    cos, sin = cos_ref[...], sin_ref[...]
    even = lax.broadcasted_iota(jnp.int32, cos.shape, 1) % 2 == 0
    for s in range(N_QKV_SLICES):
        y = z_ref[:, s * HEAD_DIM:(s + 1) * HEAD_DIM]
        if s < A_V:
            gain = gq_ref[...] if s < A_K else gk_ref[...]
            y = y * lax.rsqrt(jnp.mean(y * y, axis=-1, keepdims=True) + LN_EPS) * gain
        if s < A_V or B_Q <= s < B_V:
            partner = jnp.where(even, pltpu.roll(y, HEAD_DIM - 1, axis=1), pltpu.roll(y, 1, axis=1))
            y = y * cos + partner * sin
        if s < A_K or B_Q <= s < B_K:
            y = y * (HEAD_DIM ** -0.5)
        o_ref[0, s] = y.astype(o_ref.dtype)


def _prep_call(z, cos, sin, gain_q, gain_k, n_batch, seq_tot):
    n = z.shape[0]
    tr = ROW_TILE
    tpb = seq_tot // tr
    width = N_QKV_SLICES * HEAD_DIM
    vec = pl.BlockSpec((1, HEAD_DIM), lambda i: (0, 0))
    return pl.pallas_call(
        _prep_kernel,
        grid=(n // tr,),
        in_specs=[pl.BlockSpec((tr, width), lambda i: (i, 0)),
                  pl.BlockSpec((tr, HEAD_DIM), lambda i: (i % tpb, 0)),
                  pl.BlockSpec((tr, HEAD_DIM), lambda i: (i % tpb, 0)), vec, vec],
        out_specs=pl.BlockSpec((1, N_QKV_SLICES, tr, HEAD_DIM), lambda i: (i // tpb, 0, i % tpb, 0)),
        out_shape=jax.ShapeDtypeStruct((n_batch, N_QKV_SLICES, seq_tot, HEAD_DIM), bf16),
        compiler_params=_params("parallel"),
        name="qkv_prep",
    )(z, cos, sin, gain_q.reshape(1, HEAD_DIM), gain_k.reshape(1, HEAD_DIM))


def _attn_a_kernel(q_ref, k_ref, v_ref, o_ref, m_s, acc_s, *, n_ctx, tk, n_lat_chunks):
    q = q_ref[0].reshape(KV_GROUP * Q_TILE, HEAD_DIM)

    def values(lo, n):
        return jnp.concatenate([v_ref[0, 0, lo:lo + n, :], jnp.ones((n, HEAD_DIM), bf16)], axis=1)


    s = _dot_nt(q, k_ref[0, 0, 0:n_ctx, :]).astype(bf16)
    m0 = jnp.max(s, axis=-1, keepdims=True)
    m_s[...] = m0.astype(f32)
    acc_s[...] = _dot(jnp.exp(s - m0), values(0, n_ctx))

    @pl.when(pl.program_id(2) >= n_ctx // Q_TILE)
    def _():
        for c in range(n_lat_chunks):
            lo = n_ctx + c * tk
            s = _dot_nt(q, k_ref[0, 0, lo:lo + tk, :]).astype(bf16)
            m_prev = m_s[...]
            m_new = jnp.maximum(m_prev, jnp.max(s, axis=-1, keepdims=True).astype(f32))
            p = jnp.exp(s - m_new.astype(bf16))
            acc_s[...] = jnp.exp(m_prev - m_new) * acc_s[...] + _dot(p, values(lo, tk))
            m_s[...] = m_new

    out = acc_s[:, 0:HEAD_DIM] / acc_s[:, HEAD_DIM:2 * HEAD_DIM]
    for g in range(KV_GROUP):
        o_ref[0, :, g * HEAD_DIM:(g + 1) * HEAD_DIM] = out[g * Q_TILE:(g + 1) * Q_TILE].astype(o_ref.dtype)


def _attn_a_call(qkv, n_ctx, tk=1024):
    nb, _, seq_tot, _ = qkv.shape
    n_kv = (A_V - A_K)
    rows = KV_GROUP * Q_TILE
    n_lat = seq_tot - n_ctx
    tk = min(tk, n_lat)
    assert n_lat % tk == 0
    kern = functools.partial(_attn_a_kernel, n_ctx=n_ctx, tk=tk, n_lat_chunks=n_lat // tk)
    return pl.pallas_call(
        kern,
        grid=(nb, n_kv, seq_tot // Q_TILE),
        in_specs=[pl.BlockSpec((1, KV_GROUP, Q_TILE, HEAD_DIM), lambda b, h, i: (b, A_Q // KV_GROUP + h, i, 0)),
                  pl.BlockSpec((1, 1, seq_tot, HEAD_DIM), lambda b, h, i: (b, A_K + h, 0, 0)),
                  pl.BlockSpec((1, 1, seq_tot, HEAD_DIM), lambda b, h, i: (b, A_V + h, 0, 0))],
        out_specs=pl.BlockSpec((1, Q_TILE, KV_GROUP * HEAD_DIM), lambda b, h, i: (b, i, h)),
        out_shape=jax.ShapeDtypeStruct((nb, seq_tot, n_kv * KV_GROUP * HEAD_DIM), bf16),
        scratch_shapes=[pltpu.VMEM((rows, 1), f32), pltpu.VMEM((rows, 2 * HEAD_DIM), f32)],
        compiler_params=_params("parallel", "parallel", "arbitrary"),
        name="dense_attention",
    )(qkv, qkv, qkv)


def _attn_b_kernel(q_ref, k_ref, v_ref, sink_ref, o_ref, *, n_ctx, n_blk):
    i = pl.program_id(2)
    blk = i - n_ctx // Q_TILE
    rows = KV_GROUP * Q_TILE
    q = q_ref[0].reshape(rows, HEAD_DIM)
    sink = sink_ref[0]
    kc = k_ref[0, 0, 0:n_ctx, :]
    vc = v_ref[0, 0, 0:n_ctx, :]
    s_ctx = _dot_nt(q, kc)
    m = jnp.maximum(jnp.max(s_ctx, axis=-1, keepdims=True), sink)
    span = 3 * Q_TILE
    first = jnp.clip(blk - 1, 0, n_blk - 3) * Q_TILE
    start = pl.multiple_of(n_ctx + first, Q_TILE)
    q_pos = blk * Q_TILE + lax.broadcasted_iota(jnp.int32, (rows, span), 0) % Q_TILE
    k_pos = first + lax.broadcasted_iota(jnp.int32, (rows, span), 1)
    keep = jnp.logical_and(jnp.abs(k_pos - q_pos) <= WINDOW, blk >= 0)
    s_loc = jnp.where(keep, _dot_nt(q, k_ref[0, 0, pl.ds(start, span), :]), NEG)
    m = jnp.maximum(m, jnp.max(s_loc, axis=-1, keepdims=True))

    def values(v):
        return jnp.concatenate([v, jnp.ones(v.shape, bf16)], axis=1)

    acc = _dot(jnp.exp((s_ctx - m).astype(bf16)), values(vc))
    acc = acc + _dot(jnp.exp((s_loc - m).astype(bf16)), values(v_ref[0, 0, pl.ds(start, span), :]))
    out = acc[:, 0:HEAD_DIM] / (jnp.exp(sink - m) + acc[:, HEAD_DIM:2 * HEAD_DIM])
    for g in range(KV_GROUP):
        o_ref[0, :, g * HEAD_DIM:(g + 1) * HEAD_DIM] = out[g * Q_TILE:(g + 1) * Q_TILE].astype(o_ref.dtype)


def _attn_b_call(qkv, sink, n_ctx):
    nb, _, seq_tot, _ = qkv.shape
    n_kv = (B_V - B_K)
    rows = KV_GROUP * Q_TILE
    sink_rows = jnp.repeat(sink.astype(f32).reshape(n_kv, KV_GROUP), Q_TILE, axis=1).reshape(n_kv, rows, 1)
    n_blk = (seq_tot - n_ctx) // Q_TILE
    assert n_blk >= 3, "the windowed mixer reads a span of three key blocks"
    kern = functools.partial(_attn_b_kernel, n_ctx=n_ctx, n_blk=n_blk)
    return pl.pallas_call(
        kern,
        grid=(nb, n_kv, seq_tot // Q_TILE),
        in_specs=[pl.BlockSpec((1, KV_GROUP, Q_TILE, HEAD_DIM), lambda b, h, i: (b, B_Q // KV_GROUP + h, i, 0)),
                  pl.BlockSpec((1, 1, seq_tot, HEAD_DIM), lambda b, h, i: (b, B_K + h, 0, 0)),
                  pl.BlockSpec((1, 1, seq_tot, HEAD_DIM), lambda b, h, i: (b, B_V + h, 0, 0)),
                  pl.BlockSpec((1, rows, 1), lambda b, h, i: (h, 0, 0))],
        out_specs=pl.BlockSpec((1, Q_TILE, KV_GROUP * HEAD_DIM), lambda b, h, i: (b, i, h)),
        out_shape=jax.ShapeDtypeStruct((nb, seq_tot, n_kv * KV_GROUP * HEAD_DIM), bf16),
        compiler_params=_params("parallel", "parallel", "arbitrary"),
        name="window_attention",
    )(qkv, qkv, qkv, sink_rows)


HALO = 8


def _lru_gate_kernel(x_ref, xp_ref, xn_ref, cw_ref, cb_ref, wr_ref, br_ref, wi_ref, bi_ref, sp_ref,
                     a_ref, b_ref, pad_s, *, tiles_per_batch):
    tr = x_ref.shape[0]
    t = pl.program_id(0) % tiles_per_batch
    has_prev = t >= 2
    has_next = jnp.logical_and(t >= 1, t < tiles_per_batch - 1)
    x = x_ref[...]
    pad_s[0:HALO, :] = jnp.where(has_prev, xp_ref[...], 0.0)
    pad_s[HALO:HALO + tr, :] = x
    pad_s[HALO + tr:2 * HALO + tr, :] = jnp.where(has_next, xn_ref[...], 0.0)
    cw = cw_ref[...]
    xl = cb_ref[...] + pad_s[HALO - 2:HALO - 2 + tr, :] * cw[0:1]
    xl = xl + pad_s[HALO - 1:HALO - 1 + tr, :] * cw[1:2]
    xl = xl + x * cw[2:3]
    xl = xl + pad_s[HALO + 1:HALO + 1 + tr, :] * cw[3:4]
    bw = xl.shape[1] // LRU_BLOCKS
    for n in range(LRU_BLOCKS):
        cols = slice(n * bw, (n + 1) * bw)
        xb = xl[:, cols]
        xb16 = xb.astype(bf16)
        for d in range(2):
            r = _sigmoid(_dot(xb16, wr_ref[d, n]) + br_ref[d:d + 1, cols])
            gi = _sigmoid(_dot(xb16, wi_ref[d, n]) + bi_ref[d:d + 1, cols])
            log_a = (-LRU_C) * r * sp_ref[d:d + 1, cols]
            a = jnp.exp(log_a)
            a_ref[d, :, cols] = a
            b_ref[d, :, cols] = jnp.sqrt(1.0 - a * a) * gi * xb


def _lru_gate_call(z, col_block, conv_w, conv_b, w_r, b_r, w_i, b_i, softplus_lam, tiles_per_batch):
    n = z.shape[0]
    w = conv_w.shape[1]
    tr = ROW_TILE
    hb = tr // HALO
    n_halo = n // HALO
    full = lambda shape: pl.BlockSpec(shape, lambda i: (0,) * len(shape))
    kern = functools.partial(_lru_gate_kernel, tiles_per_batch=tiles_per_batch)
    out = jax.ShapeDtypeStruct((2, n, w), f32)
    return pl.pallas_call(
        kern,
        grid=(n // tr,),
        in_specs=[pl.BlockSpec((tr, w), lambda i: (i, col_block)),
                  pl.BlockSpec((HALO, w), lambda i: (jnp.maximum(i * hb - 1, 0), col_block)),
                  pl.BlockSpec((HALO, w), lambda i: (jnp.minimum((i + 1) * hb, n_halo - 1), col_block)),
                  full(conv_w.shape), full((1, w)), full(w_r.shape), full(b_r.shape),
                  full(w_i.shape), full(b_i.shape), full(softplus_lam.shape)],
        out_specs=[pl.BlockSpec((2, tr, w), lambda i: (0, i, 0)), pl.BlockSpec((2, tr, w), lambda i: (0, i, 0))],
        out_shape=[out, out],
        scratch_shapes=[pltpu.VMEM((tr + 2 * HALO, w), f32)],
        compiler_params=_params("parallel"),
        name="lru_gates",
    )(z, z, z, conv_w, conv_b.reshape(1, w), w_r, b_r, w_i, b_i, softplus_lam)


def _lru_scan_kernel(af_ref, bf_ref, ar_ref, br_ref, hf_ref, hr_ref, sf, sr):
    tr = hf_ref.shape[0]

    @pl.when(pl.program_id(1) == 0)
    def _():
        sf[...] = jnp.zeros(sf.shape, f32)
        sr[...] = jnp.zeros(sr.shape, f32)

    def body(t, carry):
        hf, hr = carry
        hf = af_ref[0, pl.ds(t, 1), :] * hf + bf_ref[0, pl.ds(t, 1), :]
        hf_ref[pl.ds(t, 1), :] = hf
        u = tr - 1 - t
        hr = ar_ref[0, pl.ds(u, 1), :] * hr + br_ref[0, pl.ds(u, 1), :]
        hr_ref[pl.ds(u, 1), :] = hr
        return hf, hr

    hf, hr = lax.fori_loop(0, tr, body, (sf[...], sr[...]), unroll=8)
    sf[...] = hf
    sr[...] = hr


def _scan_tile_maps(tiles_per_batch):
    fwd = lambda b, i: b * tiles_per_batch + i
    rev = lambda b, i: b * tiles_per_batch + jnp.where(i == 0, 0, tiles_per_batch - i)
    return fwd, rev


def _lru_scan_call(a, b, n_batch, tiles_per_batch):
    _, n, w = a.shape
    tr = ROW_TILE
    fwd, rev = _scan_tile_maps(tiles_per_batch)
    out = jax.ShapeDtypeStruct((n, w), f32)
    return pl.pallas_call(
        _lru_scan_kernel,
        grid=(n_batch, tiles_per_batch),
        in_specs=[pl.BlockSpec((1, tr, w), lambda bb, i: (0, fwd(bb, i), 0)),
                  pl.BlockSpec((1, tr, w), lambda bb, i: (0, fwd(bb, i), 0)),
                  pl.BlockSpec((1, tr, w), lambda bb, i: (1, rev(bb, i), 0)),
                  pl.BlockSpec((1, tr, w), lambda bb, i: (1, rev(bb, i), 0))],
        out_specs=[pl.BlockSpec((tr, w), lambda bb, i: (fwd(bb, i), 0)),
                   pl.BlockSpec((tr, w), lambda bb, i: (rev(bb, i), 0))],
        out_shape=[out, out],
        scratch_shapes=[pltpu.VMEM((1, w), f32), pltpu.VMEM((1, w), f32)],
        compiler_params=_params("parallel", "arbitrary"),
        name="lru_scan",
    )(a, b, a, b)


def _lru_out_kernel(hf_ref, hr_ref, g_ref, o_ref):
    o_ref[...] = ((hf_ref[...] + hr_ref[...]) * _gelu(g_ref[...])).astype(o_ref.dtype)


def _lru_out_call(hf, hr, z, gate_col_block):
    n, w = hf.shape
    tr = ROW_TILE
    row = pl.BlockSpec((tr, w), lambda i: (i, 0))
    return pl.pallas_call(
        _lru_out_kernel,
        grid=(n // tr,),
        in_specs=[row, row, pl.BlockSpec((tr, w), lambda i: (i, gate_col_block))],
        out_specs=row,
        out_shape=jax.ShapeDtypeStruct((n, w), bf16),
        compiler_params=_params("parallel"),
        name="lru_out",
    )(hf, hr, z)


def _s5_weights(a_re, a_im, log_step, b_re, b_im, c_re, c_im):
    r = S5_CHUNK
    n_g = a_re.shape[1]
    n_oct = n_g // S5_OCT
    step = jnp.exp(log_step)[..., None]
    mag = jnp.exp(a_re * step)
    lb_re, lb_im = mag * jnp.cos(a_im * step), mag * jnp.sin(a_im * step)
    den = a_re * a_re + a_im * a_im
    num_re = lb_re - 1.0
    coef_re = (num_re * a_re + lb_im * a_im) / den
    coef_im = (lb_im * a_re - num_re * a_im) / den
    bb_re = coef_re[..., None] * b_re - coef_im[..., None] * b_im
    bb_im = coef_re[..., None] * b_im + coef_im[..., None] * b_re
    tau = jnp.arange(r + 1, dtype=f32)[:, None, None, None]
    pmag = jnp.exp(tau * (a_re * step))
    pw_re, pw_im = pmag * jnp.cos(tau * (a_im * step)), pmag * jnp.sin(tau * (a_im * step))

    lb_b_re = pw_re[..., None] * bb_re - pw_im[..., None] * bb_im
    lb_b_im = pw_re[..., None] * bb_im + pw_im[..., None] * bb_re
    lag = (jnp.einsum('tdgpc,dgop->tdgco', lb_b_re, c_re) - jnp.einsum('tdgpc,dgop->tdgco', lb_b_im, c_im))
    idx = jnp.arange(r)
    sh_f = (idx[None, None, :] - idx[None, :, None] == jnp.arange(r + 1)[:, None, None]).astype(f32)
    sh_r = (idx[None, :, None] - idx[None, None, :] == jnp.arange(r + 1)[:, None, None]).astype(f32)
    k_loc = jnp.einsum('tio,tgcd->gicod', sh_f, lag[:, 0]) + jnp.einsum('tio,tgcd->gicod', sh_r, lag[:, 1])
    k_cmp = k_loc.reshape(n_oct, S5_OCT, r, S5_GROUP, r * S5_GROUP)
    k_cmp = k_cmp.transpose(0, 2, 1, 3, 4).reshape(n_oct, r * 128, r * S5_GROUP)

    def inject(d, powers):
        w_re = pw_re[powers, d][..., None] * bb_re[d] - pw_im[powers, d][..., None] * bb_im[d]
        w_im = pw_re[powers, d][..., None] * bb_im[d] + pw_im[powers, d][..., None] * bb_re[d]
        out = []
        for w in (w_re, w_im):
            w = w.reshape(r, n_oct, S5_OCT, S5_STATE, S5_GROUP).transpose(1, 0, 2, 4, 3)
            out.append(w.reshape(n_oct, r * 128, S5_STATE))
        return out
    w_cmp = jnp.concatenate(inject(0, idx[::-1]) + inject(1, idx), axis=-1)

    def readout(d, powers):
        cl_re = c_re[d][None] * pw_re[powers, d][:, :, None, :] - c_im[d][None] * pw_im[powers, d][:, :, None, :]
        cl_im = c_re[d][None] * pw_im[powers, d][:, :, None, :] + c_im[d][None] * pw_re[powers, d][:, :, None, :]
        out = []
        for w in (cl_re, -cl_im):
            w = w.reshape(r, n_oct, S5_OCT, S5_GROUP, S5_STATE).transpose(1, 2, 4, 0, 3)
            out.append(w.reshape(n_oct, S5_OCT * S5_STATE, r * S5_GROUP))
        return out
    m_cmp = jnp.concatenate(readout(0, idx + 1) + readout(1, r - idx), axis=1)

    lam_r = jnp.stack([pw_re[r, 0].reshape(-1), pw_im[r, 0].reshape(-1),
                       pw_re[r, 1].reshape(-1), pw_im[r, 1].reshape(-1)])
    return k_cmp.astype(bf16), w_cmp.astype(bf16), m_cmp.astype(bf16), lam_r


def _expand_octet(compact, inner, row_inner):
    n_rows, n_cols = compact.shape
    wide = n_cols * S5_OCT
    log_inner, log_row = inner.bit_length() - 1, row_inner.bit_length() - 1
    oct_bits = S5_OCT.bit_length() - 1
    src = lax.broadcasted_iota(jnp.int32, (n_cols, wide), 0)
    dst = lax.broadcasted_iota(jnp.int32, (n_cols, wide), 1)
    dst_compact = ((dst >> (log_inner + oct_bits)) << log_inner) + (dst & (inner - 1))
    spread = _dot(compact, (src == dst_compact).astype(compact.dtype))
    row_g = (lax.broadcasted_iota(jnp.int32, (n_rows, wide), 0) >> log_row) & (S5_OCT - 1)
    col_g = (lax.broadcasted_iota(jnp.int32, (n_rows, wide), 1) >> log_inner) & (S5_OCT - 1)
    return jnp.where(row_g == col_g, spread, 0.0).astype(compact.dtype)


def _chunk_rows(u_ref):
    n_rows = u_ref.shape[0] // S5_CHUNK
    return [u_ref[pl.ds(i, n_rows, stride=S5_CHUNK), :] for i in range(S5_CHUNK)]


def _s5_inject_kernel(u_ref, w_ref, efr_ref, efi_ref, err_ref, eri_ref, w_s):
    @pl.when(pl.program_id(1) == 0)
    def _():
        w_s[...] = _expand_octet(w_ref[0], S5_STATE, S5_GROUP)

    x = jnp.concatenate(_chunk_rows(u_ref), axis=1).astype(bf16)
    e = _dot(x, w_s[...])
    w = efr_ref.shape[1]
    for k, o_ref in enumerate((efr_ref, efi_ref, err_ref, eri_ref)):
        o_ref[...] = e[:, k * w:(k + 1) * w]


def _s5_inject_call(z, col_block0, w_cmp, steps):
    n = z.shape[0]
    n_oct, n_in, n_cmp = w_cmp.shape
    sw = S5_OCT * S5_STATE
    rows = steps // S5_CHUNK
    out = jax.ShapeDtypeStruct((n // S5_CHUNK, n_oct * sw), f32)
    ospec = pl.BlockSpec((rows, sw), lambda k, i: (i, k))
    return pl.pallas_call(
        _s5_inject_kernel,
        grid=(n_oct, n // steps),
        in_specs=[pl.BlockSpec((steps, 128), lambda k, i: (i, col_block0 + k)),
                  pl.BlockSpec((1, n_in, n_cmp), lambda k, i: (k, 0, 0))],
        out_specs=[ospec] * 4,
        out_shape=[out] * 4,
        scratch_shapes=[pltpu.VMEM((n_in, n_cmp * S5_OCT), bf16)],
        compiler_params=_params("arbitrary", "arbitrary"),
        name="s5_inject",
    )(z, w_cmp)


def _s5_scan_kernel(efr_ref, efi_ref, err_ref, eri_ref, lam_ref, hfr_ref, hfi_ref, hrr_ref, hri_ref, *, n_ctx):
    n_rows = efr_ref.shape[0]
    lfr, lfi, lrr, lri = (lam_ref[k:k + 1, :] for k in range(4))
    zero = jnp.zeros((1, efr_ref.shape[1]), f32)

    def body(t, carry):
        fr, fi, rr, ri = carry
        hfr_ref[pl.ds(t, 1), :] = fr
        hfi_ref[pl.ds(t, 1), :] = fi
        er, ei = efr_ref[pl.ds(t, 1), :], efi_ref[pl.ds(t, 1), :]
        fr, fi = lfr * fr - lfi * fi + er, lfr * fi + lfi * fr + ei
        u = jnp.where(t < n_ctx, n_ctx - 1 - t, n_rows - 1 - (t - n_ctx))
        hrr_ref[pl.ds(u, 1), :] = rr
        hri_ref[pl.ds(u, 1), :] = ri
        er, ei = err_ref[pl.ds(u, 1), :], eri_ref[pl.ds(u, 1), :]
        rr, ri = lrr * rr - lri * ri + er, lrr * ri + lri * rr + ei
        return fr, fi, rr, ri

    lax.fori_loop(0, n_rows, body, (zero, zero, zero, zero), unroll=4)


def _s5_scan_call(e_parts, lam_r, n_batch, n_ctx_rows, lane_block=512):
    n_rows_tot, width = e_parts[0].shape
    rows = n_rows_tot // n_batch
    blk = pl.BlockSpec((rows, lane_block), lambda b, j: (b, j))
    out = jax.ShapeDtypeStruct((n_rows_tot, width), f32)
    return pl.pallas_call(
        functools.partial(_s5_scan_kernel, n_ctx=n_ctx_rows),
        grid=(n_batch, width // lane_block),
        in_specs=[blk] * 4 + [pl.BlockSpec((4, lane_block), lambda b, j: (0, j))],
        out_specs=[blk] * 4,
        out_shape=[out] * 4,
        compiler_params=_params("parallel", "parallel"),
        name="s5_scan",
    )(*e_parts, lam_r)


def _s5_read_kernel(u_ref, hfr_ref, hfi_ref, hrr_ref, hri_ref, k_ref, m_ref, d_ref, o_ref, k_s, m_s):
    @pl.when(pl.program_id(1) == 0)
    def _():
        k_s[...] = _expand_octet(k_ref[0], S5_GROUP, S5_GROUP)
        m_s[...] = _expand_octet(m_ref[0], S5_GROUP, S5_STATE)

    parts = _chunk_rows(u_ref)
    x = jnp.concatenate(parts, axis=1).astype(bf16)
    h = jnp.concatenate([hfr_ref[...], hfi_ref[...], hrr_ref[...], hri_ref[...]], axis=1).astype(bf16)
    y = _dot(x, k_s[...]) + _dot(h, m_s[...])
    n_rows = u_ref.shape[0] // S5_CHUNK
    for i in range(S5_CHUNK):
        o_ref[pl.ds(i, n_rows, stride=S5_CHUNK), :] = y[:, i * 128:(i + 1) * 128] + parts[i] * d_ref[...]


def _s5_read_call(z, col_block0, h_parts, k_cmp, m_cmp, d_skip, steps):
    n = z.shape[0]
    n_oct = k_cmp.shape[0]
    sw = S5_OCT * S5_STATE
    rows = steps // S5_CHUNK
    hspec = pl.BlockSpec((rows, sw), lambda k, i: (i, k))
    return pl.pallas_call(
        _s5_read_kernel,
        grid=(n_oct, n // steps),
        in_specs=[pl.BlockSpec((steps, 128), lambda k, i: (i, col_block0 + k))] + [hspec] * 4 +
                 [pl.BlockSpec((1,) + k_cmp.shape[1:], lambda k, i: (k, 0, 0)),
                  pl.BlockSpec((1,) + m_cmp.shape[1:], lambda k, i: (k, 0, 0)),
                  pl.BlockSpec((1, 128), lambda k, i: (0, k))],
        out_specs=pl.BlockSpec((steps, 128), lambda k, i: (i, k)),
        out_shape=jax.ShapeDtypeStruct((n, n_oct * 128), f32),
        scratch_shapes=[pltpu.VMEM((k_cmp.shape[1], k_cmp.shape[2] * S5_OCT), bf16),
                        pltpu.VMEM((m_cmp.shape[1], m_cmp.shape[2] * S5_OCT), bf16)],
        compiler_params=_params("arbitrary", "arbitrary"),
        name="s5_readout",
    )(z, *h_parts, k_cmp, m_cmp, d_skip.reshape(1, -1))


def _top_values(s, k):
    vals = []
    for _ in range(k):
        m = jnp.max(s, axis=0, keepdims=True)
        vals.append(m)
        s = jnp.where(s == m, -jnp.inf, s)
    return vals


def _merge_sort_pairs(n):
    pairs = []

    def merge(lo, hi, r):
        step = r * 2
        if step < hi - lo:
            merge(lo, hi, step)
            merge(lo + r, hi, step)
            pairs.extend((i, i + r) for i in range(lo + r, hi - r, step))
        else:
            pairs.append((lo, lo + r))

    def sort(lo, hi):
        if hi - lo >= 1:
            mid = lo + (hi - lo) // 2
            sort(lo, mid)
            sort(mid + 1, hi)
            merge(lo, hi, 1)

    sort(0, n - 1)
    return pairs


def _top_values_sorted(s, k):
    n_grp = s.shape[0] // 8
    lists = [s[8 * v:8 * v + 8, :] for v in range(n_grp)]
    for i, j in _merge_sort_pairs(n_grp):
        lists[i], lists[j] = jnp.maximum(lists[i], lists[j]), jnp.minimum(lists[i], lists[j])
    lists.append(jnp.full_like(lists[0], -jnp.inf))
    vals = []
    for t in range(k):
        m = jnp.max(lists[0], axis=0, keepdims=True)
        vals.append(m)
        hit = lists[0] == m
        for v in range(min(k - t, n_grp)):
            lists[v] = jnp.where(hit, lists[v + 1], lists[v])
    return vals


def _peer_route_kernel(q_ref, keys_ref, s0_ref, s1_ref, e0_ref, e1_ref, th_ref):
    k = PEER_TOPK
    s0 = _dot_nt(keys_ref[0], q_ref[:, 0:PEER_KEYS])
    s1 = _dot_nt(keys_ref[1], q_ref[:, PEER_KEYS:2 * PEER_KEYS])
    top0, top1 = _top_values_sorted(s0, k + 1), _top_values_sorted(s1, k + 1)
    pad = [jnp.full_like(top0[0], -jnp.inf)] * 7
    t0, t1 = jnp.concatenate(top0 + pad, axis=0), jnp.concatenate(top1 + pad, axis=0)
    cand = jnp.concatenate([top0[0] + t1] + [top0[i] + t1[0:8] for i in range(1, 8)] + [t0[8:24] + top1[0]], axis=0)
    best = _top_values(cand, k + 1)
    z = None
    for v in best[:k]:
        e = jnp.exp(v - best[0])
        z = e if z is None else z + e
    s0_ref[0] = s0.reshape(s0_ref.shape[1:])
    s1_ref[0] = s1
    e0_ref[0] = (jnp.exp(s0 - top0[0]) / z).reshape(e0_ref.shape[1:])
    e1_ref[0] = jnp.exp(s1 - top1[0])
    th_ref[0] = 0.5 * (best[k - 1] + best[k])


def _peer_route_call(q, keys, tt=512):
    n = q.shape[0]
    n_heads = keys.shape[0] // 2
    big = jax.ShapeDtypeStruct((n_heads, PEER_KEYS, n), f32)
    bspec = pl.BlockSpec((1, PEER_KEYS, tt), lambda t, h: (h, 0, t))
    grp = jax.ShapeDtypeStruct((n_heads, PEER_KEYS // 8, 8, n), f32)
    gspec = pl.BlockSpec((1, PEER_KEYS // 8, 8, tt), lambda t, h: (h, 0, 0, t))
    return pl.pallas_call(
        _peer_route_kernel,
        grid=(n // tt, n_heads),
        in_specs=[pl.BlockSpec((tt, 2 * PEER_KEYS), lambda t, h: (t, h)),
                  pl.BlockSpec((2, PEER_KEYS, PEER_KEYS), lambda t, h: (h, 0, 0))],
        out_specs=[gspec, bspec, gspec, bspec, pl.BlockSpec((1, 1, tt), lambda t, h: (h, 0, t))],
        out_shape=[grp, big, grp, big, jax.ShapeDtypeStruct((n_heads, 1, n), f32)],
        compiler_params=_params("parallel", "arbitrary"),
        name="peer_route",
    )(q, keys)


PEER_I_PER_TILE = 4


def _peer_dense_kernel(xm_ref, u_ref, v_ref, s0_ref, e0_ref, s1_ref, e1_ref, th_ref, o_ref, s_s, w_s):
    e = pl.program_id(1)
    n_steps = pl.num_programs(1)

    @pl.when(e == 0)
    def _():
        o_ref[...] = jnp.zeros(o_ref.shape, f32)

    @pl.when(e > 0)
    def _():
        n_heads = s1_ref.shape[0]
        sub = ((e - 1) % (8 // PEER_I_PER_TILE)) * PEER_I_PER_TILE
        for il in range(PEER_I_PER_TILE):
            w = None
            for h in range(n_heads):
                s0 = s0_ref[h, 0, pl.ds(sub + il, 1), :]
                e0 = 0.5 * e0_ref[h, 0, pl.ds(sub + il, 1), :]
                keep = s1_ref[h] >= th_ref[h] - s0
                term = jnp.where(keep, e0 * e1_ref[h], 0.0)
                w = term if w is None else w + term
            w_s[il * PEER_KEYS:(il + 1) * PEER_KEYS, :] = w
        s = s_s[...]
        c = math.sqrt(2.0 / math.pi)
        cdf2 = 1.0 + jnp.tanh(s * (c + (c * 0.044715) * (s * s)))
        act = (s * cdf2 * w_s[...]).T.astype(bf16)
        o_ref[...] += _dot(act, v_ref[...])

    @pl.when(e < n_steps - 1)
    def _():
        s_s[...] = _dot_nt(u_ref[...], xm_ref[...])


def _peer_dense_call(xm, u_tab, v_tab, layer, s0, e0, s1, e1, th, tm=512):
    n, d = xm.shape
    n_exp = u_tab.shape[1]
    te = PEER_I_PER_TILE * PEER_KEYS
    n_tiles = n_exp // te
    n_heads = s1.shape[0]
    per8 = 8 // PEER_I_PER_TILE
    prev = lambda e: jnp.maximum(e - 1, 0)
    row8 = pl.BlockSpec((n_heads, 1, 8, tm), lambda t, e: (0, prev(e) // per8, 0, t))
    full = pl.BlockSpec((n_heads, PEER_KEYS, tm), lambda t, e: (0, 0, t))
    return pl.pallas_call(
        _peer_dense_kernel,
        grid=(n // tm, n_tiles + 1),
        in_specs=[pl.BlockSpec((tm, d), lambda t, e: (t, 0)),
                  pl.BlockSpec((None, te, d), lambda t, e: (layer, jnp.minimum(e, n_tiles - 1), 0)),
                  pl.BlockSpec((None, te, d), lambda t, e: (layer, prev(e), 0)),
                  row8, row8, full, full,
                  pl.BlockSpec((n_heads, 1, tm), lambda t, e: (0, 0, t))],
        out_specs=pl.BlockSpec((tm, d), lambda t, e: (t, 0)),
        out_shape=jax.ShapeDtypeStruct((n, d), f32),
        scratch_shapes=[pltpu.VMEM((te, tm), f32), pltpu.VMEM((te, tm), f32)],
        compiler_params=_params("parallel", "arbitrary", vmem=60 * 1024 * 1024),
        name="peer_dense",
    )(xm, u_tab, v_tab, s0, e0, s1, e1, th)


def kernel(x, c, ctx, c_ctx, w_ada, b_ada, w_in, a_q_norm, a_k_norm, b_sink, lru_conv_w, lru_conv_b, lru_w_r, lru_b_r, lru_w_i, lru_b_i, lru_lambda, s5_a_re, s5_a_im, s5_log_step, s5_b_re, s5_b_im, s5_c_re, s5_c_im, s5_d, s5_w_glu, s5_b_glu, w_gate, b_gate, w_branch, w_out, ln_g, ln_b, peer_w_q, peer_sub_keys, peer_u, peer_v):
    n_batch, n_lat, d = x.shape
    n_ctx = ctx.shape[1]
    depth = w_ada.shape[0]
    seq_tot = n_ctx + n_lat
    n_tok = n_batch * seq_tot
    mix_w = d // 4
    assert n_ctx % ROW_TILE == 0 and n_lat % ROW_TILE == 0 and n_tok % MM_TILE_M == 0
    assert n_ctx == ROW_TILE, "scan kernels treat row tile 0 of every batch element as the context"
    tiles_per_batch = seq_tot // ROW_TILE
    geom = (tiles_per_batch, n_ctx // ROW_TILE, n_batch)
    alpha = (2.0 * depth) ** 0.25
    s5_steps = seq_tot // 4
    assert seq_tot % 4 == 0 and s5_steps % (8 * S5_CHUNK) == 0 and n_ctx % S5_CHUNK == 0

    cin = jnp.zeros((8, d), f32).at[:n_batch].set(c).at[n_batch].set(c_ctx)
    mod = _ada_call(cin, w_ada, b_ada)

    def mod_vec(l, k):
        return mod[l, :, k * d:(k + 1) * d].reshape(8, 1, d)

    cos, sin = _rope_tables(n_lat, n_ctx)
    w_in_b, w_gate_b, w_branch_b, w_out_b = (w.astype(bf16) for w in (w_in, w_gate, w_branch, w_out))
    w_glu_b, w_q_b, u_tab_b, v_tab_b = (w.astype(bf16) for w in (s5_w_glu, peer_w_q, peer_u, peer_v))
    stream = (x, ctx)
    um = _mod_call(stream, n_tok, mod_vec(0, 1), mod_vec(0, 0), geom)

    qkv_cols = N_QKV_SLICES * HEAD_DIM
    for l in range(depth):
        z = _mm_call(um, w_in_b, l, f32, name="in_proj")
        qkv = _prep_call(z, cos, sin, a_q_norm[l], a_k_norm[l], n_batch, seq_tot)
        ya = _attn_a_call(qkv, n_ctx).reshape(n_tok, mix_w)
        yb = _attn_b_call(qkv, b_sink[l], n_ctx).reshape(n_tok, mix_w)

        lru_x_block = qkv_cols // mix_w
        a_coef, b_coef = _lru_gate_call(
            z, lru_x_block, lru_conv_w[l], lru_conv_b[l], lru_w_r[l].astype(bf16), lru_b_r[l],
            lru_w_i[l].astype(bf16), lru_b_i[l], jax.nn.softplus(-lru_lambda[l]), tiles_per_batch)
        hf, hr = _lru_scan_call(a_coef, b_coef, n_batch, tiles_per_batch)
        yr = _lru_out_call(hf, hr, z, lru_x_block + 1)

        s5_block0 = (qkv_cols + 2 * mix_w) // 128
        k_cmp, w_cmp, m_cmp, lam_r = _s5_weights(s5_a_re[l], s5_a_im[l], s5_log_step[l], s5_b_re[l],
                                                  s5_b_im[l], s5_c_re[l], s5_c_im[l])
        e_parts = _s5_inject_call(z, s5_block0, w_cmp, s5_steps)
        h_parts = _s5_scan_call(e_parts, lam_r, n_batch, n_ctx // S5_CHUNK)
        y_s5 = _s5_read_call(z, s5_block0, h_parts, k_cmp, m_cmp, s5_d[l], s5_steps)
        ys = _glu_call(y_s5, w_glu_b, l, s5_b_glu[l])

        merged = _merge_call(um, (ya, yb, yr, ys), w_gate_b, b_gate[l], w_branch_b, l)
        y = _mm_call(merged, w_out_b, l, f32, name="out_proj")
        xs, um = _ln_call(stream, y, mod_vec(l, 2), ln_g[l, 0], ln_b[l, 0], (mod_vec(l, 4), mod_vec(l, 3)), geom, alpha)

        q = _mm_call(um, w_q_b, l, bf16, name="peer_query")
        n_heads = peer_sub_keys.shape[1]
        keys = peer_sub_keys[l].astype(bf16).reshape(2 * n_heads, PEER_KEYS, -1)
        s0, s1, e0, e1, th = _peer_route_call(q, keys)
        f = _peer_dense_call(um, u_tab_b, v_tab_b, l, s0, e0, s1, e1, th)
        if l == depth - 1:
            return _ln_call((xs,), f, mod_vec(l, 5), ln_g[l, 1], ln_b[l, 1], None, geom, alpha)[0]
        xs, um = _ln_call((xs,), f, mod_vec(l, 5), ln_g[l, 1], ln_b[l, 1],
                          (mod_vec(l + 1, 1), mod_vec(l + 1, 0)), geom, alpha)
        stream = (xs,)
```

```python
import functools
import math

import jax
import jax.numpy as jnp
from jax import lax
from jax.experimental import pallas as pl
from jax.experimental.pallas import tpu as pltpu

f32 = jnp.float32
bf16 = jnp.bfloat16

HEAD_DIM = 128
GRID_W = 64
ROPE_THETA = 10000.0
WINDOW = 128
Q_TILE = 128
KV_GROUP = 4
LRU_C = 8.0
LRU_BLOCKS = 8
S5_GROUP = 16
S5_STATE = 64
S5_CHUNK = 8
S5_OCT = 8
PEER_TOPK = 16
PEER_KEYS = 128
LN_EPS = 1e-6
NEG = -1e30
ROW_TILE = 256
MM_TILE_M = 512
PROJ_TILE_M = 1536
MERGE_TILE_M = 768
V7X_VMEM_LIMIT = 56 * 1024 * 1024
V7X_VMEM_LIMIT_PEER = 60 * 1024 * 1024


def _params(*sem, vmem=V7X_VMEM_LIMIT):
    return pltpu.CompilerParams(dimension_semantics=sem, vmem_limit_bytes=vmem)


def _dot(a, b):
    return jnp.dot(a, b, preferred_element_type=f32)


def _dot_nt(a, b):
    return lax.dot_general(a, b, (((1,), (1,)), ((), ())), preferred_element_type=f32)


def _gelu(x):
    return x * (0.5 * (1.0 + jnp.tanh(math.sqrt(2.0 / math.pi) * (x + 0.044715 * (x * x * x)))))


def _sigmoid(x):
    return 0.5 * (1.0 + jnp.tanh(0.5 * x))


def _ada_kernel(c_ref, w_ref, b_ref, o_ref):
    c = c_ref[...]
    a = (c * _sigmoid(c)).astype(bf16)
    o_ref[0] = _dot(a, w_ref[0].astype(bf16)) + b_ref[0]


def _ada_call(cin, w_ada, b_ada, tn=512):
    depth, d, n = w_ada.shape
    return pl.pallas_call(
        _ada_kernel,
        grid=(depth, n // tn),
        in_specs=[pl.BlockSpec((8, d), lambda l, j: (0, 0)),
                  pl.BlockSpec((1, d, tn), lambda l, j: (l, 0, j)),
                  pl.BlockSpec((1, 1, tn), lambda l, j: (l, 0, j))],
        out_specs=pl.BlockSpec((1, 8, tn), lambda l, j: (l, 0, j)),
        out_shape=jax.ShapeDtypeStruct((depth, 8, n), f32),
        compiler_params=_params("parallel", "parallel"),
        name="adaln",
    )(cin, w_ada, b_ada.reshape(depth, 1, n))


def _row_group(i, tiles_per_batch, ctx_tiles, n_batch):
    return jnp.where(i % tiles_per_batch < ctx_tiles, n_batch, i // tiles_per_batch)


def _stream_specs(stream, geom, d):
    tpb = geom[0]
    if len(stream) == 1:
        return [pl.BlockSpec((ROW_TILE, d), lambda i: (i, 0))]
    return [pl.BlockSpec((1, ROW_TILE, d), lambda i: (i // tpb, jnp.maximum(i % tpb - 1, 0), 0)),
            pl.BlockSpec((1, ROW_TILE, d), lambda i: (i // tpb, 0, 0))]


def _read_stream(refs, is_ctx):
    if len(refs) == 1:
        return refs[0][...]
    return jnp.where(is_ctx, refs[1][0], refs[0][0])


def _mod_kernel(*refs, n_stream, tiles_per_batch):
    sc_ref, sh_ref, o_ref = refs[n_stream:]
    x = _read_stream(refs[:n_stream], pl.program_id(0) % tiles_per_batch == 0)
    o_ref[...] = (x * (1.0 + sc_ref[0]) + sh_ref[0]).astype(o_ref.dtype)


def _mod_call(stream, n, sc, sh, geom):
    d = sc.shape[-1]
    tpb, ct, nb = geom
    sel = lambda i: (_row_group(i, tpb, ct, nb), 0, 0)
    return pl.pallas_call(
        functools.partial(_mod_kernel, n_stream=len(stream), tiles_per_batch=tpb),
        grid=(n // ROW_TILE,),
        in_specs=_stream_specs(stream, geom, d) + [pl.BlockSpec((1, 1, d), sel), pl.BlockSpec((1, 1, d), sel)],
        out_specs=pl.BlockSpec((ROW_TILE, d), lambda i: (i, 0)),
        out_shape=jax.ShapeDtypeStruct((n, d), bf16),
        compiler_params=_params("parallel"),
        name="modulate",
    )(*stream, sc, sh)


def _ln_kernel(*refs, n_stream, tiles_per_batch, alpha, last):
    y_ref, g_ref, lg_ref, lb_ref = refs[n_stream:n_stream + 4]
    is_ctx = pl.program_id(0) % tiles_per_batch == 0
    v = alpha * _read_stream(refs[:n_stream], is_ctx) + g_ref[0] * y_ref[...]
    mu = jnp.mean(v, axis=-1, keepdims=True)
    vc = v - mu
    var = jnp.mean(vc * vc, axis=-1, keepdims=True)
    o = vc * lax.rsqrt(var + LN_EPS) * lg_ref[...] + lb_ref[...]
    if last:
        xo_ref, = refs[n_stream + 4:]

        @pl.when(jnp.logical_not(is_ctx))
        def _():
            xo_ref[0] = o
    else:
        sc_ref, sh_ref, xo_ref, xm_ref = refs[n_stream + 4:]
        xo_ref[...] = o
        xm_ref[...] = (o * (1.0 + sc_ref[0]) + sh_ref[0]).astype(xm_ref.dtype)


def _ln_call(stream, y, gate, ln_g, ln_b, nxt_mod, geom, alpha):
    n, d = y.shape
    tpb, ct, nb = geom
    sel = lambda i: (_row_group(i, tpb, ct, nb), 0, 0)
    row = pl.BlockSpec((ROW_TILE, d), lambda i: (i, 0))
    vec = pl.BlockSpec((1, d), lambda i: (0, 0))
    mod = pl.BlockSpec((1, 1, d), sel)
    last = nxt_mod is None
    in_specs = _stream_specs(stream, geom, d) + [row, mod, vec, vec]
    args = list(stream) + [y, gate, ln_g.reshape(1, d), ln_b.reshape(1, d)]
    if last:
        out_specs = [pl.BlockSpec((1, ROW_TILE, d), lambda i: (i // tpb, jnp.maximum(i % tpb - 1, 0), 0))]
        out_shape = [jax.ShapeDtypeStruct((nb, n // nb - ct * ROW_TILE, d), f32)]
    else:
        in_specs += [mod, mod]
        args += list(nxt_mod)
        out_specs = [row, row]
        out_shape = [jax.ShapeDtypeStruct((n, d), f32), jax.ShapeDtypeStruct((n, d), bf16)]
    return pl.pallas_call(
        functools.partial(_ln_kernel, n_stream=len(stream), tiles_per_batch=tpb, alpha=alpha, last=last),
        grid=(n // ROW_TILE,),
        in_specs=in_specs, out_specs=out_specs, out_shape=out_shape,
        compiler_params=_params("arbitrary" if last else "parallel"),
        name="residual_ln",
    )(*args)


def _mm_kernel(a_ref, b_ref, o_ref):
    o_ref[...] = _dot(a_ref[...], b_ref[...]).astype(o_ref.dtype)


def _mm_call(a, b, layer, out_dtype, tn=512, name="matmul"):
    m, k = a.shape
    n = b.shape[2]
    tm = PROJ_TILE_M if m % PROJ_TILE_M == 0 else MM_TILE_M
    return pl.pallas_call(
        _mm_kernel,
        grid=(m // tm, n // tn),
        in_specs=[pl.BlockSpec((tm, k), lambda i, j: (i, 0)),
                  pl.BlockSpec((None, k, tn), lambda i, j: (layer, 0, j))],
        out_specs=pl.BlockSpec((tm, tn), lambda i, j: (i, j)),
        out_shape=jax.ShapeDtypeStruct((m, n), out_dtype),
        compiler_params=_params("parallel", "arbitrary"),
        name=name,
    )(a, b)


def _glu_kernel(a_ref, wv_ref, wg_ref, bv_ref, bg_ref, o_ref):
    a = a_ref[...].astype(bf16)
    val = _dot(a, wv_ref[...]) + bv_ref[...]
    gate = _dot(a, wg_ref[...]) + bg_ref[...]
    o_ref[...] = (val * _sigmoid(gate)).astype(o_ref.dtype)


def _glu_call(a, w, layer, b, tn=512, tm=MM_TILE_M):
    m, k = a.shape
    n = w.shape[2] // 2
    nj = n // tn
    b2 = b.reshape(1, 2 * n)
    return pl.pallas_call(
        _glu_kernel,
        grid=(m // tm, nj),
        in_specs=[pl.BlockSpec((tm, k), lambda i, j: (i, 0)),
                  pl.BlockSpec((None, k, tn), lambda i, j: (layer, 0, j)),
                  pl.BlockSpec((None, k, tn), lambda i, j: (layer, 0, j + nj)),
                  pl.BlockSpec((1, tn), lambda i, j: (0, j)),
                  pl.BlockSpec((1, tn), lambda i, j: (0, j + nj))],
        out_specs=pl.BlockSpec((tm, tn), lambda i, j: (i, j)),
        out_shape=jax.ShapeDtypeStruct((m, n), bf16),
        compiler_params=_params("parallel", "arbitrary"),
        name="s5_glu",
    )(a, w, w, b2, b2)


def _merge_kernel(u_ref, ya_ref, yb_ref, yr_ref, ys_ref, wg_ref, bg_ref, wb_ref, o_ref):
    u = u_ref[...]
    acc = None
    for i, y_ref in enumerate((ya_ref, yb_ref, yr_ref, ys_ref)):
        gate = _sigmoid(_dot(u, wg_ref[i]) + bg_ref[i])
        term = gate * _dot(y_ref[...], wb_ref[i])
        acc = term if acc is None else acc + term
    o_ref[...] = acc.astype(o_ref.dtype)


def _merge_call(u, branches, w_gate, b_gate, w_branch, layer, tn=256):
    m, d = u.shape
    tm = MERGE_TILE_M if m % MERGE_TILE_M == 0 else MM_TILE_M
    _, nbr, bw, n = w_branch.shape
    ybr = pl.BlockSpec((tm, bw), lambda i, j: (i, 0))
    return pl.pallas_call(
        _merge_kernel,
        grid=(m // tm, n // tn),
        in_specs=[pl.BlockSpec((tm, d), lambda i, j: (i, 0)), ybr, ybr, ybr, ybr,
                  pl.BlockSpec((None, nbr, d, tn), lambda i, j: (layer, 0, 0, j)),
                  pl.BlockSpec((nbr, 1, tn), lambda i, j: (0, 0, j)),
                  pl.BlockSpec((None, nbr, bw, tn), lambda i, j: (layer, 0, 0, j))],
        out_specs=pl.BlockSpec((tm, tn), lambda i, j: (i, j)),
        out_shape=jax.ShapeDtypeStruct((m, n), bf16),
        compiler_params=_params("parallel", "arbitrary"),
        name="branch_merge",
    )(u, *branches, w_gate, b_gate.reshape(nbr, 1, n), w_branch)


def _rope_tables(n_lat, n_ctx):
    t = jnp.arange(n_lat)
    row = (t // GRID_W).astype(f32)
    col = (t % GRID_W).astype(f32)
    n_freq = HEAD_DIM // 4
    inv = ROPE_THETA ** (-jnp.arange(n_freq, dtype=f32) / n_freq)
    ang = jnp.concatenate([row[:, None] * inv, col[:, None] * inv], axis=-1)
    cos = jnp.repeat(jnp.cos(ang), 2, axis=-1)
    sin = jnp.repeat(jnp.sin(ang), 2, axis=-1) * jnp.tile(jnp.array([-1.0, 1.0], f32), HEAD_DIM // 2)
    cos = jnp.concatenate([jnp.ones((n_ctx, HEAD_DIM), f32), cos], axis=0)
    sin = jnp.concatenate([jnp.zeros((n_ctx, HEAD_DIM), f32), sin], axis=0)
    return cos, sin


N_QKV_SLICES = 24
A_Q, A_K, A_V, B_Q, B_K, B_V = 0, 8, 10, 12, 20, 22


def _prep_kernel(z_ref, cos_ref, sin_ref, gq_ref, gk_ref, o_ref):
    cos, sin = cos_ref[...], sin_ref[...]
    even = lax.broadcasted_iota(jnp.int32, cos.shape, 1) % 2 == 0
    for s in range(N_QKV_SLICES):
        y = z_ref[:, s * HEAD_DIM:(s + 1) * HEAD_DIM]
        if s < A_V:
            gain = gq_ref[...] if s < A_K else gk_ref[...]
            y = y * lax.rsqrt(jnp.mean(y * y, axis=-1, keepdims=True) + LN_EPS) * gain
        if s < A_V or B_Q <= s < B_V:
            partner = jnp.where(even, pltpu.roll(y, HEAD_DIM - 1, axis=1), pltpu.roll(y, 1, axis=1))
            y = y * cos + partner * sin
        if s < A_K or B_Q <= s < B_K:
            y = y * (HEAD_DIM ** -0.5)
        o_ref[0, s] = y.astype(o_ref.dtype)


def _prep_call(z, cos, sin, gain_q, gain_k, n_batch, seq_tot):
    n = z.shape[0]
    tr = ROW_TILE
    tpb = seq_tot // tr
    width = N_QKV_SLICES * HEAD_DIM
    vec = pl.BlockSpec((1, HEAD_DIM), lambda i: (0, 0))
    return pl.pallas_call(
        _prep_kernel,
        grid=(n // tr,),
        in_specs=[pl.BlockSpec((tr, width), lambda i: (i, 0)),
                  pl.BlockSpec((tr, HEAD_DIM), lambda i: (i % tpb, 0)),
                  pl.BlockSpec((tr, HEAD_DIM), lambda i: (i % tpb, 0)), vec, vec],
        out_specs=pl.BlockSpec((1, N_QKV_SLICES, tr, HEAD_DIM), lambda i: (i // tpb, 0, i % tpb, 0)),
        out_shape=jax.ShapeDtypeStruct((n_batch, N_QKV_SLICES, seq_tot, HEAD_DIM), bf16),
        compiler_params=_params("parallel"),
        name="qkv_prep",
    )(z, cos, sin, gain_q.reshape(1, HEAD_DIM), gain_k.reshape(1, HEAD_DIM))


def _attn_a_kernel(q_ref, k_ref, v_ref, o_ref, m_s, acc_s, *, n_ctx, tk, n_lat_chunks):
    q = q_ref[0].reshape(KV_GROUP * Q_TILE, HEAD_DIM)

    def values(lo, n):
        return jnp.concatenate([v_ref[0, 0, lo:lo + n, :], jnp.ones((n, HEAD_DIM), bf16)], axis=1)


    s = _dot_nt(q, k_ref[0, 0, 0:n_ctx, :]).astype(bf16)
    m0 = jnp.max(s, axis=-1, keepdims=True)
    m_s[...] = m0.astype(f32)
    acc_s[...] = _dot(jnp.exp(s - m0), values(0, n_ctx))

    @pl.when(pl.program_id(2) >= n_ctx // Q_TILE)
    def _():
        for c in range(n_lat_chunks):
            lo = n_ctx + c * tk
            s = _dot_nt(q, k_ref[0, 0, lo:lo + tk, :]).astype(bf16)
            m_prev = m_s[...]
            m_new = jnp.maximum(m_prev, jnp.max(s, axis=-1, keepdims=True).astype(f32))
            p = jnp.exp(s - m_new.astype(bf16))
            acc_s[...] = jnp.exp(m_prev - m_new) * acc_s[...] + _dot(p, values(lo, tk))
            m_s[...] = m_new

    out = acc_s[:, 0:HEAD_DIM] / acc_s[:, HEAD_DIM:2 * HEAD_DIM]
    for g in range(KV_GROUP):
        o_ref[0, :, g * HEAD_DIM:(g + 1) * HEAD_DIM] = out[g * Q_TILE:(g + 1) * Q_TILE].astype(o_ref.dtype)


def _attn_a_call(qkv, n_ctx, tk=1024):
    nb, _, seq_tot, _ = qkv.shape
    n_kv = (A_V - A_K)
    rows = KV_GROUP * Q_TILE
    n_lat = seq_tot - n_ctx
    tk = min(tk, n_lat)
    assert n_lat % tk == 0
    kern = functools.partial(_attn_a_kernel, n_ctx=n_ctx, tk=tk, n_lat_chunks=n_lat // tk)
    return pl.pallas_call(
        kern,
        grid=(nb, n_kv, seq_tot // Q_TILE),
        in_specs=[pl.BlockSpec((1, KV_GROUP, Q_TILE, HEAD_DIM), lambda b, h, i: (b, A_Q // KV_GROUP + h, i, 0)),
                  pl.BlockSpec((1, 1, seq_tot, HEAD_DIM), lambda b, h, i: (b, A_K + h, 0, 0)),
                  pl.BlockSpec((1, 1, seq_tot, HEAD_DIM), lambda b, h, i: (b, A_V + h, 0, 0))],
        out_specs=pl.BlockSpec((1, Q_TILE, KV_GROUP * HEAD_DIM), lambda b, h, i: (b, i, h)),
        out_shape=jax.ShapeDtypeStruct((nb, seq_tot, n_kv * KV_GROUP * HEAD_DIM), bf16),
        scratch_shapes=[pltpu.VMEM((rows, 1), f32), pltpu.VMEM((rows, 2 * HEAD_DIM), f32)],
        compiler_params=_params("parallel", "parallel", "arbitrary"),
        name="dense_attention",
    )(qkv, qkv, qkv)


def _attn_b_kernel(q_ref, k_ref, v_ref, sink_ref, o_ref, *, n_ctx, n_blk):
    i = pl.program_id(2)
    blk = i - n_ctx // Q_TILE
    rows = KV_GROUP * Q_TILE
    q = q_ref[0].reshape(rows, HEAD_DIM)
    sink = sink_ref[0]
    kc = k_ref[0, 0, 0:n_ctx, :]
    vc = v_ref[0, 0, 0:n_ctx, :]
    s_ctx = _dot_nt(q, kc)
    m = jnp.maximum(jnp.max(s_ctx, axis=-1, keepdims=True), sink)
    span = 3 * Q_TILE
    first = jnp.clip(blk - 1, 0, n_blk - 3) * Q_TILE
    start = pl.multiple_of(n_ctx + first, Q_TILE)
    q_pos = blk * Q_TILE + lax.broadcasted_iota(jnp.int32, (rows, span), 0) % Q_TILE
    k_pos = first + lax.broadcasted_iota(jnp.int32, (rows, span), 1)
    keep = jnp.logical_and(jnp.abs(k_pos - q_pos) <= WINDOW, blk >= 0)
    s_loc = jnp.where(keep, _dot_nt(q, k_ref[0, 0, pl.ds(start, span), :]), NEG)
    m = jnp.maximum(m, jnp.max(s_loc, axis=-1, keepdims=True))

    def values(v):
        return jnp.concatenate([v, jnp.ones(v.shape, bf16)], axis=1)

    acc = _dot(jnp.exp((s_ctx - m).astype(bf16)), values(vc))
    acc = acc + _dot(jnp.exp((s_loc - m).astype(bf16)), values(v_ref[0, 0, pl.ds(start, span), :]))
    out = acc[:, 0:HEAD_DIM] / (jnp.exp(sink - m) + acc[:, HEAD_DIM:2 * HEAD_DIM])
    for g in range(KV_GROUP):
        o_ref[0, :, g * HEAD_DIM:(g + 1) * HEAD_DIM] = out[g * Q_TILE:(g + 1) * Q_TILE].astype(o_ref.dtype)


def _attn_b_call(qkv, sink, n_ctx):
    nb, _, seq_tot, _ = qkv.shape
    n_kv = (B_V - B_K)
    rows = KV_GROUP * Q_TILE
    sink_rows = jnp.repeat(sink.astype(f32).reshape(n_kv, KV_GROUP), Q_TILE, axis=1).reshape(n_kv, rows, 1)
    n_blk = (seq_tot - n_ctx) // Q_TILE
    assert n_blk >= 3, "the windowed mixer reads a span of three key blocks"
    kern = functools.partial(_attn_b_kernel, n_ctx=n_ctx, n_blk=n_blk)
    return pl.pallas_call(
        kern,
        grid=(nb, n_kv, seq_tot // Q_TILE),
        in_specs=[pl.BlockSpec((1, KV_GROUP, Q_TILE, HEAD_DIM), lambda b, h, i: (b, B_Q // KV_GROUP + h, i, 0)),
                  pl.BlockSpec((1, 1, seq_tot, HEAD_DIM), lambda b, h, i: (b, B_K + h, 0, 0)),
                  pl.BlockSpec((1, 1, seq_tot, HEAD_DIM), lambda b, h, i: (b, B_V + h, 0, 0)),
                  pl.BlockSpec((1, rows, 1), lambda b, h, i: (h, 0, 0))],
        out_specs=pl.BlockSpec((1, Q_TILE, KV_GROUP * HEAD_DIM), lambda b, h, i: (b, i, h)),
        out_shape=jax.ShapeDtypeStruct((nb, seq_tot, n_kv * KV_GROUP * HEAD_DIM), bf16),
        compiler_params=_params("parallel", "parallel", "arbitrary"),
        name="window_attention",
    )(qkv, qkv, qkv, sink_rows)


HALO = 8


def _lru_gate_kernel(x_ref, xp_ref, xn_ref, cw_ref, cb_ref, wr_ref, br_ref, wi_ref, bi_ref, sp_ref,
                     a_ref, b_ref, pad_s, *, tiles_per_batch):
    tr = x_ref.shape[0]
    t = pl.program_id(0) % tiles_per_batch
    has_prev = t >= 2
    has_next = jnp.logical_and(t >= 1, t < tiles_per_batch - 1)
    x = x_ref[...]
    pad_s[0:HALO, :] = jnp.where(has_prev, xp_ref[...], 0.0)
    pad_s[HALO:HALO + tr, :] = x
    pad_s[HALO + tr:2 * HALO + tr, :] = jnp.where(has_next, xn_ref[...], 0.0)
    cw = cw_ref[...]
    xl = cb_ref[...] + pad_s[HALO - 2:HALO - 2 + tr, :] * cw[0:1]
    xl = xl + pad_s[HALO - 1:HALO - 1 + tr, :] * cw[1:2]
    xl = xl + x * cw[2:3]
    xl = xl + pad_s[HALO + 1:HALO + 1 + tr, :] * cw[3:4]
    bw = xl.shape[1] // LRU_BLOCKS
    for n in range(LRU_BLOCKS):
        cols = slice(n * bw, (n + 1) * bw)
        xb = xl[:, cols]
        xb16 = xb.astype(bf16)
        for d in range(2):
            r = _sigmoid(_dot(xb16, wr_ref[d, n]) + br_ref[d:d + 1, cols])
            gi = _sigmoid(_dot(xb16, wi_ref[d, n]) + bi_ref[d:d + 1, cols])
            log_a = (-LRU_C) * r * sp_ref[d:d + 1, cols]
            a = jnp.exp(log_a)
            a_ref[d, :, cols] = a
            b_ref[d, :, cols] = jnp.sqrt(1.0 - a * a) * gi * xb


def _lru_gate_call(z, col_block, conv_w, conv_b, w_r, b_r, w_i, b_i, softplus_lam, tiles_per_batch):
    n = z.shape[0]
    w = conv_w.shape[1]
    tr = ROW_TILE
    hb = tr // HALO
    n_halo = n // HALO
    full = lambda shape: pl.BlockSpec(shape, lambda i: (0,) * len(shape))
    kern = functools.partial(_lru_gate_kernel, tiles_per_batch=tiles_per_batch)
    out = jax.ShapeDtypeStruct((2, n, w), f32)
    return pl.pallas_call(
        kern,
        grid=(n // tr,),
        in_specs=[pl.BlockSpec((tr, w), lambda i: (i, col_block)),
                  pl.BlockSpec((HALO, w), lambda i: (jnp.maximum(i * hb - 1, 0), col_block)),
                  pl.BlockSpec((HALO, w), lambda i: (jnp.minimum((i + 1) * hb, n_halo - 1), col_block)),
                  full(conv_w.shape), full((1, w)), full(w_r.shape), full(b_r.shape),
                  full(w_i.shape), full(b_i.shape), full(softplus_lam.shape)],
        out_specs=[pl.BlockSpec((2, tr, w), lambda i: (0, i, 0)), pl.BlockSpec((2, tr, w), lambda i: (0, i, 0))],
        out_shape=[out, out],
        scratch_shapes=[pltpu.VMEM((tr + 2 * HALO, w), f32)],
        compiler_params=_params("parallel"),
        name="lru_gates",
    )(z, z, z, conv_w, conv_b.reshape(1, w), w_r, b_r, w_i, b_i, softplus_lam)


def _lru_scan_kernel(af_ref, bf_ref, ar_ref, br_ref, hf_ref, hr_ref, sf, sr):
    tr = hf_ref.shape[0]

    @pl.when(pl.program_id(1) == 0)
    def _():
        sf[...] = jnp.zeros(sf.shape, f32)
        sr[...] = jnp.zeros(sr.shape, f32)

    def body(t, carry):
        hf, hr = carry
        hf = af_ref[0, pl.ds(t, 1), :] * hf + bf_ref[0, pl.ds(t, 1), :]
        hf_ref[pl.ds(t, 1), :] = hf
        u = tr - 1 - t
        hr = ar_ref[0, pl.ds(u, 1), :] * hr + br_ref[0, pl.ds(u, 1), :]
        hr_ref[pl.ds(u, 1), :] = hr
        return hf, hr

    hf, hr = lax.fori_loop(0, tr, body, (sf[...], sr[...]), unroll=8)
    sf[...] = hf
    sr[...] = hr


def _scan_tile_maps(tiles_per_batch):
    fwd = lambda b, i: b * tiles_per_batch + i
    rev = lambda b, i: b * tiles_per_batch + jnp.where(i == 0, 0, tiles_per_batch - i)
    return fwd, rev


def _lru_scan_call(a, b, n_batch, tiles_per_batch):
    _, n, w = a.shape
    tr = ROW_TILE
    fwd, rev = _scan_tile_maps(tiles_per_batch)
    out = jax.ShapeDtypeStruct((n, w), f32)
    return pl.pallas_call(
        _lru_scan_kernel,
        grid=(n_batch, tiles_per_batch),
        in_specs=[pl.BlockSpec((1, tr, w), lambda bb, i: (0, fwd(bb, i), 0)),
                  pl.BlockSpec((1, tr, w), lambda bb, i: (0, fwd(bb, i), 0)),
                  pl.BlockSpec((1, tr, w), lambda bb, i: (1, rev(bb, i), 0)),
                  pl.BlockSpec((1, tr, w), lambda bb, i: (1, rev(bb, i), 0))],
        out_specs=[pl.BlockSpec((tr, w), lambda bb, i: (fwd(bb, i), 0)),
                   pl.BlockSpec((tr, w), lambda bb, i: (rev(bb, i), 0))],
        out_shape=[out, out],
        scratch_shapes=[pltpu.VMEM((1, w), f32), pltpu.VMEM((1, w), f32)],
        compiler_params=_params("parallel", "arbitrary"),
        name="lru_scan",
    )(a, b, a, b)


def _lru_out_kernel(hf_ref, hr_ref, g_ref, o_ref):
    o_ref[...] = ((hf_ref[...] + hr_ref[...]) * _gelu(g_ref[...])).astype(o_ref.dtype)


def _lru_out_call(hf, hr, z, gate_col_block):
    n, w = hf.shape
    tr = ROW_TILE
    row = pl.BlockSpec((tr, w), lambda i: (i, 0))
    return pl.pallas_call(
        _lru_out_kernel,
        grid=(n // tr,),
        in_specs=[row, row, pl.BlockSpec((tr, w), lambda i: (i, gate_col_block))],
        out_specs=row,
        out_shape=jax.ShapeDtypeStruct((n, w), bf16),
        compiler_params=_params("parallel"),
        name="lru_out",
    )(hf, hr, z)


def _s5_weights(a_re, a_im, log_step, b_re, b_im, c_re, c_im):
    r = S5_CHUNK
    n_g = a_re.shape[1]
    n_oct = n_g // S5_OCT
    step = jnp.exp(log_step)[..., None]
    mag = jnp.exp(a_re * step)
    lb_re, lb_im = mag * jnp.cos(a_im * step), mag * jnp.sin(a_im * step)
    den = a_re * a_re + a_im * a_im
    num_re = lb_re - 1.0
    coef_re = (num_re * a_re + lb_im * a_im) / den
    coef_im = (lb_im * a_re - num_re * a_im) / den
    bb_re = coef_re[..., None] * b_re - coef_im[..., None] * b_im
    bb_im = coef_re[..., None] * b_im + coef_im[..., None] * b_re
    tau = jnp.arange(r + 1, dtype=f32)[:, None, None, None]
    pmag = jnp.exp(tau * (a_re * step))
    pw_re, pw_im = pmag * jnp.cos(tau * (a_im * step)), pmag * jnp.sin(tau * (a_im * step))

    lb_b_re = pw_re[..., None] * bb_re - pw_im[..., None] * bb_im
    lb_b_im = pw_re[..., None] * bb_im + pw_im[..., None] * bb_re
    lag = (jnp.einsum('tdgpc,dgop->tdgco', lb_b_re, c_re) - jnp.einsum('tdgpc,dgop->tdgco', lb_b_im, c_im))
    idx = jnp.arange(r)
    sh_f = (idx[None, None, :] - idx[None, :, None] == jnp.arange(r + 1)[:, None, None]).astype(f32)
    sh_r = (idx[None, :, None] - idx[None, None, :] == jnp.arange(r + 1)[:, None, None]).astype(f32)
    k_loc = jnp.einsum('tio,tgcd->gicod', sh_f, lag[:, 0]) + jnp.einsum('tio,tgcd->gicod', sh_r, lag[:, 1])
    k_cmp = k_loc.reshape(n_oct, S5_OCT, r, S5_GROUP, r * S5_GROUP)
    k_cmp = k_cmp.transpose(0, 2, 1, 3, 4).reshape(n_oct, r * 128, r * S5_GROUP)

    def inject(d, powers):
        w_re = pw_re[powers, d][..., None] * bb_re[d] - pw_im[powers, d][..., None] * bb_im[d]
        w_im = pw_re[powers, d][..., None] * bb_im[d] + pw_im[powers, d][..., None] * bb_re[d]
        out = []
        for w in (w_re, w_im):
            w = w.reshape(r, n_oct, S5_OCT, S5_STATE, S5_GROUP).transpose(1, 0, 2, 4, 3)
            out.append(w.reshape(n_oct, r * 128, S5_STATE))
        return out
    w_cmp = jnp.concatenate(inject(0, idx[::-1]) + inject(1, idx), axis=-1)

    def readout(d, powers):
        cl_re = c_re[d][None] * pw_re[powers, d][:, :, None, :] - c_im[d][None] * pw_im[powers, d][:, :, None, :]
        cl_im = c_re[d][None] * pw_im[powers, d][:, :, None, :] + c_im[d][None] * pw_re[powers, d][:, :, None, :]
        out = []
        for w in (cl_re, -cl_im):
            w = w.reshape(r, n_oct, S5_OCT, S5_GROUP, S5_STATE).transpose(1, 2, 4, 0, 3)
            out.append(w.reshape(n_oct, S5_OCT * S5_STATE, r * S5_GROUP))
        return out
    m_cmp = jnp.concatenate(readout(0, idx + 1) + readout(1, r - idx), axis=1)

    lam_r = jnp.stack([pw_re[r, 0].reshape(-1), pw_im[r, 0].reshape(-1),
                       pw_re[r, 1].reshape(-1), pw_im[r, 1].reshape(-1)])
    return k_cmp.astype(bf16), w_cmp.astype(bf16), m_cmp.astype(bf16), lam_r


def _expand_octet(compact, inner, row_inner):
    n_rows, n_cols = compact.shape
    wide = n_cols * S5_OCT
    log_inner, log_row = inner.bit_length() - 1, row_inner.bit_length() - 1
    oct_bits = S5_OCT.bit_length() - 1
    src = lax.broadcasted_iota(jnp.int32, (n_cols, wide), 0)
    dst = lax.broadcasted_iota(jnp.int32, (n_cols, wide), 1)
    dst_compact = ((dst >> (log_inner + oct_bits)) << log_inner) + (dst & (inner - 1))
    spread = _dot(compact, (src == dst_compact).astype(compact.dtype))
    row_g = (lax.broadcasted_iota(jnp.int32, (n_rows, wide), 0) >> log_row) & (S5_OCT - 1)
    col_g = (lax.broadcasted_iota(jnp.int32, (n_rows, wide), 1) >> log_inner) & (S5_OCT - 1)
    return jnp.where(row_g == col_g, spread, 0.0).astype(compact.dtype)


def _chunk_rows(u_ref):
    n_rows = u_ref.shape[0] // S5_CHUNK
    return [u_ref[pl.ds(i, n_rows, stride=S5_CHUNK), :] for i in range(S5_CHUNK)]


def _s5_inject_kernel(u_ref, w_ref, efr_ref, efi_ref, err_ref, eri_ref, w_s):
    @pl.when(pl.program_id(1) == 0)
    def _():
        w_s[...] = _expand_octet(w_ref[0], S5_STATE, S5_GROUP)

    x = jnp.concatenate(_chunk_rows(u_ref), axis=1).astype(bf16)
    e = _dot(x, w_s[...])
    w = efr_ref.shape[1]
    for k, o_ref in enumerate((efr_ref, efi_ref, err_ref, eri_ref)):
        o_ref[...] = e[:, k * w:(k + 1) * w]


def _s5_inject_call(z, col_block0, w_cmp, steps):
    n = z.shape[0]
    n_oct, n_in, n_cmp = w_cmp.shape
    sw = S5_OCT * S5_STATE
    rows = steps // S5_CHUNK
    out = jax.ShapeDtypeStruct((n // S5_CHUNK, n_oct * sw), f32)
    ospec = pl.BlockSpec((rows, sw), lambda k, i: (i, k))
    return pl.pallas_call(
        _s5_inject_kernel,
        grid=(n_oct, n // steps),
        in_specs=[pl.BlockSpec((steps, 128), lambda k, i: (i, col_block0 + k)),
                  pl.BlockSpec((1, n_in, n_cmp), lambda k, i: (k, 0, 0))],
        out_specs=[ospec] * 4,
        out_shape=[out] * 4,
        scratch_shapes=[pltpu.VMEM((n_in, n_cmp * S5_OCT), bf16)],
        compiler_params=_params("arbitrary", "arbitrary"),
        name="s5_inject",
    )(z, w_cmp)


def _s5_scan_kernel(efr_ref, efi_ref, err_ref, eri_ref, lam_ref, hfr_ref, hfi_ref, hrr_ref, hri_ref, *, n_ctx):
    n_rows = efr_ref.shape[0]
    lfr, lfi, lrr, lri = (lam_ref[k:k + 1, :] for k in range(4))
    zero = jnp.zeros((1, efr_ref.shape[1]), f32)

    def body(t, carry):
        fr, fi, rr, ri = carry
        hfr_ref[pl.ds(t, 1), :] = fr
        hfi_ref[pl.ds(t, 1), :] = fi
        er, ei = efr_ref[pl.ds(t, 1), :], efi_ref[pl.ds(t, 1), :]
        fr, fi = lfr * fr - lfi * fi + er, lfr * fi + lfi * fr + ei
        u = jnp.where(t < n_ctx, n_ctx - 1 - t, n_rows - 1 - (t - n_ctx))
        hrr_ref[pl.ds(u, 1), :] = rr
        hri_ref[pl.ds(u, 1), :] = ri
        er, ei = err_ref[pl.ds(u, 1), :], eri_ref[pl.ds(u, 1), :]
        rr, ri = lrr * rr - lri * ri + er, lrr * ri + lri * rr + ei
        return fr, fi, rr, ri

    lax.fori_loop(0, n_rows, body, (zero, zero, zero, zero), unroll=4)


def _s5_scan_call(e_parts, lam_r, n_batch, n_ctx_rows, lane_block=512):
    n_rows_tot, width = e_parts[0].shape
    rows = n_rows_tot // n_batch
    blk = pl.BlockSpec((rows, lane_block), lambda b, j: (b, j))
    out = jax.ShapeDtypeStruct((n_rows_tot, width), f32)
    return pl.pallas_call(
        functools.partial(_s5_scan_kernel, n_ctx=n_ctx_rows),
        grid=(n_batch, width // lane_block),
        in_specs=[blk] * 4 + [pl.BlockSpec((4, lane_block), lambda b, j: (0, j))],
        out_specs=[blk] * 4,
        out_shape=[out] * 4,
        compiler_params=_params("parallel", "parallel"),
        name="s5_scan",
    )(*e_parts, lam_r)


def _s5_read_kernel(u_ref, hfr_ref, hfi_ref, hrr_ref, hri_ref, k_ref, m_ref, d_ref, o_ref, k_s, m_s):
    @pl.when(pl.program_id(1) == 0)
    def _():
        k_s[...] = _expand_octet(k_ref[0], S5_GROUP, S5_GROUP)
        m_s[...] = _expand_octet(m_ref[0], S5_GROUP, S5_STATE)

    parts = _chunk_rows(u_ref)
    x = jnp.concatenate(parts, axis=1).astype(bf16)
    h = jnp.concatenate([hfr_ref[...], hfi_ref[...], hrr_ref[...], hri_ref[...]], axis=1).astype(bf16)
    y = _dot(x, k_s[...]) + _dot(h, m_s[...])
    n_rows = u_ref.shape[0] // S5_CHUNK
    for i in range(S5_CHUNK):
        o_ref[pl.ds(i, n_rows, stride=S5_CHUNK), :] = y[:, i * 128:(i + 1) * 128] + parts[i] * d_ref[...]


def _s5_read_call(z, col_block0, h_parts, k_cmp, m_cmp, d_skip, steps):
    n = z.shape[0]
    n_oct = k_cmp.shape[0]
    sw = S5_OCT * S5_STATE
    rows = steps // S5_CHUNK
    hspec = pl.BlockSpec((rows, sw), lambda k, i: (i, k))
    return pl.pallas_call(
        _s5_read_kernel,
        grid=(n_oct, n // steps),
        in_specs=[pl.BlockSpec((steps, 128), lambda k, i: (i, col_block0 + k))] + [hspec] * 4 +
                 [pl.BlockSpec((1,) + k_cmp.shape[1:], lambda k, i: (k, 0, 0)),
                  pl.BlockSpec((1,) + m_cmp.shape[1:], lambda k, i: (k, 0, 0)),
                  pl.BlockSpec((1, 128), lambda k, i: (0, k))],
        out_specs=pl.BlockSpec((steps, 128), lambda k, i: (i, k)),
        out_shape=jax.ShapeDtypeStruct((n, n_oct * 128), f32),
        scratch_shapes=[pltpu.VMEM((k_cmp.shape[1], k_cmp.shape[2] * S5_OCT), bf16),
                        pltpu.VMEM((m_cmp.shape[1], m_cmp.shape[2] * S5_OCT), bf16)],
        compiler_params=_params("arbitrary", "arbitrary"),
        name="s5_readout",
    )(z, *h_parts, k_cmp, m_cmp, d_skip.reshape(1, -1))


def _top_values(s, k):
    vals = []
    for _ in range(k):
        m = jnp.max(s, axis=0, keepdims=True)
        vals.append(m)
        s = jnp.where(s == m, -jnp.inf, s)
    return vals


def _merge_sort_pairs(n):
    pairs = []

    def merge(lo, hi, r):
        step = r * 2
        if step < hi - lo:
            merge(lo, hi, step)
            merge(lo + r, hi, step)
            pairs.extend((i, i + r) for i in range(lo + r, hi - r, step))
        else:
            pairs.append((lo, lo + r))

    def sort(lo, hi):
        if hi - lo >= 1:
            mid = lo + (hi - lo) // 2
            sort(lo, mid)
            sort(mid + 1, hi)
            merge(lo, hi, 1)

    sort(0, n - 1)
    return pairs


def _top_values_sorted(s, k):
    n_grp = s.shape[0] // 8
    lists = [s[8 * v:8 * v + 8, :] for v in range(n_grp)]
    for i, j in _merge_sort_pairs(n_grp):
        lists[i], lists[j] = jnp.maximum(lists[i], lists[j]), jnp.minimum(lists[i], lists[j])
    lists.append(jnp.full_like(lists[0], -jnp.inf))
    vals = []
    for t in range(k):
        m = jnp.max(lists[0], axis=0, keepdims=True)
        vals.append(m)
        hit = lists[0] == m
        for v in range(min(k - t, n_grp)):
            lists[v] = jnp.where(hit, lists[v + 1], lists[v])
    return vals


def _peer_route_kernel(q_ref, keys_ref, s0_ref, s1_ref, e0_ref, e1_ref, th_ref):
    k = PEER_TOPK
    s0 = _dot_nt(keys_ref[0], q_ref[:, 0:PEER_KEYS])
    s1 = _dot_nt(keys_ref[1], q_ref[:, PEER_KEYS:2 * PEER_KEYS])
    top0, top1 = _top_values_sorted(s0, k + 1), _top_values_sorted(s1, k + 1)
    pad = [jnp.full_like(top0[0], -jnp.inf)] * 7
    t0, t1 = jnp.concatenate(top0 + pad, axis=0), jnp.concatenate(top1 + pad, axis=0)
    cand = jnp.concatenate([top0[0] + t1] + [top0[i] + t1[0:8] for i in range(1, 8)] + [t0[8:24] + top1[0]], axis=0)
    best = _top_values(cand, k + 1)
    z = None
    for v in best[:k]:
        e = jnp.exp(v - best[0])
        z = e if z is None else z + e
    s0_ref[0] = s0.reshape(s0_ref.shape[1:])
    s1_ref[0] = s1
    e0_ref[0] = (jnp.exp(s0 - top0[0]) / z).reshape(e0_ref.shape[1:])
    e1_ref[0] = jnp.exp(s1 - top1[0])
    th_ref[0] = 0.5 * (best[k - 1] + best[k])


def _peer_route_call(q, keys, tt=512):
    n = q.shape[0]
    n_heads = keys.shape[0] // 2
    big = jax.ShapeDtypeStruct((n_heads, PEER_KEYS, n), f32)
    bspec = pl.BlockSpec((1, PEER_KEYS, tt), lambda t, h: (h, 0, t))
    grp = jax.ShapeDtypeStruct((n_heads, PEER_KEYS // 8, 8, n), f32)
    gspec = pl.BlockSpec((1, PEER_KEYS // 8, 8, tt), lambda t, h: (h, 0, 0, t))
    return pl.pallas_call(
        _peer_route_kernel,
        grid=(n // tt, n_heads),
        in_specs=[pl.BlockSpec((tt, 2 * PEER_KEYS), lambda t, h: (t, h)),
                  pl.BlockSpec((2, PEER_KEYS, PEER_KEYS), lambda t, h: (h, 0, 0))],
        out_specs=[gspec, bspec, gspec, bspec, pl.BlockSpec((1, 1, tt), lambda t, h: (h, 0, t))],
        out_shape=[grp, big, grp, big, jax.ShapeDtypeStruct((n_heads, 1, n), f32)],
        compiler_params=_params("parallel", "arbitrary"),
        name="peer_route",
    )(q, keys)


PEER_I_PER_TILE = 4


def _peer_dense_kernel(xm_ref, u_ref, v_ref, s0_ref, e0_ref, s1_ref, e1_ref, th_ref, o_ref, s_s, w_s):
    e = pl.program_id(1)
    n_steps = pl.num_programs(1)

    @pl.when(e == 0)
    def _():
        o_ref[...] = jnp.zeros(o_ref.shape, f32)

    @pl.when(e > 0)
    def _():
        n_heads = s1_ref.shape[0]
        sub = ((e - 1) % (8 // PEER_I_PER_TILE)) * PEER_I_PER_TILE
        for il in range(PEER_I_PER_TILE):
            w = None
            for h in range(n_heads):
                s0 = s0_ref[h, 0, pl.ds(sub + il, 1), :]
                e0 = 0.5 * e0_ref[h, 0, pl.ds(sub + il, 1), :]
                keep = s1_ref[h] >= th_ref[h] - s0
                term = jnp.where(keep, e0 * e1_ref[h], 0.0)
                w = term if w is None else w + term
            w_s[il * PEER_KEYS:(il + 1) * PEER_KEYS, :] = w
        s = s_s[...]
        c = math.sqrt(2.0 / math.pi)
        cdf2 = 1.0 + jnp.tanh(s * (c + (c * 0.044715) * (s * s)))
        act = (s * cdf2 * w_s[...]).T.astype(bf16)
        o_ref[...] += _dot(act, v_ref[...])

    @pl.when(e < n_steps - 1)
    def _():
        s_s[...] = _dot_nt(u_ref[...], xm_ref[...])


def _peer_dense_call(xm, u_tab, v_tab, layer, s0, e0, s1, e1, th, tm=512):
    n, d = xm.shape
    n_exp = u_tab.shape[1]
    te = PEER_I_PER_TILE * PEER_KEYS
    n_tiles = n_exp // te
    n_heads = s1.shape[0]
    per8 = 8 // PEER_I_PER_TILE
    prev = lambda e: jnp.maximum(e - 1, 0)
    row8 = pl.BlockSpec((n_heads, 1, 8, tm), lambda t, e: (0, prev(e) // per8, 0, t))
    full = pl.BlockSpec((n_heads, PEER_KEYS, tm), lambda t, e: (0, 0, t))
    return pl.pallas_call(
        _peer_dense_kernel,
        grid=(n // tm, n_tiles + 1),
        in_specs=[pl.BlockSpec((tm, d), lambda t, e: (t, 0)),
                  pl.BlockSpec((None, te, d), lambda t, e: (layer, jnp.minimum(e, n_tiles - 1), 0)),
                  pl.BlockSpec((None, te, d), lambda t, e: (layer, prev(e), 0)),
                  row8, row8, full, full,
                  pl.BlockSpec((n_heads, 1, tm), lambda t, e: (0, 0, t))],
        out_specs=pl.BlockSpec((tm, d), lambda t, e: (t, 0)),
        out_shape=jax.ShapeDtypeStruct((n, d), f32),
        scratch_shapes=[pltpu.VMEM((te, tm), f32), pltpu.VMEM((te, tm), f32)],
        compiler_params=_params("parallel", "arbitrary", vmem=V7X_VMEM_LIMIT_PEER),
        name="peer_dense",
    )(xm, u_tab, v_tab, s0, e0, s1, e1, th)


def kernel(x, c, ctx, c_ctx, w_ada, b_ada, w_in, a_q_norm, a_k_norm, b_sink, lru_conv_w, lru_conv_b, lru_w_r, lru_b_r, lru_w_i, lru_b_i, lru_lambda, s5_a_re, s5_a_im, s5_log_step, s5_b_re, s5_b_im, s5_c_re, s5_c_im, s5_d, s5_w_glu, s5_b_glu, w_gate, b_gate, w_branch, w_out, ln_g, ln_b, peer_w_q, peer_sub_keys, peer_u, peer_v):
    n_batch, n_lat, d = x.shape
    n_ctx = ctx.shape[1]
    depth = w_ada.shape[0]
    seq_tot = n_ctx + n_lat
    n_tok = n_batch * seq_tot
    mix_w = d // 4
    assert n_ctx % ROW_TILE == 0 and n_lat % ROW_TILE == 0 and n_tok % MM_TILE_M == 0
    assert n_ctx == ROW_TILE, "scan kernels treat row tile 0 of every batch element as the context"
    tiles_per_batch = seq_tot // ROW_TILE
    geom = (tiles_per_batch, n_ctx // ROW_TILE, n_batch)
    alpha = (2.0 * depth) ** 0.25
    s5_steps = seq_tot // 4
    assert seq_tot % 4 == 0 and s5_steps % (8 * S5_CHUNK) == 0 and n_ctx % S5_CHUNK == 0

    cin = jnp.zeros((8, d), f32).at[:n_batch].set(c).at[n_batch].set(c_ctx)
    mod = _ada_call(cin, w_ada, b_ada)

    def mod_vec(l, k):
        return mod[l, :, k * d:(k + 1) * d].reshape(8, 1, d)

    cos, sin = _rope_tables(n_lat, n_ctx)
    w_in_b, w_gate_b, w_branch_b, w_out_b = (w.astype(bf16) for w in (w_in, w_gate, w_branch, w_out))
    w_glu_b, w_q_b, u_tab_b, v_tab_b = (w.astype(bf16) for w in (s5_w_glu, peer_w_q, peer_u, peer_v))
    stream = (x, ctx)
    um = _mod_call(stream, n_tok, mod_vec(0, 1), mod_vec(0, 0), geom)

    qkv_cols = N_QKV_SLICES * HEAD_DIM
    for l in range(depth):
        z = _mm_call(um, w_in_b, l, f32, name="in_proj")
        qkv = _prep_call(z, cos, sin, a_q_norm[l], a_k_norm[l], n_batch, seq_tot)
        ya = _attn_a_call(qkv, n_ctx).reshape(n_tok, mix_w)
        yb = _attn_b_call(qkv, b_sink[l], n_ctx).reshape(n_tok, mix_w)

        lru_x_block = qkv_cols // mix_w
        a_coef, b_coef = _lru_gate_call(
            z, lru_x_block, lru_conv_w[l], lru_conv_b[l], lru_w_r[l].astype(bf16), lru_b_r[l],
            lru_w_i[l].astype(bf16), lru_b_i[l], jax.nn.softplus(-lru_lambda[l]), tiles_per_batch)
        hf, hr = _lru_scan_call(a_coef, b_coef, n_batch, tiles_per_batch)
        yr = _lru_out_call(hf, hr, z, lru_x_block + 1)

        s5_block0 = (qkv_cols + 2 * mix_w) // 128
        k_cmp, w_cmp, m_cmp, lam_r = _s5_weights(s5_a_re[l], s5_a_im[l], s5_log_step[l], s5_b_re[l],
                                                  s5_b_im[l], s5_c_re[l], s5_c_im[l])
        e_parts = _s5_inject_call(z, s5_block0, w_cmp, s5_steps)
        h_parts = _s5_scan_call(e_parts, lam_r, n_batch, n_ctx // S5_CHUNK)
        y_s5 = _s5_read_call(z, s5_block0, h_parts, k_cmp, m_cmp, s5_d[l], s5_steps)
        ys = _glu_call(y_s5, w_glu_b, l, s5_b_glu[l])

        merged = _merge_call(um, (ya, yb, yr, ys), w_gate_b, b_gate[l], w_branch_b, l)
        y = _mm_call(merged, w_out_b, l, f32, name="out_proj")
        xs, um = _ln_call(stream, y, mod_vec(l, 2), ln_g[l, 0], ln_b[l, 0], (mod_vec(l, 4), mod_vec(l, 3)), geom, alpha)

        q = _mm_call(um, w_q_b, l, bf16, name="peer_query")
        n_heads = peer_sub_keys.shape[1]
        keys = peer_sub_keys[l].astype(bf16).reshape(2 * n_heads, PEER_KEYS, -1)
        s0, s1, e0, e1, th = _peer_route_call(q, keys)
        f = _peer_dense_call(um, u_tab_b, v_tab_b, l, s0, e0, s1, e1, th)
        if l == depth - 1:
            return _ln_call((xs,), f, mod_vec(l, 5), ln_g[l, 1], ln_b[l, 1], None, geom, alpha)[0]
        xs, um = _ln_call((xs,), f, mod_vec(l, 5), ln_g[l, 1], ln_b[l, 1],
                          (mod_vec(l + 1, 1), mod_vec(l + 1, 0)), geom, alpha)
        stream = (xs,)
```
